```python
import jax
import jax.numpy as jnp
from jax import lax
import numpy as np

D_MODEL = 1024
BATCH = 8
SEQ = 4096
DEPTH = 4

GRID_W = 64
CTX_LEN = 256
N_MIXERS = 4
HEAD_DIM = 64
GROUP_WIDTH = D_MODEL // N_MIXERS
GROUP_HEADS = GROUP_WIDTH // HEAD_DIM

NA_HEADS = GROUP_HEADS
NA_WIN_ROWS = 8
NA_WIN_COLS = 16

GQA_HEADS = GROUP_HEADS
GQA_KV_HEADS = GROUP_HEADS // 2

MLA_HEADS = GROUP_HEADS
MLA_Q_RANK = D_MODEL // 4
MLA_KV_RANK = D_MODEL // 8
MLA_NOPE_DIM = HEAD_DIM
MLA_ROPE_DIM = HEAD_DIM // 2
MLA_V_DIM = HEAD_DIM

RWKV_HEADS = GROUP_HEADS
RWKV_HEAD_DIM = HEAD_DIM
RWKV_WIDTH = RWKV_HEADS * RWKV_HEAD_DIM
RWKV_DECAY_RANK = 64
RWKV_ICLR_RANK = 64
RWKV_GATE_RANK = 160
SHIFT_TAPS = 3

MLP_HIDDEN = 4 * D_MODEL

NA_COLS = 3 * NA_HEADS * HEAD_DIM
GQA_COLS = (GQA_HEADS + 2 * GQA_KV_HEADS) * HEAD_DIM
MLA_COLS = MLA_Q_RANK + MLA_KV_RANK + MLA_ROPE_DIM
RWKV_COLS = 3 * RWKV_WIDTH + 2 * RWKV_DECAY_RANK + 2 * RWKV_ICLR_RANK + RWKV_GATE_RANK
IN_COLS = NA_COLS + GQA_COLS + MLA_COLS + RWKV_COLS
MIX_WIDTH = NA_HEADS * HEAD_DIM + GQA_HEADS * HEAD_DIM + MLA_HEADS * MLA_V_DIM + RWKV_WIDTH

Q_BLOCK = 128
ROPE_THETA = 10000.0
RMS_EPS = 1e-6
GN_EPS = 64e-5
NEG_INF = -1e30
F32 = jnp.float32

kernel_name = 'hybrid_na_gqa_mla_rwkv7_dit'


def rms_norm(x, g, eps=RMS_EPS):
    xf = x.astype(F32)
    y = xf * lax.rsqrt(jnp.mean(xf * xf, axis=-1, keepdims=True) + eps)
    return (y * g.astype(F32)).astype(x.dtype)


def split_cols(u, sizes):
    return jnp.split(u, np.cumsum(sizes)[:-1].tolist(), axis=-1)


def to_heads(u, n_heads):
    return u.reshape(u.shape[:-1] + (n_heads, u.shape[-1] // n_heads))


def axial_rope_tables(n_tokens, rot_dim):
    t = jnp.arange(n_tokens, dtype=jnp.int32)
    rows = (t // GRID_W).astype(F32)
    cols = (t % GRID_W).astype(F32)
    per_axis = rot_dim // 2
    inv_freq = ROPE_THETA ** (-jnp.arange(0, per_axis, 2, dtype=F32) / per_axis)
    ang = jnp.concatenate([rows[:, None] * inv_freq, cols[:, None] * inv_freq], axis=-1)
    return jnp.cos(ang), jnp.sin(ang)


def apply_rope(x, cos, sin):
    c = cos[None, :, None, :]
    s = sin[None, :, None, :]
    xf = x.astype(F32)
    x1, x2 = xf[..., 0::2], xf[..., 1::2]
    y = jnp.stack([x1 * c - x2 * s, x1 * s + x2 * c], axis=-1)
    return y.reshape(x.shape).astype(x.dtype)


def block_attention(q, k, v):
    bsz, n_q, n_heads, dk = q.shape
    n_kv, dv = k.shape[2], v.shape[-1]
    grp = n_heads // n_kv
    scale = dk ** -0.5
    qb = q.reshape(bsz, n_q // Q_BLOCK, Q_BLOCK, n_kv, grp, dk).swapaxes(0, 1)

    def one_block(qi):
        s = jnp.einsum('bqhgd,bkhd->bhgqk', qi, k).astype(F32) * scale
        p = jax.nn.softmax(s, axis=-1).astype(v.dtype)
        return jnp.einsum('bhgqk,bkhd->bqhgd', p, v)

    o = lax.map(one_block, qb)
    return o.swapaxes(0, 1).reshape(bsz, n_q, n_heads * dv)


def neighbourhood_attention(q, k, v, k_ctx, v_ctx, rpb, rows):
    bsz, n_tok, n_heads, d = q.shape
    wr = min(NA_WIN_ROWS, rows)
    r_idx = np.arange(rows)
    key_rows = np.clip(r_idx - wr // 2, 0, rows - wr)[:, None] + np.arange(wr)[None, :]
    c_idx = np.arange(GRID_W)
    col_start = np.clip(c_idx - NA_WIN_COLS // 2, 0, GRID_W - NA_WIN_COLS)
    col_in_win = (c_idx[None, :] >= col_start[:, None]) & (c_idx[None, :] < col_start[:, None] + NA_WIN_COLS)
    dr = key_rows - r_idx[:, None] + NA_WIN_ROWS - 1
    dc = np.clip(c_idx[None, :] - c_idx[:, None] + NA_WIN_COLS - 1, 0, 2 * NA_WIN_COLS - 2)
    bias = rpb[:, dr[:, None, :, None], dc[None, :, None, :]].astype(F32)
    q_grid = q.reshape(bsz, rows, GRID_W, n_heads, d)
    k_strip = k.reshape(bsz, rows, GRID_W, n_heads, d)[:, key_rows]
    v_strip = v.reshape(bsz, rows, GRID_W, n_heads, d)[:, key_rows]
    scale = d ** -0.5
    s_win = jnp.einsum('brqhd,brjkhd->bhrqjk', q_grid, k_strip).astype(F32) * scale + bias[None]
    s_win = jnp.where(col_in_win[:, None, :], s_win, NEG_INF).reshape(bsz, n_heads, rows, GRID_W, wr * GRID_W)
    s_ctx = jnp.einsum('brqhd,bchd->bhrqc', q_grid, k_ctx).astype(F32) * scale
    p = jax.nn.softmax(jnp.concatenate([s_win, s_ctx], axis=-1), axis=-1).astype(v.dtype)
    p_win = p[..., :wr * GRID_W].reshape(bsz, n_heads, rows, GRID_W, wr, GRID_W)
    p_ctx = p[..., wr * GRID_W:]
    out = (jnp.einsum('bhrqjk,brjkhd->brqhd', p_win, v_strip)
           + jnp.einsum('bhrqc,bchd->brqhd', p_ctx, v_ctx))
    return out.reshape(bsz, n_tok, n_heads * d)


def gqa_queries(u, q_norm_g):
    return rms_norm(to_heads(u[..., :GQA_HEADS * HEAD_DIM], GQA_HEADS), q_norm_g)


def gqa_keys_values(u, k_norm_g):
    k, v = split_cols(u[..., GQA_HEADS * HEAD_DIM:], [GQA_KV_HEADS * HEAD_DIM] * 2)
    return rms_norm(to_heads(k, GQA_KV_HEADS), k_norm_g), to_heads(v, GQA_KV_HEADS)


def mla_queries(u, q_norm_g, w_uq):
    q = to_heads(rms_norm(u[..., :MLA_Q_RANK], q_norm_g) @ w_uq, MLA_HEADS)
    return q[..., :MLA_NOPE_DIM], q[..., MLA_NOPE_DIM:]


def mla_keys_values(u, kv_norm_g, w_ukv):
    c_kv = u[..., MLA_Q_RANK:MLA_Q_RANK + MLA_KV_RANK]
    k_rope = u[..., MLA_Q_RANK + MLA_KV_RANK:][:, :, None, :]
    kv = to_heads(rms_norm(c_kv, kv_norm_g) @ w_ukv, MLA_HEADS)
    return kv[..., :MLA_NOPE_DIM], k_rope, kv[..., MLA_NOPE_DIM:]


def mla_join_key(k_nope, k_rope):
    return jnp.concatenate([k_nope, jnp.broadcast_to(k_rope, k_nope.shape[:-1] + (MLA_ROPE_DIM,))], axis=-1)


def centred_token_shift(u, taps):
    up = jnp.pad(u, ((0, 0), (1, 1), (0, 0)))
    return up[:, :-2] * taps[0] + u * taps[1] + up[:, 2:] * taps[2]


def rwkv7_scan(r, decay, k, v, a_vec, b_vec, s0, reverse):
    seq_first = tuple(jnp.swapaxes(t, 0, 1) for t in (r, decay, k, v, a_vec, b_vec))

    def step(s, inp):
        r_t, w_t, k_t, v_t, a_t, b_t = inp
        sa = jnp.einsum('bhvk,bhk->bhv', s, a_t)
        s = s * w_t[:, :, None, :] + sa[..., None] * b_t[:, :, None, :] + v_t[..., None] * k_t[:, :, None, :]
        return s, jnp.einsum('bhvk,bhk->bhv', s, r_t)

    s_final, ys = lax.scan(step, s0, seq_first, reverse=reverse)
    return s_final, jnp.swapaxes(ys, 0, 1)


def rwkv7_time_mix(u, s0, params, want_out):
    w0, w2, a0, a2, k_k, k_a, r_k, g2, lnx_w, lnx_b = params
    uf = u.astype(F32)
    r, k, v, wl_f, wl_b, al_f, al_b, g_low = split_cols(
        uf, [RWKV_WIDTH] * 3 + [RWKV_DECAY_RANK] * 2 + [RWKV_ICLR_RANK] * 2 + [RWKV_GATE_RANK])
    bsz, n_tok = u.shape[0], u.shape[1]
    if s0 is None:
        zero = jnp.zeros((bsz, RWKV_HEADS, RWKV_HEAD_DIM, RWKV_HEAD_DIM), F32)
        s0 = (zero, zero)
    r_h, v_h = to_heads(r, RWKV_HEADS), to_heads(v, RWKV_HEADS)
    y_sum, keys_dir, finals = 0.0, [], []
    for d, (w_low, a_low) in enumerate(((wl_f, al_f), (wl_b, al_b))):
        w = -jax.nn.softplus(-(w0[d] + jnp.tanh(w_low) @ w2[d])) - 0.5
        decay = to_heads(jnp.exp(-jnp.exp(w)), RWKV_HEADS)
        a = jax.nn.sigmoid(a0[d] + a_low @ a2[d])
        kk = to_heads(k * k_k[d], RWKV_HEADS)
        kk = kk / jnp.maximum(jnp.sqrt(jnp.sum(kk * kk, axis=-1, keepdims=True)), 1e-12)
        k_d = to_heads(k * (1.0 + (a - 1.0) * k_a[d]), RWKV_HEADS)
        s_fin, y = rwkv7_scan(r_h, decay, k_d, v_h, -kk, kk * to_heads(a, RWKV_HEADS), s0[d], d == 1)
        y_sum = y_sum + y
        keys_dir.append(k_d)
        finals.append(s_fin)
    states = (finals[0], finals[1])
    if not want_out:
        return None, states
    mu = jnp.mean(y_sum, axis=-1, keepdims=True)
    var = jnp.mean(jnp.square(y_sum - mu), axis=-1, keepdims=True)
    y = ((y_sum - mu) * lax.rsqrt(var + GN_EPS)).reshape(bsz, n_tok, RWKV_WIDTH)
    bonus = (jnp.sum(r_h * keys_dir[0] * r_k[0], axis=-1, keepdims=True)
             + jnp.sum(r_h * keys_dir[1] * r_k[1], axis=-1, keepdims=True)) * v_h
    y = y * lnx_w + lnx_b + bonus.reshape(bsz, n_tok, RWKV_WIDTH)
    gate = jax.nn.sigmoid(g_low) @ g2
    return (y * gate).astype(u.dtype), states


def hybrid_token_mixer(h_lat, h_ctx, rows, w_in, shift_taps, na_rpb, gqa_q_norm, gqa_k_norm,
                       mla_q_norm, mla_kv_norm, mla_w_uq, mla_w_ukv, rwkv_params,
                       rope_head, rope_mla, want_ctx):
    p_lat, p_ctx = h_lat @ w_in, h_ctx @ w_in
    sizes = [NA_COLS, GQA_COLS, MLA_COLS, RWKV_COLS]
    na_l, gqa_l, mla_l, rw_l = split_cols(p_lat, sizes)
    na_c, gqa_c, mla_c, rw_c = split_cols(p_ctx, sizes)
    cos_h, sin_h = rope_head
    cos_m, sin_m = rope_mla

    a_q, a_k, a_v = [to_heads(t, NA_HEADS) for t in split_cols(na_l, [GROUP_WIDTH] * 3)]
    a_qc, a_kc, a_vc = [to_heads(t, NA_HEADS) for t in split_cols(na_c, [GROUP_WIDTH] * 3)]
    out_a = neighbourhood_attention(a_q, a_k, a_v, a_kc, a_vc, na_rpb, rows)

    b_k, b_v = gqa_keys_values(gqa_l, gqa_k_norm)
    b_kc, b_vc = gqa_keys_values(gqa_c, gqa_k_norm)
    b_q = apply_rope(gqa_queries(gqa_l, gqa_q_norm), cos_h, sin_h)
    b_k_all = jnp.concatenate([apply_rope(b_k, cos_h, sin_h), b_kc], axis=1)
    out_b = block_attention(b_q, b_k_all, jnp.concatenate([b_v, b_vc], axis=1))

    c_qn, c_qr = mla_queries(mla_l, mla_q_norm, mla_w_uq)
    c_kn, c_kr, c_v = mla_keys_values(mla_l, mla_kv_norm, mla_w_ukv)
    c_knc, c_krc, c_vc = mla_keys_values(mla_c, mla_kv_norm, mla_w_ukv)
    c_k_ctx = mla_join_key(c_knc, c_krc)
    c_q = jnp.concatenate([c_qn, apply_rope(c_qr, cos_m, sin_m)], axis=-1)
    c_k_all = jnp.concatenate([mla_join_key(c_kn, apply_rope(c_kr, cos_m, sin_m)), c_k_ctx], axis=1)
    out_c = block_attention(c_q, c_k_all, jnp.concatenate([c_v, c_vc], axis=1))

    out_dc, ctx_states = rwkv7_time_mix(centred_token_shift(rw_c, shift_taps), None, rwkv_params, want_ctx)
    out_d, _ = rwkv7_time_mix(centred_token_shift(rw_l, shift_taps), ctx_states, rwkv_params, True)

    mix_lat = jnp.concatenate([out_a, out_b, out_c, out_d], axis=-1)
    if not want_ctx:
        return mix_lat, None
    out_ac = block_attention(a_qc, a_kc, a_vc)
    out_bc = block_attention(gqa_queries(gqa_c, gqa_q_norm), b_kc, b_vc)
    c_qnc, c_qrc = mla_queries(mla_c, mla_q_norm, mla_w_uq)
    out_cc = block_attention(jnp.concatenate([c_qnc, c_qrc], axis=-1), c_k_ctx, c_vc)
    mix_ctx = jnp.concatenate([out_ac, out_bc, out_cc, out_dc], axis=-1)
    return mix_lat, mix_ctx


def squared_relu_mlp(h, w1, w2):
    return jnp.square(jax.nn.relu(h @ w1)) @ w2


def setup_inputs(seed: int = 0) -> dict:
    key = jax.random.key(seed)
    keys = iter(jax.random.split(key, 40))

    def normal(shape, scale):
        return scale * jax.random.normal(next(keys), shape, F32)

    def gain(shape):
        return 1.0 + normal(shape, 0.05)

    shift_base = jnp.array([0.25, 0.5, 0.25], F32)[None, :, None]
    return {
        'x': normal((BATCH, SEQ, D_MODEL), 1.0),
        'c': normal((BATCH, D_MODEL), 1.0),
        'ctx': normal((BATCH, CTX_LEN, D_MODEL), 1.0),
        'c_ctx': normal((D_MODEL,), 1.0),
        'w_mod': normal((DEPTH, D_MODEL, 6 * D_MODEL), 0.5 * D_MODEL ** -0.5),
        'b_mod': normal((DEPTH, 6 * D_MODEL), 0.01),
        'norm1_g': gain((DEPTH, D_MODEL)),
        'norm2_g': gain((DEPTH, D_MODEL)),
        'w_in': normal((DEPTH, D_MODEL, IN_COLS), D_MODEL ** -0.5),
        'rwkv_shift': shift_base + normal((DEPTH, SHIFT_TAPS, RWKV_COLS), 0.05),
        'na_rpb': normal((DEPTH, NA_HEADS, 2 * NA_WIN_ROWS - 1, 2 * NA_WIN_COLS - 1), 0.3),
        'gqa_q_norm': gain((DEPTH, HEAD_DIM)),
        'gqa_k_norm': gain((DEPTH, HEAD_DIM)),
        'mla_q_norm': gain((DEPTH, MLA_Q_RANK)),
        'mla_kv_norm': gain((DEPTH, MLA_KV_RANK)),
        'mla_w_uq': normal((DEPTH, MLA_Q_RANK, MLA_HEADS * (MLA_NOPE_DIM + MLA_ROPE_DIM)), MLA_Q_RANK ** -0.5),
        'mla_w_ukv': normal((DEPTH, MLA_KV_RANK, MLA_HEADS * (MLA_NOPE_DIM + MLA_V_DIM)), MLA_KV_RANK ** -0.5),
        'rwkv_w0': jax.random.uniform(next(keys), (DEPTH, 2, RWKV_WIDTH), F32, -6.0, -1.0),
        'rwkv_w2': normal((DEPTH, 2, RWKV_DECAY_RANK, RWKV_WIDTH), 0.5 * RWKV_DECAY_RANK ** -0.5),
        'rwkv_a0': normal((DEPTH, 2, RWKV_WIDTH), 0.5),
        'rwkv_a2': normal((DEPTH, 2, RWKV_ICLR_RANK, RWKV_WIDTH), 0.5 * RWKV_ICLR_RANK ** -0.5),
        'rwkv_k_k': 0.85 + normal((DEPTH, 2, RWKV_WIDTH), 0.05),
        'rwkv_k_a': gain((DEPTH, 2, RWKV_WIDTH)),
        'rwkv_r_k': normal((DEPTH, 2, RWKV_HEADS, RWKV_HEAD_DIM), 0.1),
        'rwkv_g2': normal((DEPTH, RWKV_GATE_RANK, RWKV_WIDTH), RWKV_GATE_RANK ** -0.5),
        'rwkv_lnx_w': gain((DEPTH, RWKV_WIDTH)),
        'rwkv_lnx_b': normal((DEPTH, RWKV_WIDTH), 0.01),
        'w_out': normal((DEPTH, MIX_WIDTH, D_MODEL), MIX_WIDTH ** -0.5),
        'w_fc1': normal((DEPTH, D_MODEL, MLP_HIDDEN), D_MODEL ** -0.5),
        'w_fc2': normal((DEPTH, MLP_HIDDEN, D_MODEL), MLP_HIDDEN ** -0.5),
        'final_norm_g': gain((D_MODEL,)),
    }


def reference(x, c, ctx, c_ctx, w_mod, b_mod, norm1_g, norm2_g, w_in, rwkv_shift, na_rpb,
              gqa_q_norm, gqa_k_norm, mla_q_norm, mla_kv_norm, mla_w_uq, mla_w_ukv,
              rwkv_w0, rwkv_w2, rwkv_a0, rwkv_a2, rwkv_k_k, rwkv_k_a, rwkv_r_k, rwkv_g2,
              rwkv_lnx_w, rwkv_lnx_b, w_out, w_fc1, w_fc2, final_norm_g):
    n_lat = x.shape[1]
    rows = n_lat // GRID_W
    rope_head = axial_rope_tables(n_lat, HEAD_DIM)
    rope_mla = axial_rope_tables(n_lat, MLA_ROPE_DIM)
    silu_c = jax.nn.silu(c)
    silu_c_ctx = jax.nn.silu(c_ctx)
    xc = ctx
    for layer in range(DEPTH):
        want_ctx = layer < DEPTH - 1
        mod = silu_c @ w_mod[layer] + b_mod[layer]
        sh1, sc1, g1, sh2, sc2, g2 = jnp.split(mod[:, None, :], 6, axis=-1)
        mod_c = silu_c_ctx @ w_mod[layer] + b_mod[layer]
        csh1, csc1, cg1, csh2, csc2, cg2 = jnp.split(mod_c, 6, axis=-1)
        h_lat = rms_norm(x, norm1_g[layer]) * (1.0 + sc1) + sh1
        h_ctx = rms_norm(xc, norm1_g[layer]) * (1.0 + csc1) + csh1
        rwkv_params = (rwkv_w0[layer], rwkv_w2[layer], rwkv_a0[layer], rwkv_a2[layer],
                       rwkv_k_k[layer], rwkv_k_a[layer], rwkv_r_k[layer], rwkv_g2[layer],
                       rwkv_lnx_w[layer], rwkv_lnx_b[layer])
        mix_lat, mix_ctx = hybrid_token_mixer(
            h_lat, h_ctx, rows, w_in[layer], rwkv_shift[layer], na_rpb[layer],
            gqa_q_norm[layer], gqa_k_norm[layer], mla_q_norm[layer], mla_kv_norm[layer],
            mla_w_uq[layer], mla_w_ukv[layer], rwkv_params, rope_head, rope_mla, want_ctx)
        x = x + g1 * (mix_lat @ w_out[layer])
        h2 = rms_norm(x, norm2_g[layer]) * (1.0 + sc2) + sh2
        x = x + g2 * squared_relu_mlp(h2, w_fc1[layer], w_fc2[layer])
        if want_ctx:
            xc = xc + cg1 * (mix_ctx @ w_out[layer])
            h2c = rms_norm(xc, norm2_g[layer]) * (1.0 + csc2) + csh2
            xc = xc + cg2 * squared_relu_mlp(h2c, w_fc1[layer], w_fc2[layer])
    return rms_norm(x, final_norm_g)
```

```python
import functools

import numpy as np
import jax
import jax.numpy as jnp
from jax import lax
from jax.experimental import pallas as pl
from jax.experimental.pallas import tpu as pltpu

F32 = jnp.float32
BF16 = jnp.bfloat16

GRID_W = 64
HEAD_DIM = 64
N_HEADS = 4
GROUP_WIDTH = N_HEADS * HEAD_DIM
GQA_KV_HEADS = 2
NA_WIN_ROWS = 8
NA_WIN_COLS = 16
MLA_Q_RANK = 256
MLA_KV_RANK = 128
MLA_ROPE_DIM = 32
MLA_QK_DIM = HEAD_DIM + MLA_ROPE_DIM
RWKV_DECAY_RANK = 64
RWKV_ICLR_RANK = 64
RWKV_GATE_RANK = 160
NA_COLS = 3 * GROUP_WIDTH
GQA_COLS = (N_HEADS + 2 * GQA_KV_HEADS) * HEAD_DIM
MLA_COLS = MLA_Q_RANK + MLA_KV_RANK + MLA_ROPE_DIM
RWKV_COLS = 3 * GROUP_WIDTH + 2 * RWKV_DECAY_RANK + 2 * RWKV_ICLR_RANK + RWKV_GATE_RANK
ROPE_THETA = 10000.0
RMS_EPS = 1e-6
GN_EPS = 64e-5
NEG_INF = -1e30

TOKEN_TILE = 256
ATTN_Q_TILE = 256
NA_Q_ROWS = ATTN_Q_TILE // GRID_W
NA_KEY_ROWS = NA_Q_ROWS + NA_WIN_ROWS - 1
RWKV_CHUNK = 64
VMEM_LIMIT = 56 * 1024 * 1024


def _params(*sem):
    return pltpu.CompilerParams(dimension_semantics=sem, vmem_limit_bytes=VMEM_LIMIT)


def _bdot(a, b):
    return jnp.dot(a.astype(BF16), b.astype(BF16), preferred_element_type=F32)


def _bdot_nt(a, b):
    return lax.dot_general(a.astype(BF16), b.astype(BF16), (((1,), (1,)), ((), ())),
                           preferred_element_type=F32)


def _bdot_tn(a, b):
    return lax.dot_general(a.astype(BF16), b.astype(BF16), (((0,), (0,)), ((), ())),
                           preferred_element_type=F32)


def _matmul_kernel(a_ref, b_ref, o_ref):
    o_ref[...] = _bdot(a_ref[...], b_ref[...]).astype(o_ref.dtype)


def matmul(a, b, out_dtype=F32, tm=1024):
    m, k = a.shape
    n = b.shape[1]
    tm = min(tm, m)
    while m % tm:
        tm //= 2
    return pl.pallas_call(
        _matmul_kernel,
        grid=(m // tm,),
        in_specs=[pl.BlockSpec((tm, k), lambda i: (i, 0)),
                  pl.BlockSpec((k, n), lambda i: (0, 0))],
        out_specs=pl.BlockSpec((tm, n), lambda i: (i, 0)),
        out_shape=jax.ShapeDtypeStruct((m, n), out_dtype),
        compiler_params=_params("parallel"),
        name="matmul",
    )(a, b)


def _mod_kernel(a_ref, w_ref, b_ref, o_ref):
    o_ref[0] = _bdot(a_ref[...], w_ref[0]) + b_ref[0]


def modulation(cond, w_mod, b_mod, tn=1536):
    depth, d, n = w_mod.shape
    r = cond.shape[0]
    return pl.pallas_call(
        _mod_kernel,
        grid=(depth, n // tn),
        in_specs=[pl.BlockSpec((r, d), lambda l, j: (0, 0)),
                  pl.BlockSpec((1, d, tn), lambda l, j: (l, 0, j)),
                  pl.BlockSpec((1, 1, tn), lambda l, j: (l, 0, j))],
        out_specs=pl.BlockSpec((1, r, tn), lambda l, j: (l, 0, j)),
        out_shape=jax.ShapeDtypeStruct((depth, r, n), F32),
        compiler_params=_params("parallel", "parallel"),
        name="modulation",
    )(cond, w_mod, b_mod)


def _rms(x, g):
    return x * lax.rsqrt(jnp.mean(x * x, axis=-1, keepdims=True) + RMS_EPS) * g


def _in_proj_kernel(x_ref, mod_ref, g_ref, wna_ref, wgqa_ref, wmla_ref, wrw_ref,
                    na_ref, gqa_ref, mla_ref, rw_ref):
    mod = mod_ref[0, 0]
    h = _rms(x_ref[0], g_ref[...]) * (1.0 + mod[1:2]) + mod[0:1]
    hb = h.astype(BF16)
    na_ref[0] = jnp.dot(hb, wna_ref[...], preferred_element_type=F32).astype(na_ref.dtype)
    gqa_ref[0] = jnp.dot(hb, wgqa_ref[...], preferred_element_type=F32)
    mla_ref[0] = jnp.dot(hb, wmla_ref[...], preferred_element_type=F32)
    rw_ref[0] = jnp.dot(hb, wrw_ref[...], preferred_element_type=F32)


def in_projection(x_all, mod, g, w_in, ctx_tiles):
    bsz, n_tok, d = x_all.shape
    tm = TOKEN_TILE
    bounds = np.cumsum([0, NA_COLS, GQA_COLS, MLA_COLS, RWKV_COLS])
    ws = [w_in[:, bounds[i]:bounds[i + 1]] for i in range(4)]
    tok = lambda b, i: (b, i, 0)
    full = lambda b, i: (0, 0)
    return pl.pallas_call(
        _in_proj_kernel,
        grid=(bsz, n_tok // tm),
        in_specs=[pl.BlockSpec((1, tm, d), tok),
                  pl.BlockSpec((1, 1, 6, d), lambda b, i: (b, jnp.where(i < ctx_tiles, 0, 1), 0, 0)),
                  pl.BlockSpec((1, d), full)]
                 + [pl.BlockSpec(w.shape, full) for w in ws],
        out_specs=[pl.BlockSpec((1, tm, w.shape[1]), tok) for w in ws],
        out_shape=[jax.ShapeDtypeStruct((bsz, n_tok, NA_COLS), BF16)]
                  + [jax.ShapeDtypeStruct((bsz, n_tok, w.shape[1]), F32) for w in ws[1:]],
        compiler_params=_params("parallel", "parallel"),
        name="in_projection",
    )(x_all, mod, g, *ws)


def _out_mlp_kernel(x_ref, mix_ref, mod_ref, g2_ref, gf_ref, wout_ref, w1_ref, w2_ref, o_ref,
                    *, hidden_tile, final_norm):
    mod = mod_ref[0, 0]
    x1 = x_ref[0] + mod[2:3] * jnp.dot(mix_ref[0], wout_ref[...], preferred_element_type=F32)
    hb = (_rms(x1, g2_ref[...]) * (1.0 + mod[4:5]) + mod[3:4]).astype(BF16)
    acc = jnp.zeros_like(x1)
    for j in range(w1_ref.shape[1] // hidden_tile):
        cols = pl.ds(j * hidden_tile, hidden_tile)
        u = jnp.maximum(jnp.dot(hb, w1_ref[:, cols], preferred_element_type=F32), 0.0)
        acc = acc + jnp.dot((u * u).astype(BF16), w2_ref[cols, :], preferred_element_type=F32)
    x2 = x1 + mod[5:6] * acc
    if final_norm:
        x2 = _rms(x2, gf_ref[...])
    o_ref[0] = x2


def out_projection_mlp(x_all, mix, mod, g2, g_final, w_out, w_fc1, w_fc2, ctx_tiles, final_norm):
    bsz, n_tok, d = x_all.shape
    tm = TOKEN_TILE
    tok = lambda b, i: (b, i, 0)
    full = lambda b, i: (0, 0)
    kern = functools.partial(_out_mlp_kernel, hidden_tile=1024, final_norm=final_norm)
    return pl.pallas_call(
        kern,
        grid=(bsz, n_tok // tm),
        in_specs=[pl.BlockSpec((1, tm, d), tok),
                  pl.BlockSpec((1, tm, mix.shape[2]), tok),
                  pl.BlockSpec((1, 1, 6, d), lambda b, i: (b, jnp.where(i < ctx_tiles, 0, 1), 0, 0)),
                  pl.BlockSpec((1, d), full),
                  pl.BlockSpec((1, d), full),
                  pl.BlockSpec(w_out.shape, full, pipeline_mode=pl.Buffered(1)),
                  pl.BlockSpec(w_fc1.shape, full, pipeline_mode=pl.Buffered(1)),
                  pl.BlockSpec(w_fc2.shape, full, pipeline_mode=pl.Buffered(1))],
        out_specs=pl.BlockSpec((1, tm, d), tok),
        out_shape=jax.ShapeDtypeStruct((bsz, n_tok, d), F32),
        compiler_params=_params("parallel", "parallel"),
        name="out_projection_mlp",
    )(x_all, mix, mod, g2, g_final, w_out, w_fc1, w_fc2)


def _attention_kernel(q_ref, k_ref, v_ref, o_ref):
    grp, tq, dk = q_ref.shape[2], q_ref.shape[3], q_ref.shape[4]
    q = q_ref[0, 0].reshape(grp * tq, dk)
    s = _bdot_nt(q, k_ref[0, 0])
    p = jnp.exp(s - jnp.max(s, axis=-1, keepdims=True))
    o = jnp.dot(p.astype(BF16), v_ref[0, 0], preferred_element_type=F32)
    o = o / jnp.sum(p, axis=-1, keepdims=True)
    o_ref[0, 0] = o.reshape(grp, tq, o.shape[-1]).astype(o_ref.dtype)


def attention(q, k, v, q_tile0, n_q_tiles, n_keys):
    bsz, hk, grp, _, dk = q.shape
    dv = v.shape[-1]
    tq = ATTN_Q_TILE
    return pl.pallas_call(
        _attention_kernel,
        grid=(bsz, hk, n_q_tiles),
        in_specs=[pl.BlockSpec((1, 1, grp, tq, dk), lambda b, h, i: (b, h, 0, i + q_tile0, 0)),
                  pl.BlockSpec((1, 1, n_keys, dk), lambda b, h, i: (b, h, 0, 0)),
                  pl.BlockSpec((1, 1, n_keys, dv), lambda b, h, i: (b, h, 0, 0))],
        out_specs=pl.BlockSpec((1, 1, grp, tq, dv), lambda b, h, i: (b, h, 0, i, 0)),
        out_shape=jax.ShapeDtypeStruct((bsz, hk, grp, n_q_tiles * tq, dv), BF16),
        compiler_params=_params("parallel", "parallel", "parallel"),
        name="attention",
    )(q, k, v)


def _na_kernel(q_ref, k_ref, v_ref, bias_ref, o_ref, *, n_ctx, n_row_tiles):
    i = pl.program_id(2)
    n_win = NA_KEY_ROWS * GRID_W
    max_row0 = n_row_tiles * NA_Q_ROWS - NA_KEY_ROWS
    row0 = jnp.clip(i * NA_Q_ROWS - NA_WIN_ROWS // 2, 0, max_row0)
    start = pl.multiple_of(n_ctx + row0 * GRID_W, GRID_W)
    q = q_ref[0, 0]
    s_win = _bdot_nt(q, k_ref[0, 0, pl.ds(start, n_win), :]) + bias_ref[0, 0]
    s_ctx = _bdot_nt(q, k_ref[0, 0, pl.ds(0, n_ctx), :])
    m = jnp.maximum(jnp.max(s_win, axis=-1, keepdims=True), jnp.max(s_ctx, axis=-1, keepdims=True))
    p_win = jnp.exp(s_win - m)
    p_ctx = jnp.exp(s_ctx - m)
    denom = jnp.sum(p_win, axis=-1, keepdims=True) + jnp.sum(p_ctx, axis=-1, keepdims=True)
    o = (jnp.dot(p_win.astype(BF16), v_ref[0, 0, pl.ds(start, n_win), :], preferred_element_type=F32)
         + jnp.dot(p_ctx.astype(BF16), v_ref[0, 0, pl.ds(0, n_ctx), :], preferred_element_type=F32))
    o_ref[0, 0] = (o / denom).astype(o_ref.dtype)


def na_bias_table(rpb, rows):
    n_tiles = rows // NA_Q_ROWS
    tables = []
    for tile in (0, 1, n_tiles - 1):
        row0 = int(np.clip(tile * NA_Q_ROWS - NA_WIN_ROWS // 2, 0, rows - NA_KEY_ROWS))
        qr = tile * NA_Q_ROWS + np.arange(NA_Q_ROWS)[:, None, None, None]
        qc = np.arange(GRID_W)[None, :, None, None]
        kr = row0 + np.arange(NA_KEY_ROWS)[None, None, :, None]
        kc = np.arange(GRID_W)[None, None, None, :]
        r_start = np.clip(qr - NA_WIN_ROWS // 2, 0, rows - NA_WIN_ROWS)
        c_start = np.clip(qc - NA_WIN_COLS // 2, 0, GRID_W - NA_WIN_COLS)
        valid = ((kr >= r_start) & (kr < r_start + NA_WIN_ROWS)
                 & (kc >= c_start) & (kc < c_start + NA_WIN_COLS))
        dr = np.clip(kr - qr + NA_WIN_ROWS - 1, 0, 2 * NA_WIN_ROWS - 2)
        dc = np.clip(kc - qc + NA_WIN_COLS - 1, 0, 2 * NA_WIN_COLS - 2)
        shape = (NA_Q_ROWS, GRID_W, NA_KEY_ROWS, GRID_W)
        dr, dc, valid = (np.broadcast_to(t, shape) for t in (dr, dc, valid))
        bias = jnp.where(valid[None], rpb[:, dr, dc].astype(F32), NEG_INF)
        tables.append(bias.reshape(rpb.shape[0], NA_Q_ROWS * GRID_W, NA_KEY_ROWS * GRID_W))
    return jnp.stack(tables, axis=1)


def neighbourhood_attention(q, k, v, bias, n_ctx):
    bsz, n_heads, n_tok, d = q.shape
    tq = ATTN_Q_TILE
    n_tiles = (n_tok - n_ctx) // tq
    ctx_tiles = n_ctx // tq
    kern = functools.partial(_na_kernel, n_ctx=n_ctx, n_row_tiles=n_tiles)

    def bias_map(b, h, i):
        return (h, jnp.where(i == 0, 0, jnp.where(i == n_tiles - 1, 2, 1)), 0, 0)

    return pl.pallas_call(
        kern,
        grid=(bsz, n_heads, n_tiles),
        in_specs=[pl.BlockSpec((1, 1, tq, d), lambda b, h, i: (b, h, i + ctx_tiles, 0)),
                  pl.BlockSpec((1, 1, n_tok, d), lambda b, h, i: (b, h, 0, 0)),
                  pl.BlockSpec((1, 1, n_tok, d), lambda b, h, i: (b, h, 0, 0)),
                  pl.BlockSpec((1, 1) + bias.shape[2:], bias_map)],
        out_specs=pl.BlockSpec((1, 1, tq, d), lambda b, h, i: (b, h, i, 0)),
        out_shape=jax.ShapeDtypeStruct((bsz, n_heads, n_tok - n_ctx, d), BF16),
        compiler_params=_params("parallel", "parallel", "parallel"),
        name="neighbourhood_attention",
    )(q, k, v, bias)


def _rwkv_kernel(r_ref, lw_ref, k_ref, v_ref, a_ref, b_ref, y_ref, h_ref):
    n = HEAD_DIM
    c = r_ref.shape[1]

    @pl.when(pl.program_id(1) == 0)
    def _():
        h_ref[...] = jnp.zeros_like(h_ref)

    row = lax.broadcasted_iota(jnp.int32, (c, c), 0)
    col = lax.broadcasted_iota(jnp.int32, (c, c), 1)
    incl = row >= col
    strict = row > col
    lw = lw_ref[0]
    cum = jnp.dot(incl.astype(F32), lw, precision=lax.Precision.HIGHEST,
                  preferred_element_type=F32)
    total = cum[c - 1:c]
    g_inv = jnp.exp(-cum)
    g_end = jnp.exp(total - cum)
    r_t = r_ref[0] * jnp.exp(cum)
    a_t = a_ref[0] * jnp.exp(cum - lw)
    b_t = b_ref[0] * g_inv
    k_t = k_ref[0] * g_inv
    b_e = b_ref[0] * g_end
    k_e = k_ref[0] * g_end
    g_tot = jnp.exp(total)
    v_all = v_ref[0]
    eye = (row == col).astype(F32)
    outs = []
    for h in range(N_HEADS):
        sl = slice(h * n, (h + 1) * n)
        rt, at, bt, kt, v = r_t[:, sl], a_t[:, sl], b_t[:, sl], k_t[:, sl], v_all[:, sl]
        a_ab = jnp.where(strict, _bdot_nt(at, bt), 0.0)
        a_ak = jnp.where(strict, _bdot_nt(at, kt), 0.0)
        a_rb = jnp.where(incl, _bdot_nt(rt, bt), 0.0)
        a_rk = jnp.where(incl, _bdot_nt(rt, kt), 0.0)
        rb, cb = row, col
        inv = eye + jnp.where((rb == cb + 1) & ((rb & 1) == 1), a_ab, 0.0)
        size = 2
        while size < c:
            rb, cb = rb >> 1, cb >> 1
            off = jnp.where((rb == cb + 1) & ((rb & 1) == 1), a_ab, 0.0)
            inv = inv + _bdot(_bdot(inv, off), inv)
            size *= 2
        w = _bdot(inv, at)
        u0 = _bdot(inv, _bdot(a_ak, v))
        q_eff = rt + _bdot(a_rb, w)
        y0 = _bdot(a_rb, u0) + _bdot(a_rk, v)
        s0 = h_ref[h]
        outs.append(_bdot_nt(q_eff, s0) + y0)
        u = _bdot_nt(w, s0) + u0
        h_ref[h] = g_tot[:, sl] * s0 + _bdot_tn(u, b_e[:, sl]) + _bdot_tn(v, k_e[:, sl])
    y_ref[0] = jnp.concatenate(outs, axis=-1)


def rwkv_scan(r, lw, k, v, a, b):
    n_seq, n_tok, width = r.shape
    c = RWKV_CHUNK
    spec = pl.BlockSpec((1, c, width), lambda s, i: (s, i, 0))
    return pl.pallas_call(
        _rwkv_kernel,
        grid=(n_seq, n_tok // c),
        in_specs=[spec] * 6,
        out_specs=spec,
        out_shape=jax.ShapeDtypeStruct((n_seq, n_tok, width), F32),
        scratch_shapes=[pltpu.VMEM((N_HEADS, HEAD_DIM, HEAD_DIM), F32)],
        compiler_params=_params("parallel", "arbitrary"),
        name="rwkv_scan",
    )(r, lw, k, v, a, b)


def _rope_tables(n_ctx, n_lat, rot_dim):
    t = jnp.arange(n_lat, dtype=jnp.int32)
    rows = (t // GRID_W).astype(F32)
    cols = (t % GRID_W).astype(F32)
    per_axis = rot_dim // 2
    inv_freq = ROPE_THETA ** (-jnp.arange(0, per_axis, 2, dtype=F32) / per_axis)
    ang = jnp.concatenate([rows[:, None] * inv_freq, cols[:, None] * inv_freq], axis=-1)
    ang = jnp.concatenate([jnp.zeros((n_ctx, ang.shape[1]), F32), ang], axis=0)
    return jnp.cos(ang), jnp.sin(ang)


def _rope(x, cos, sin):
    c = cos[None, :, None, :]
    s = sin[None, :, None, :]
    x1, x2 = x[..., 0::2], x[..., 1::2]
    return jnp.stack([x1 * c - x2 * s, x1 * s + x2 * c], axis=-1).reshape(x.shape)


def _heads(u, n_heads):
    return u.reshape(u.shape[:-1] + (n_heads, u.shape[-1] // n_heads))


def _head_major(u):
    return jnp.swapaxes(u, 1, 2)


def _segment_map(fn, u, n_ctx):
    return jnp.concatenate([fn(u[:, :n_ctx]), fn(u[:, n_ctx:])], axis=1)


def _token_shift(u, taps):
    up = jnp.pad(u, ((0, 0), (1, 1), (0, 0)))
    return up[:, :-2] * taps[0] + u * taps[1] + up[:, 2:] * taps[2]


def _tok_matmul(u, w, out_dtype=F32):
    lead = u.shape[:-1]
    return matmul(u.reshape(-1, u.shape[-1]), w.astype(BF16), out_dtype).reshape(lead + (w.shape[1],))


def _rwkv_mixer(rw, n_ctx, shift_taps, w0, w2, a0, a2, k_k, k_a, r_k, g2, lnx_w, lnx_b):
    bsz, n_tok, _ = rw.shape
    u = _segment_map(lambda t: _token_shift(t, shift_taps), rw, n_ctx)
    sizes = [GROUP_WIDTH] * 3 + [RWKV_DECAY_RANK] * 2 + [RWKV_ICLR_RANK] * 2 + [RWKV_GATE_RANK]
    r, k, v, wl_f, wl_b, al_f, al_b, g_low = jnp.split(u, np.cumsum(sizes)[:-1].tolist(), axis=-1)
    flip = lambda t: _segment_map(lambda s: jnp.flip(s, axis=1), t, n_ctx)
    streams = [[] for _ in range(6)]
    keys_dir = []
    for d, (w_low, a_low) in enumerate(((wl_f, al_f), (wl_b, al_b))):
        w = -jax.nn.softplus(-(w0[d] + _tok_matmul(jnp.tanh(w_low), w2[d]))) - 0.5
        log_decay = -jnp.exp(w)
        a = jax.nn.sigmoid(a0[d] + _tok_matmul(a_low, a2[d]))
        kk = _heads(k * k_k[d], N_HEADS)
        kk = kk / jnp.maximum(jnp.sqrt(jnp.sum(kk * kk, axis=-1, keepdims=True)), 1e-12)
        kk = kk.reshape(k.shape)
        k_d = k * (1.0 + (a - 1.0) * k_a[d])
        keys_dir.append(k_d)
        order = flip if d == 1 else (lambda t: t)
        for lst, t in zip(streams, (r, log_decay, k_d, v, -kk, kk * a)):
            lst.append(order(t))
    y = rwkv_scan(*[jnp.concatenate(s, axis=0) for s in streams])
    y_sum = _heads(y[:bsz] + flip(y[bsz:]), N_HEADS)
    mu = jnp.mean(y_sum, axis=-1, keepdims=True)
    var = jnp.mean(jnp.square(y_sum - mu), axis=-1, keepdims=True)
    yn = ((y_sum - mu) * lax.rsqrt(var + GN_EPS)).reshape(bsz, n_tok, GROUP_WIDTH)
    r_h, v_h = _heads(r, N_HEADS), _heads(v, N_HEADS)
    bonus = (jnp.sum(r_h * _heads(keys_dir[0], N_HEADS) * r_k[0], axis=-1, keepdims=True)
             + jnp.sum(r_h * _heads(keys_dir[1], N_HEADS) * r_k[1], axis=-1, keepdims=True)) * v_h
    yn = yn * lnx_w + lnx_b + bonus.reshape(bsz, n_tok, GROUP_WIDTH)
    gate = _tok_matmul(jax.nn.sigmoid(g_low), g2)
    return (yn * gate).astype(BF16)


def _head_rms(u, g):
    return u * lax.rsqrt(jnp.mean(u * u, axis=-1, keepdims=True) + RMS_EPS) * g


def _merge_heads(o_lat, o_ctx):
    bsz, n_heads, n_lat, d = o_lat.shape
    if o_ctx is None:
        o_ctx = jnp.zeros((bsz, n_heads, 0, d), o_lat.dtype)
    o = jnp.swapaxes(jnp.concatenate([o_ctx, o_lat], axis=2), 1, 2)
    return o.reshape(bsz, o.shape[1], n_heads * d)


def _dense_branch(q, k, v, n_ctx, want_ctx):
    bsz, hk, grp, n_tok, _ = q.shape
    ctx_tiles = n_ctx // ATTN_Q_TILE
    o_lat = attention(q, k, v, ctx_tiles, (n_tok - n_ctx) // ATTN_Q_TILE, n_tok)
    o_lat = o_lat.reshape(bsz, hk * grp, n_tok - n_ctx, -1)
    o_ctx = None
    if want_ctx:
        o_ctx = attention(q, k, v, 0, ctx_tiles, n_ctx).reshape(bsz, hk * grp, n_ctx, -1)
    return o_lat, o_ctx


def kernel(x, c, ctx, c_ctx, w_mod, b_mod, norm1_g, norm2_g, w_in, rwkv_shift, na_rpb, gqa_q_norm, gqa_k_norm, mla_q_norm, mla_kv_norm, mla_w_uq, mla_w_ukv, rwkv_w0, rwkv_w2, rwkv_a0, rwkv_a2, rwkv_k_k, rwkv_k_a, rwkv_r_k, rwkv_g2, rwkv_lnx_w, rwkv_lnx_b, w_out, w_fc1, w_fc2, final_norm_g):
    bsz, n_lat, d_model = x.shape
    n_ctx = ctx.shape[1]
    depth = w_mod.shape[0]
    rows = n_lat // GRID_W
    ctx_tiles = n_ctx // TOKEN_TILE
    assert n_ctx % TOKEN_TILE == 0 and n_lat % TOKEN_TILE == 0 and rows % NA_Q_ROWS == 0

    cos_h, sin_h = _rope_tables(n_ctx, n_lat, HEAD_DIM)
    cos_m, sin_m = _rope_tables(n_ctx, n_lat, MLA_ROPE_DIM)

    cond = jnp.concatenate([jax.nn.silu(c), jax.nn.silu(c_ctx)[None]], axis=0)
    n_cond = -(-cond.shape[0] // 8) * 8
    cond = jnp.pad(cond, ((0, n_cond - cond.shape[0]), (0, 0)))
    mod_all = modulation(cond, w_mod, b_mod[:, None, :])

    x_all = jnp.concatenate([ctx, x], axis=1)
    qk_scale = HEAD_DIM ** -0.5
    mla_scale = MLA_QK_DIM ** -0.5

    for layer in range(depth):
        want_ctx = layer < depth - 1
        mod = mod_all[layer].reshape(n_cond, 6, d_model)
        mod = jnp.stack([jnp.broadcast_to(mod[bsz], (bsz, 6, d_model)), mod[:bsz]], axis=1)

        na, gqa, mla, rw = in_projection(x_all, mod, norm1_g[layer][None], w_in[layer].astype(BF16), ctx_tiles)

        a_q, a_k, a_v = [_head_major(_heads(t, N_HEADS)) for t in jnp.split(na, 3, axis=-1)]
        a_q = a_q * jnp.asarray(qk_scale, BF16)
        out_a = neighbourhood_attention(a_q, a_k, a_v, na_bias_table(na_rpb[layer], rows), n_ctx)
        out_ac = None
        if want_ctx:
            out_ac = attention(a_q[:, :, None], a_k, a_v, 0, ctx_tiles, n_ctx)[:, :, 0]
        mix_a = _merge_heads(out_a, out_ac)

        b_q = _head_rms(_heads(gqa[..., :GROUP_WIDTH], N_HEADS), gqa_q_norm[layer])
        b_k, b_v = jnp.split(gqa[..., GROUP_WIDTH:], 2, axis=-1)
        b_k = _head_rms(_heads(b_k, GQA_KV_HEADS), gqa_k_norm[layer])
        b_q = _head_major(_rope(b_q, cos_h, sin_h) * qk_scale).astype(BF16)
        b_q = b_q.reshape(bsz, GQA_KV_HEADS, N_HEADS // GQA_KV_HEADS, b_q.shape[2], HEAD_DIM)
        b_k = _head_major(_rope(b_k, cos_h, sin_h)).astype(BF16)
        b_v = _head_major(_heads(b_v, GQA_KV_HEADS)).astype(BF16)
        mix_b = _merge_heads(*_dense_branch(b_q, b_k, b_v, n_ctx, want_ctx))

        c_q = _tok_matmul(_head_rms(mla[..., :MLA_Q_RANK], mla_q_norm[layer]), mla_w_uq[layer])
        c_q = _heads(c_q, N_HEADS)
        c_q = jnp.concatenate([c_q[..., :HEAD_DIM], _rope(c_q[..., HEAD_DIM:], cos_m, sin_m)], axis=-1)
        c_q = _head_major(c_q * mla_scale).astype(BF16)[:, :, None]
        c_kv = _tok_matmul(_head_rms(mla[..., MLA_Q_RANK:MLA_Q_RANK + MLA_KV_RANK], mla_kv_norm[layer]),
                           mla_w_ukv[layer])
        c_kv = _heads(c_kv, N_HEADS)
        c_kr = _rope(mla[..., MLA_Q_RANK + MLA_KV_RANK:][:, :, None, :], cos_m, sin_m)
        c_k = jnp.concatenate([c_kv[..., :HEAD_DIM],
                               jnp.broadcast_to(c_kr, c_kv.shape[:-1] + (MLA_ROPE_DIM,))], axis=-1)
        c_k = _head_major(c_k).astype(BF16)
        c_v = _head_major(c_kv[..., HEAD_DIM:]).astype(BF16)
        mix_c = _merge_heads(*_dense_branch(c_q, c_k, c_v, n_ctx, want_ctx))

        mix_d = _rwkv_mixer(rw, n_ctx, rwkv_shift[layer], rwkv_w0[layer], rwkv_w2[layer], rwkv_a0[layer],
                            rwkv_a2[layer], rwkv_k_k[layer], rwkv_k_a[layer], rwkv_r_k[layer],
                            rwkv_g2[layer], rwkv_lnx_w[layer], rwkv_lnx_b[layer])

        if not want_ctx:
            pad = lambda t: jnp.pad(t, ((0, 0), (n_ctx, 0), (0, 0)))
            mix_a, mix_b, mix_c = pad(mix_a), pad(mix_b), pad(mix_c)
        mix = jnp.concatenate([mix_a, mix_b, mix_c, mix_d], axis=-1)

        x_all = out_projection_mlp(x_all, mix, mod, norm2_g[layer][None], final_norm_g[None],
                                   w_out[layer].astype(BF16), w_fc1[layer].astype(BF16),
                                   w_fc2[layer].astype(BF16), ctx_tiles, layer == depth - 1)
    return x_all[:, n_ctx:]
```

```python
import functools

import numpy as np
import jax
import jax.numpy as jnp
from jax import lax
from jax.experimental import pallas as pl
from jax.experimental.pallas import tpu as pltpu

F32 = jnp.float32
BF16 = jnp.bfloat16

GRID_W = 64
HEAD_DIM = 64
N_HEADS = 4
GROUP_WIDTH = N_HEADS * HEAD_DIM
GQA_KV_HEADS = 2
NA_WIN_ROWS = 8
NA_WIN_COLS = 16
MLA_Q_RANK = 256
MLA_KV_RANK = 128
MLA_ROPE_DIM = 32
MLA_QK_DIM = HEAD_DIM + MLA_ROPE_DIM
RWKV_DECAY_RANK = 64
RWKV_ICLR_RANK = 64
RWKV_GATE_RANK = 160
NA_COLS = 3 * GROUP_WIDTH
GQA_COLS = (N_HEADS + 2 * GQA_KV_HEADS) * HEAD_DIM
MLA_COLS = MLA_Q_RANK + MLA_KV_RANK + MLA_ROPE_DIM
RWKV_COLS = 3 * GROUP_WIDTH + 2 * RWKV_DECAY_RANK + 2 * RWKV_ICLR_RANK + RWKV_GATE_RANK
ROPE_THETA = 10000.0
RMS_EPS = 1e-6
GN_EPS = 64e-5
NEG_INF = -1e30

TOKEN_TILE = 256
ATTN_Q_TILE = 256
NA_Q_ROWS = ATTN_Q_TILE // GRID_W
NA_KEY_ROWS = NA_Q_ROWS + NA_WIN_ROWS - 1
RWKV_CHUNK = 64
VMEM_LIMIT = 56 * 1024 * 1024


def _params(*sem):
    return pltpu.CompilerParams(dimension_semantics=sem, vmem_limit_bytes=VMEM_LIMIT)


def _bdot(a, b):
    return jnp.dot(a.astype(BF16), b.astype(BF16), preferred_element_type=F32)


def _bdot_nt(a, b):
    return lax.dot_general(a.astype(BF16), b.astype(BF16), (((1,), (1,)), ((), ())),
                           preferred_element_type=F32)


def _bdot_tn(a, b):
    return lax.dot_general(a.astype(BF16), b.astype(BF16), (((0,), (0,)), ((), ())),
                           preferred_element_type=F32)


def _matmul_kernel(a_ref, b_ref, o_ref):
    o_ref[...] = _bdot(a_ref[...], b_ref[...]).astype(o_ref.dtype)


def matmul(a, b, out_dtype=F32, tm=1024):
    m, k = a.shape
    n = b.shape[1]
    tm = min(tm, m)
    while m % tm:
        tm //= 2
    return pl.pallas_call(
        _matmul_kernel,
        grid=(m // tm,),
        in_specs=[pl.BlockSpec((tm, k), lambda i: (i, 0)),
                  pl.BlockSpec((k, n), lambda i: (0, 0))],
        out_specs=pl.BlockSpec((tm, n), lambda i: (i, 0)),
        out_shape=jax.ShapeDtypeStruct((m, n), out_dtype),
        compiler_params=_params("parallel"),
        name="matmul",
    )(a, b)


def _mod_kernel(a_ref, w_ref, b_ref, o_ref):
    o_ref[0] = _bdot(a_ref[...], w_ref[0]) + b_ref[0]


def modulation(cond, w_mod, b_mod, tn=1536):
    depth, d, n = w_mod.shape
    r = cond.shape[0]
    return pl.pallas_call(
        _mod_kernel,
        grid=(depth, n // tn),
        in_specs=[pl.BlockSpec((r, d), lambda l, j: (0, 0)),
                  pl.BlockSpec((1, d, tn), lambda l, j: (l, 0, j)),
                  pl.BlockSpec((1, 1, tn), lambda l, j: (l, 0, j))],
        out_specs=pl.BlockSpec((1, r, tn), lambda l, j: (l, 0, j)),
        out_shape=jax.ShapeDtypeStruct((depth, r, n), F32),
        compiler_params=_params("parallel", "parallel"),
        name="modulation",
    )(cond, w_mod, b_mod)


def _rms(x, g):
    return x * lax.rsqrt(jnp.mean(x * x, axis=-1, keepdims=True) + RMS_EPS) * g


def _in_proj_kernel(x_ref, mod_ref, g_ref, wna_ref, wgqa_ref, wmla_ref, wrw_ref,
                    na_ref, gqa_ref, mla_ref, rw_ref):
    mod = mod_ref[0, 0]
    h = _rms(x_ref[0], g_ref[...]) * (1.0 + mod[1:2]) + mod[0:1]
    hb = h.astype(BF16)
    na_ref[0] = jnp.dot(hb, wna_ref[...], preferred_element_type=F32).astype(na_ref.dtype)
    gqa_ref[0] = jnp.dot(hb, wgqa_ref[...], preferred_element_type=F32)
    mla_ref[0] = jnp.dot(hb, wmla_ref[...], preferred_element_type=F32)
    rw_ref[0] = jnp.dot(hb, wrw_ref[...], preferred_element_type=F32)


def in_projection(x_all, mod, g, w_in, ctx_tiles):
    bsz, n_tok, d = x_all.shape
    tm = TOKEN_TILE
    bounds = np.cumsum([0, NA_COLS, GQA_COLS, MLA_COLS, RWKV_COLS])
    ws = [w_in[:, bounds[i]:bounds[i + 1]] for i in range(4)]
    tok = lambda b, i: (b, i, 0)
    full = lambda b, i: (0, 0)
    return pl.pallas_call(
        _in_proj_kernel,
        grid=(bsz, n_tok // tm),
        in_specs=[pl.BlockSpec((1, tm, d), tok),
                  pl.BlockSpec((1, 1, 6, d), lambda b, i: (b, jnp.where(i < ctx_tiles, 0, 1), 0, 0)),
                  pl.BlockSpec((1, d), full)]
                 + [pl.BlockSpec(w.shape, full) for w in ws],
        out_specs=[pl.BlockSpec((1, tm, w.shape[1]), tok) for w in ws],
        out_shape=[jax.ShapeDtypeStruct((bsz, n_tok, NA_COLS), BF16)]
                  + [jax.ShapeDtypeStruct((bsz, n_tok, w.shape[1]), F32) for w in ws[1:]],
        compiler_params=_params("parallel", "parallel"),
        name="in_projection",
    )(x_all, mod, g, *ws)


def _out_mlp_kernel(x_ref, mix_ref, mod_ref, g2_ref, gf_ref, wout_ref, w1_ref, w2_ref, o_ref,
                    *, hidden_tile, final_norm):
    mod = mod_ref[0, 0]
    x1 = x_ref[0] + mod[2:3] * jnp.dot(mix_ref[0], wout_ref[...], preferred_element_type=F32)
    hb = (_rms(x1, g2_ref[...]) * (1.0 + mod[4:5]) + mod[3:4]).astype(BF16)
    acc = jnp.zeros_like(x1)
    for j in range(w1_ref.shape[1] // hidden_tile):
        cols = pl.ds(j * hidden_tile, hidden_tile)
        u = jnp.maximum(jnp.dot(hb, w1_ref[:, cols], preferred_element_type=F32), 0.0)
        acc = acc + jnp.dot((u * u).astype(BF16), w2_ref[cols, :], preferred_element_type=F32)
    x2 = x1 + mod[5:6] * acc
    if final_norm:
        x2 = _rms(x2, gf_ref[...])
    o_ref[0] = x2


def out_projection_mlp(x_all, mix, mod, g2, g_final, w_out, w_fc1, w_fc2, ctx_tiles, final_norm):
    bsz, n_tok, d = x_all.shape
    tm = TOKEN_TILE
    tok = lambda b, i: (b, i, 0)
    full = lambda b, i: (0, 0)
    kern = functools.partial(_out_mlp_kernel, hidden_tile=1024, final_norm=final_norm)
    return pl.pallas_call(
        kern,
        grid=(bsz, n_tok // tm),
        in_specs=[pl.BlockSpec((1, tm, d), tok),
                  pl.BlockSpec((1, tm, mix.shape[2]), tok),
                  pl.BlockSpec((1, 1, 6, d), lambda b, i: (b, jnp.where(i < ctx_tiles, 0, 1), 0, 0)),
                  pl.BlockSpec((1, d), full),
                  pl.BlockSpec((1, d), full),
                  pl.BlockSpec(w_out.shape, full, pipeline_mode=pl.Buffered(1)),
                  pl.BlockSpec(w_fc1.shape, full, pipeline_mode=pl.Buffered(1)),
                  pl.BlockSpec(w_fc2.shape, full, pipeline_mode=pl.Buffered(1))],
        out_specs=pl.BlockSpec((1, tm, d), tok),
        out_shape=jax.ShapeDtypeStruct((bsz, n_tok, d), F32),
        compiler_params=_params("parallel", "parallel"),
        name="out_projection_mlp",
    )(x_all, mix, mod, g2, g_final, w_out, w_fc1, w_fc2)


def _attention_kernel(q_ref, k_ref, v_ref, o_ref):
    grp, tq, dk = q_ref.shape[2], q_ref.shape[3], q_ref.shape[4]
    q = q_ref[0, 0].reshape(grp * tq, dk)
    s = _bdot_nt(q, k_ref[0, 0])
    p = jnp.exp(s - jnp.max(s, axis=-1, keepdims=True))
    o = jnp.dot(p.astype(BF16), v_ref[0, 0], preferred_element_type=F32)
    o = o / jnp.sum(p, axis=-1, keepdims=True)
    o_ref[0, 0] = o.reshape(grp, tq, o.shape[-1]).astype(o_ref.dtype)


def attention(q, k, v, q_tile0, n_q_tiles, n_keys):
    bsz, hk, grp, _, dk = q.shape
    dv = v.shape[-1]
    tq = ATTN_Q_TILE
    return pl.pallas_call(
        _attention_kernel,
        grid=(bsz, hk, n_q_tiles),
        in_specs=[pl.BlockSpec((1, 1, grp, tq, dk), lambda b, h, i: (b, h, 0, i + q_tile0, 0)),
                  pl.BlockSpec((1, 1, n_keys, dk), lambda b, h, i: (b, h, 0, 0)),
                  pl.BlockSpec((1, 1, n_keys, dv), lambda b, h, i: (b, h, 0, 0))],
        out_specs=pl.BlockSpec((1, 1, grp, tq, dv), lambda b, h, i: (b, h, 0, i, 0)),
        out_shape=jax.ShapeDtypeStruct((bsz, hk, grp, n_q_tiles * tq, dv), BF16),
        compiler_params=_params("parallel", "parallel", "parallel"),
        name="attention",
    )(q, k, v)


def _na_kernel(q_ref, k_ref, v_ref, bias_ref, o_ref, *, n_ctx, n_row_tiles):
    i = pl.program_id(2)
    n_win = NA_KEY_ROWS * GRID_W
    max_row0 = n_row_tiles * NA_Q_ROWS - NA_KEY_ROWS
    row0 = jnp.clip(i * NA_Q_ROWS - NA_WIN_ROWS // 2, 0, max_row0)
    start = pl.multiple_of(n_ctx + row0 * GRID_W, GRID_W)
    q = q_ref[0, 0]
    s_win = _bdot_nt(q, k_ref[0, 0, pl.ds(start, n_win), :]) + bias_ref[0, 0]
    s_ctx = _bdot_nt(q, k_ref[0, 0, pl.ds(0, n_ctx), :])
    m = jnp.maximum(jnp.max(s_win, axis=-1, keepdims=True), jnp.max(s_ctx, axis=-1, keepdims=True))
    p_win = jnp.exp(s_win - m)
    p_ctx = jnp.exp(s_ctx - m)
    denom = jnp.sum(p_win, axis=-1, keepdims=True) + jnp.sum(p_ctx, axis=-1, keepdims=True)
    o = (jnp.dot(p_win.astype(BF16), v_ref[0, 0, pl.ds(start, n_win), :], preferred_element_type=F32)
         + jnp.dot(p_ctx.astype(BF16), v_ref[0, 0, pl.ds(0, n_ctx), :], preferred_element_type=F32))
    o_ref[0, 0] = (o / denom).astype(o_ref.dtype)


def na_bias_table(rpb, rows):
    n_tiles = rows // NA_Q_ROWS
    n_heads, n_dr, n_dc = rpb.shape
    qc = np.arange(GRID_W)[:, None]
    kc = np.arange(GRID_W)[None, :]
    dc = np.clip(kc - qc + NA_WIN_COLS - 1, 0, n_dc - 1)
    onehot = jnp.asarray(dc.reshape(-1)[None, :] == np.arange(n_dc)[:, None], F32)
    by_col = jnp.dot(rpb.reshape(-1, n_dc).astype(F32), onehot, precision=lax.Precision.HIGHEST)
    pad = NA_KEY_ROWS
    by_col = jnp.pad(by_col.reshape(n_heads, n_dr, GRID_W, GRID_W), ((0, 0), (pad, pad), (0, 0), (0, 0)))
    c_start = np.clip(qc - NA_WIN_COLS // 2, 0, GRID_W - NA_WIN_COLS)
    col_ok = (kc >= c_start) & (kc < c_start + NA_WIN_COLS)
    tables = []
    for tile in (0, 1, n_tiles - 1):
        row0 = int(np.clip(tile * NA_Q_ROWS - NA_WIN_ROWS // 2, 0, rows - NA_KEY_ROWS))
        per_row = []
        for a in range(NA_Q_ROWS):
            qr = tile * NA_Q_ROWS + a
            dr0 = row0 - qr + NA_WIN_ROWS - 1 + pad
            kr = row0 + np.arange(NA_KEY_ROWS)
            r_start = int(np.clip(qr - NA_WIN_ROWS // 2, 0, rows - NA_WIN_ROWS))
            row_ok = (kr >= r_start) & (kr < r_start + NA_WIN_ROWS)
            valid = row_ok[None, :, None] & col_ok[:, None, :]
            sl = jnp.swapaxes(by_col[:, dr0:dr0 + NA_KEY_ROWS], 1, 2)
            per_row.append(jnp.where(valid[None], sl, NEG_INF))
        bias = jnp.stack(per_row, axis=1)
        tables.append(bias.reshape(n_heads, NA_Q_ROWS * GRID_W, NA_KEY_ROWS * GRID_W))
    return jnp.stack(tables, axis=1)


def neighbourhood_attention(q, k, v, bias, n_ctx):
    bsz, n_heads, n_tok, d = q.shape
    tq = ATTN_Q_TILE
    n_tiles = (n_tok - n_ctx) // tq
    ctx_tiles = n_ctx // tq
    kern = functools.partial(_na_kernel, n_ctx=n_ctx, n_row_tiles=n_tiles)

    def bias_map(b, h, i):
        return (h, jnp.where(i == 0, 0, jnp.where(i == n_tiles - 1, 2, 1)), 0, 0)

    return pl.pallas_call(
        kern,
        grid=(bsz, n_heads, n_tiles),
        in_specs=[pl.BlockSpec((1, 1, tq, d), lambda b, h, i: (b, h, i + ctx_tiles, 0)),
                  pl.BlockSpec((1, 1, n_tok, d), lambda b, h, i: (b, h, 0, 0)),
                  pl.BlockSpec((1, 1, n_tok, d), lambda b, h, i: (b, h, 0, 0)),
                  pl.BlockSpec((1, 1) + bias.shape[2:], bias_map)],
        out_specs=pl.BlockSpec((1, 1, tq, d), lambda b, h, i: (b, h, i, 0)),
        out_shape=jax.ShapeDtypeStruct((bsz, n_heads, n_tok - n_ctx, d), BF16),
        compiler_params=_params("parallel", "parallel", "parallel"),
        name="neighbourhood_attention",
    )(q, k, v, bias)


def _softplus(z):
    return jnp.maximum(z, 0.0) + jnp.log(1.0 + jnp.exp(-jnp.abs(z)))


def _sigmoid(z):
    return 1.0 / (1.0 + jnp.exp(-z))


def _hdot(a, b):
    return jnp.dot(a, b, precision=lax.Precision.HIGHEST, preferred_element_type=F32)


def _rwkv_prep_kernel(u_ref, halo_ref, taps_ref, w0_ref, w2_ref, a0_ref, a2_ref, kk_ref, ka_ref, rk_ref,
                      g2_ref, *out_refs):
    dir_refs = (out_refs[0:6], out_refs[6:12])
    v_out, gt_refs, bonus_ref, gate_ref = out_refs[12], out_refs[13:15], out_refs[15], out_refs[16]
    tm = u_ref.shape[1]
    c = RWKV_CHUNK
    n_chunks = tm // c
    gw = GROUP_WIDTH
    u = u_ref[0]
    halo = halo_ref[0, 0]
    tok = lax.broadcasted_iota(jnp.int32, u.shape, 0)
    prev = jnp.where(tok == 0, halo[0:1], pltpu.roll(u, 1, axis=0))
    nxt = jnp.where(tok == tm - 1, halo[1:2], pltpu.roll(u, tm - 1, axis=0))
    taps = taps_ref[...]
    s = prev * taps[0:1] + u * taps[1:2] + nxt * taps[2:3]

    r, k, v = s[:, 0:gw], s[:, gw:2 * gw], s[:, 2 * gw:3 * gw]
    lo = 3 * gw
    w_low = jnp.tanh(s[:, lo:lo + 2 * RWKV_DECAY_RANK])
    lo += 2 * RWKV_DECAY_RANK
    a_low = s[:, lo:lo + 2 * RWKV_ICLR_RANK]
    lo += 2 * RWKV_ICLR_RANK
    g_low = _sigmoid(s[:, lo:lo + RWKV_GATE_RANK])
    w_log = -_softplus(-(w0_ref[...] + _bdot(w_low, w2_ref[...]))) - 0.5
    log_decay = -jnp.exp(w_log)
    iclr = _sigmoid(a0_ref[...] + _bdot(a_low, a2_ref[...]))
    gate_ref[0] = _bdot(g_low, g2_ref[...])

    row = lax.broadcasted_iota(jnp.int32, (tm, tm), 0)
    col = lax.broadcasted_iota(jnp.int32, (tm, tm), 1)
    shift = c.bit_length() - 1
    same = (row >> shift) == (col >> shift)
    ones_blk = same.astype(F32)

    def head_major(ref, val):
        for h in range(N_HEADS):
            ref[0, h] = val[:, h * HEAD_DIM:(h + 1) * HEAD_DIM].astype(ref.dtype)

    head_major(v_out, v)
    bonus = jnp.zeros_like(r)
    for d in range(2):
        sl = slice(d * gw, (d + 1) * gw)
        lw = log_decay[:, sl]
        kk = k * kk_ref[:, sl]
        kk = kk * lax.rsqrt(jnp.maximum(_hdot(kk * kk, ones_blk), 1e-24))
        k_d = k * (1.0 + (iclr[:, sl] - 1.0) * ka_ref[:, sl])
        b_vec = kk * iclr[:, sl]
        tri = (same & ((row >= col) if d == 0 else (row <= col))).astype(F32)
        cum = _hdot(tri, lw)
        ends = [cum[i * c + (c - 1 if d == 0 else 0)][None] for i in range(n_chunks)]
        total = jnp.concatenate([jnp.broadcast_to(e, (c, gw)) for e in ends], axis=0)
        e_inv = jnp.exp(-cum)
        e_end = jnp.exp(total - cum)
        streams = (r * jnp.exp(cum), -kk * jnp.exp(cum - lw), b_vec * e_inv, k_d * e_inv,
                   b_vec * e_end, k_d * e_end)
        for ref, val in zip(dir_refs[d], streams):
            head_major(ref, val)
        g_tot = jnp.exp(jnp.concatenate(ends, axis=0))
        for h in range(N_HEADS):
            gt_refs[d][0, 0, h * n_chunks:(h + 1) * n_chunks, :] = g_tot[:, h * HEAD_DIM:(h + 1) * HEAD_DIM]
        bonus = bonus + r * k_d * rk_ref[:, sl]
    bonus_ref[0] = _hdot(bonus, ones_blk) * v


def rwkv_prepare(rw, halo, taps, w0, w2, a0, a2, k_k, k_a, r_k, g2):
    bsz, n_tok, width = rw.shape
    tm = TOKEN_TILE
    n_tiles = n_tok // tm
    n_chunks = tm // RWKV_CHUNK
    full = lambda b, i: (0, 0)
    consts = (taps, w0, w2, a0, a2, k_k, k_a, r_k, g2)
    stream = jax.ShapeDtypeStruct((bsz, N_HEADS, n_tok, HEAD_DIM), BF16)
    stream_spec = pl.BlockSpec((1, N_HEADS, tm, HEAD_DIM), lambda b, i: (b, 0, i, 0))
    decay = jax.ShapeDtypeStruct((bsz, n_tiles, N_HEADS * n_chunks, HEAD_DIM), F32)
    decay_spec = pl.BlockSpec((1, 1, N_HEADS * n_chunks, HEAD_DIM), lambda b, i: (b, i, 0, 0))
    tokens = jax.ShapeDtypeStruct((bsz, n_tok, GROUP_WIDTH), F32)
    tokens_spec = pl.BlockSpec((1, tm, GROUP_WIDTH), lambda b, i: (b, i, 0))
    return pl.pallas_call(
        _rwkv_prep_kernel,
        grid=(bsz, n_tiles),
        in_specs=[pl.BlockSpec((1, tm, width), lambda b, i: (b, i, 0)),
                  pl.BlockSpec((1, 1, 2, width), lambda b, i: (b, i, 0, 0))]
                 + [pl.BlockSpec(t.shape, full) for t in consts],
        out_specs=[stream_spec] * 13 + [decay_spec] * 2 + [tokens_spec] * 2,
        out_shape=[stream] * 13 + [decay] * 2 + [tokens] * 2,
        compiler_params=_params("parallel", "parallel"),
        name="rwkv_prepare",
    )(rw, halo, *consts)


def _bmm(a, b):
    return lax.dot_general(a.astype(BF16), b.astype(BF16), (((2,), (1,)), ((0,), (0,))),
                           preferred_element_type=F32)


def _bmm_nt(a, b):
    return lax.dot_general(a.astype(BF16), b.astype(BF16), (((2,), (2,)), ((0,), (0,))),
                           preferred_element_type=F32)


def _bmm_tn(a, b):
    return lax.dot_general(a.astype(BF16), b.astype(BF16), (((1,), (1,)), ((0,), (0,))),
                           preferred_element_type=F32)


def _rwkv_chunk_kernel(rt_ref, at_ref, bt_ref, kt_ref, be_ref, ke_ref, v_ref, gt_ref, y_ref, s_ref,
                       *, reverse):
    n_heads, tb, n = v_ref.shape[1:]
    c = RWKV_CHUNK
    n_chunks = tb // c
    g = n_heads * n_chunks

    @pl.when(pl.program_id(1) == 0)
    def _():
        s_ref[...] = jnp.zeros_like(s_ref)

    load = lambda ref: ref[0].reshape(g, c, n)
    rt, at, bt, kt, be, ke, v = (load(ref) for ref in (rt_ref, at_ref, bt_ref, kt_ref, be_ref, ke_ref, v_ref))
    row = lax.broadcasted_iota(jnp.int32, (c, c), 0)
    col = lax.broadcasted_iota(jnp.int32, (c, c), 1)
    if reverse:
        row, col = col, row
    strict = row > col
    incl = row >= col
    a_ab = jnp.where(strict, _bmm_nt(at, bt), 0.0)
    a_ak = jnp.where(strict, _bmm_nt(at, kt), 0.0)
    a_rb = jnp.where(incl, _bmm_nt(rt, bt), 0.0)
    a_rk = jnp.where(incl, _bmm_nt(rt, kt), 0.0)
    av = _bmm(a_ak, v)
    n1 = jnp.where((row >> 3) == (col >> 3), a_ab, 0.0)
    n2 = _bmm(n1, n1)
    n4 = _bmm(n2, n2)
    p = (row == col).astype(F32) + n1 + n2 + _bmm(n1, n2)
    inv = p + _bmm(p, n4)
    for level in range(3, c.bit_length() - 1):
        rb, cb = row >> level, col >> level
        off = jnp.where((rb == cb + 1) & ((rb & 1) == 1), a_ab, 0.0)
        inv = inv + _bmm(_bmm(inv, off), inv)
    w = _bmm(inv, at)
    u0 = _bmm(inv, av)
    q_eff = rt.astype(F32) + _bmm(a_rb, w)
    y0 = _bmm(a_rb, u0) + _bmm(a_rk, v)

    per_chunk = lambda t: t.reshape((n_heads, n_chunks) + t.shape[1:])
    q_eff, w, u0, y0, be, ke, v = (per_chunk(t) for t in (q_eff, w, u0, y0, be, ke, v))
    g_tot = gt_ref[0, 0].reshape(n_heads, n_chunks, 1, n)
    state = s_ref[...]
    for i in (range(n_chunks - 1, -1, -1) if reverse else range(n_chunks)):
        y_ref[0, :, i * c:(i + 1) * c, :] = _bmm_nt(q_eff[:, i], state) + y0[:, i]
        u = _bmm_nt(w[:, i], state) + u0[:, i]
        state = g_tot[:, i] * state + _bmm_tn(u, be[:, i]) + _bmm_tn(v[:, i], ke[:, i])
    s_ref[...] = state


def rwkv_chunk_scan(streams, v, g_tot, ctx_blocks, reverse):
    bsz, n_heads, n_tok, n = v.shape
    tb = TOKEN_TILE
    n_blocks = n_tok // tb

    def block(i):
        if not reverse:
            return i
        return jnp.where(i < ctx_blocks, ctx_blocks - 1 - i, n_blocks - 1 + ctx_blocks - i)

    spec = pl.BlockSpec((1, n_heads, tb, n), lambda b, i: (b, 0, block(i), 0))
    return pl.pallas_call(
        functools.partial(_rwkv_chunk_kernel, reverse=reverse),
        grid=(bsz, n_blocks),
        in_specs=[spec] * 7 + [pl.BlockSpec((1, 1) + g_tot.shape[2:], lambda b, i: (b, block(i), 0, 0))],
        out_specs=spec,
        out_shape=jax.ShapeDtypeStruct((bsz, n_heads, n_tok, n), F32),
        scratch_shapes=[pltpu.VMEM((n_heads, n, n), F32)],
        compiler_params=_params("parallel", "arbitrary"),
        name="rwkv_chunk_scan_bwd" if reverse else "rwkv_chunk_scan_fwd",
    )(*streams, v, g_tot)


def _rwkv_post_kernel(yf_ref, yb_ref, bonus_ref, gate_ref, lw_ref, lb_ref, o_ref):
    for h in range(N_HEADS):
        sl = slice(h * HEAD_DIM, (h + 1) * HEAD_DIM)
        y = yf_ref[0, h] + yb_ref[0, h]
        mu = jnp.mean(y, axis=-1, keepdims=True)
        var = jnp.mean(jnp.square(y - mu), axis=-1, keepdims=True)
        yn = (y - mu) * lax.rsqrt(var + GN_EPS)
        yn = yn * lw_ref[:, sl] + lb_ref[:, sl] + bonus_ref[0, :, sl]
        o_ref[0, :, sl] = (yn * gate_ref[0, :, sl]).astype(o_ref.dtype)


def rwkv_finish(y_f, y_b, bonus, gate, lnx_w, lnx_b):
    bsz, n_heads, n_tok, n = y_f.shape
    tm = TOKEN_TILE
    heads = pl.BlockSpec((1, n_heads, tm, n), lambda b, i: (b, 0, i, 0))
    tokens = pl.BlockSpec((1, tm, n_heads * n), lambda b, i: (b, i, 0))
    vec = pl.BlockSpec((1, n_heads * n), lambda b, i: (0, 0))
    return pl.pallas_call(
        _rwkv_post_kernel,
        grid=(bsz, n_tok // tm),
        in_specs=[heads, heads, tokens, tokens, vec, vec],
        out_specs=tokens,
        out_shape=jax.ShapeDtypeStruct((bsz, n_tok, n_heads * n), BF16),
        compiler_params=_params("parallel", "parallel"),
        name="rwkv_finish",
    )(y_f, y_b, bonus, gate, lnx_w, lnx_b)


def _rope_tables(n_ctx, n_lat, rot_dim):
    t = jnp.arange(n_lat, dtype=jnp.int32)
    rows = (t // GRID_W).astype(F32)
    cols = (t % GRID_W).astype(F32)
    per_axis = rot_dim // 2
    inv_freq = ROPE_THETA ** (-jnp.arange(0, per_axis, 2, dtype=F32) / per_axis)
    ang = jnp.concatenate([rows[:, None] * inv_freq, cols[:, None] * inv_freq], axis=-1)
    ang = jnp.concatenate([jnp.zeros((n_ctx, ang.shape[1]), F32), ang], axis=0)
    return jnp.cos(ang), jnp.sin(ang)


def _rope(x, cos, sin):
    c = cos[None, :, None, :]
    s = sin[None, :, None, :]
    x1, x2 = x[..., 0::2], x[..., 1::2]
    return jnp.stack([x1 * c - x2 * s, x1 * s + x2 * c], axis=-1).reshape(x.shape)


def _heads(u, n_heads):
    return u.reshape(u.shape[:-1] + (n_heads, u.shape[-1] // n_heads))


def _head_major(u):
    return jnp.swapaxes(u, 1, 2)


def _tok_matmul(u, w, out_dtype=F32):
    lead = u.shape[:-1]
    return matmul(u.reshape(-1, u.shape[-1]), w.astype(BF16), out_dtype).reshape(lead + (w.shape[1],))


def _block_diag(m0, m1):
    z01 = jnp.zeros((m0.shape[0], m1.shape[1]), m0.dtype)
    z10 = jnp.zeros((m1.shape[0], m0.shape[1]), m0.dtype)
    return jnp.concatenate([jnp.concatenate([m0, z01], axis=1), jnp.concatenate([z10, m1], axis=1)], axis=0)


def _shift_halo(rw, ctx_tiles):
    tm = TOKEN_TILE
    zero = jnp.zeros_like(rw[:, :1])
    prev = jnp.concatenate([zero, rw[:, tm - 1::tm][:, :-1]], axis=1)
    nxt = jnp.concatenate([rw[:, ::tm][:, 1:], zero], axis=1)
    tile = jnp.arange(prev.shape[1])[None, :, None]
    prev = jnp.where(tile == ctx_tiles, 0.0, prev)
    nxt = jnp.where(tile == ctx_tiles - 1, 0.0, nxt)
    return jnp.stack([prev, nxt], axis=2)


def _rwkv_mixer(rw, ctx_tiles, shift_taps, w0, w2, a0, a2, k_k, k_a, r_k, g2, lnx_w, lnx_b):
    both = lambda t: t.reshape(1, -1)
    outs = rwkv_prepare(rw, _shift_halo(rw, ctx_tiles), shift_taps, both(w0),
                        _block_diag(w2[0], w2[1]).astype(BF16), both(a0),
                        _block_diag(a2[0], a2[1]).astype(BF16), both(k_k), both(k_a), both(r_k),
                        g2.astype(BF16))
    v, bonus, gate = outs[12], outs[15], outs[16]
    y_f = rwkv_chunk_scan(outs[0:6], v, outs[13], ctx_tiles, False)
    y_b = rwkv_chunk_scan(outs[6:12], v, outs[14], ctx_tiles, True)
    return rwkv_finish(y_f, y_b, bonus, gate, lnx_w[None], lnx_b[None])


def _head_rms(u, g):
    return u * lax.rsqrt(jnp.mean(u * u, axis=-1, keepdims=True) + RMS_EPS) * g


def _merge_heads(o_lat, o_ctx):
    bsz, n_heads, n_lat, d = o_lat.shape
    if o_ctx is None:
        o_ctx = jnp.zeros((bsz, n_heads, 0, d), o_lat.dtype)
    o = jnp.swapaxes(jnp.concatenate([o_ctx, o_lat], axis=2), 1, 2)
    return o.reshape(bsz, o.shape[1], n_heads * d)


def _dense_branch(q, k, v, n_ctx, want_ctx):
    bsz, hk, grp, n_tok, _ = q.shape
    ctx_tiles = n_ctx // ATTN_Q_TILE
    o_lat = attention(q, k, v, ctx_tiles, (n_tok - n_ctx) // ATTN_Q_TILE, n_tok)
    o_lat = o_lat.reshape(bsz, hk * grp, n_tok - n_ctx, -1)
    o_ctx = None
    if want_ctx:
        o_ctx = attention(q, k, v, 0, ctx_tiles, n_ctx).reshape(bsz, hk * grp, n_ctx, -1)
    return o_lat, o_ctx


def kernel(x, c, ctx, c_ctx, w_mod, b_mod, norm1_g, norm2_g, w_in, rwkv_shift, na_rpb, gqa_q_norm, gqa_k_norm, mla_q_norm, mla_kv_norm, mla_w_uq, mla_w_ukv, rwkv_w0, rwkv_w2, rwkv_a0, rwkv_a2, rwkv_k_k, rwkv_k_a, rwkv_r_k, rwkv_g2, rwkv_lnx_w, rwkv_lnx_b, w_out, w_fc1, w_fc2, final_norm_g):
    bsz, n_lat, d_model = x.shape
    n_ctx = ctx.shape[1]
    depth = w_mod.shape[0]
    rows = n_lat // GRID_W
    ctx_tiles = n_ctx // TOKEN_TILE
    assert n_ctx % TOKEN_TILE == 0 and n_lat % TOKEN_TILE == 0 and rows % NA_Q_ROWS == 0

    cos_h, sin_h = _rope_tables(n_ctx, n_lat, HEAD_DIM)
    cos_m, sin_m = _rope_tables(n_ctx, n_lat, MLA_ROPE_DIM)

    cond = jnp.concatenate([jax.nn.silu(c), jax.nn.silu(c_ctx)[None]], axis=0)
    n_cond = -(-cond.shape[0] // 8) * 8
    cond = jnp.pad(cond, ((0, n_cond - cond.shape[0]), (0, 0)))
    mod_all = modulation(cond, w_mod, b_mod[:, None, :])

    x_all = jnp.concatenate([ctx, x], axis=1)
    qk_scale = HEAD_DIM ** -0.5
    mla_scale = MLA_QK_DIM ** -0.5

    for layer in range(depth):
        want_ctx = layer < depth - 1
        mod = mod_all[layer].reshape(n_cond, 6, d_model)
        mod = jnp.stack([jnp.broadcast_to(mod[bsz], (bsz, 6, d_model)), mod[:bsz]], axis=1)

        na, gqa, mla, rw = in_projection(x_all, mod, norm1_g[layer][None], w_in[layer].astype(BF16), ctx_tiles)

        a_q, a_k, a_v = [_head_major(_heads(t, N_HEADS)) for t in jnp.split(na, 3, axis=-1)]
        a_q = a_q * jnp.asarray(qk_scale, BF16)
        out_a = neighbourhood_attention(a_q, a_k, a_v, na_bias_table(na_rpb[layer], rows), n_ctx)
        out_ac = None
        if want_ctx:
            out_ac = attention(a_q[:, :, None], a_k, a_v, 0, ctx_tiles, n_ctx)[:, :, 0]
        mix_a = _merge_heads(out_a, out_ac)

        b_q = _head_rms(_heads(gqa[..., :GROUP_WIDTH], N_HEADS), gqa_q_norm[layer])
        b_k, b_v = jnp.split(gqa[..., GROUP_WIDTH:], 2, axis=-1)
        b_k = _head_rms(_heads(b_k, GQA_KV_HEADS), gqa_k_norm[layer])
        b_q = _head_major(_rope(b_q, cos_h, sin_h) * qk_scale).astype(BF16)
        b_q = b_q.reshape(bsz, GQA_KV_HEADS, N_HEADS // GQA_KV_HEADS, b_q.shape[2], HEAD_DIM)
        b_k = _head_major(_rope(b_k, cos_h, sin_h)).astype(BF16)
        b_v = _head_major(_heads(b_v, GQA_KV_HEADS)).astype(BF16)
        mix_b = _merge_heads(*_dense_branch(b_q, b_k, b_v, n_ctx, want_ctx))

        c_q = _tok_matmul(_head_rms(mla[..., :MLA_Q_RANK], mla_q_norm[layer]), mla_w_uq[layer])
        c_q = _heads(c_q, N_HEADS)
        c_q = jnp.concatenate([c_q[..., :HEAD_DIM], _rope(c_q[..., HEAD_DIM:], cos_m, sin_m)], axis=-1)
        c_q = _head_major(c_q * mla_scale).astype(BF16)[:, :, None]
        c_kv = _tok_matmul(_head_rms(mla[..., MLA_Q_RANK:MLA_Q_RANK + MLA_KV_RANK], mla_kv_norm[layer]),
                           mla_w_ukv[layer])
        c_kv = _heads(c_kv, N_HEADS)
        c_kr = _rope(mla[..., MLA_Q_RANK + MLA_KV_RANK:][:, :, None, :], cos_m, sin_m)
        c_k = jnp.concatenate([c_kv[..., :HEAD_DIM],
                               jnp.broadcast_to(c_kr, c_kv.shape[:-1] + (MLA_ROPE_DIM,))], axis=-1)
        c_k = _head_major(c_k).astype(BF16)
        c_v = _head_major(c_kv[..., HEAD_DIM:]).astype(BF16)
        mix_c = _merge_heads(*_dense_branch(c_q, c_k, c_v, n_ctx, want_ctx))

        mix_d = _rwkv_mixer(rw, ctx_tiles, rwkv_shift[layer], rwkv_w0[layer], rwkv_w2[layer], rwkv_a0[layer],
                            rwkv_a2[layer], rwkv_k_k[layer], rwkv_k_a[layer], rwkv_r_k[layer],
                            rwkv_g2[layer], rwkv_lnx_w[layer], rwkv_lnx_b[layer])

        if not want_ctx:
            pad = lambda t: jnp.pad(t, ((0, 0), (n_ctx, 0), (0, 0)))
            mix_a, mix_b, mix_c = pad(mix_a), pad(mix_b), pad(mix_c)
        mix = jnp.concatenate([mix_a, mix_b, mix_c, mix_d], axis=-1)

        x_all = out_projection_mlp(x_all, mix, mod, norm2_g[layer][None], final_norm_g[None],
                                   w_out[layer].astype(BF16), w_fc1[layer].astype(BF16),
                                   w_fc2[layer].astype(BF16), ctx_tiles, layer == depth - 1)
    return x_all[:, n_ctx:]
```

```python
import functools

import numpy as np
import jax
import jax.numpy as jnp
from jax import lax
from jax.experimental import pallas as pl
from jax.experimental.pallas import tpu as pltpu

F32 = jnp.float32
BF16 = jnp.bfloat16

GRID_W = 64
HEAD_DIM = 64
N_HEADS = 4
GROUP_WIDTH = N_HEADS * HEAD_DIM
GQA_KV_HEADS = 2
NA_WIN_ROWS = 8
NA_WIN_COLS = 16
MLA_Q_RANK = 256
MLA_KV_RANK = 128
MLA_ROPE_DIM = 32
MLA_QK_DIM = HEAD_DIM + MLA_ROPE_DIM
RWKV_DECAY_RANK = 64
RWKV_ICLR_RANK = 64
RWKV_GATE_RANK = 160
NA_COLS = 3 * GROUP_WIDTH
GQA_COLS = (N_HEADS + 2 * GQA_KV_HEADS) * HEAD_DIM
MLA_COLS = MLA_Q_RANK + MLA_KV_RANK + MLA_ROPE_DIM
RWKV_COLS = 3 * GROUP_WIDTH + 2 * RWKV_DECAY_RANK + 2 * RWKV_ICLR_RANK + RWKV_GATE_RANK
ROPE_THETA = 10000.0
RMS_EPS = 1e-6
GN_EPS = 64e-5
NEG_INF = -1e30

TOKEN_TILE = 256
ATTN_Q_TILE = 256
ATTN_KV_CHUNK = 512
V_EXT_WIDTH = 128
LOG2E = float(np.log2(np.e))
NA_Q_ROWS = ATTN_Q_TILE // GRID_W
NA_KEY_ROWS = NA_Q_ROWS + NA_WIN_ROWS - 1
RWKV_CHUNK = 64
VMEM_LIMIT = 56 * 1024 * 1024


def _params(*sem):
    return pltpu.CompilerParams(dimension_semantics=sem, vmem_limit_bytes=VMEM_LIMIT)


def _bdot(a, b):
    return jnp.dot(a.astype(BF16), b.astype(BF16), preferred_element_type=F32)


def _bdot_nt(a, b):
    return lax.dot_general(a.astype(BF16), b.astype(BF16), (((1,), (1,)), ((), ())),
                           preferred_element_type=F32)


def _bdot_tn(a, b):
    return lax.dot_general(a.astype(BF16), b.astype(BF16), (((0,), (0,)), ((), ())),
                           preferred_element_type=F32)


def _matmul_kernel(a_ref, b_ref, o_ref):
    o_ref[...] = _bdot(a_ref[...], b_ref[...]).astype(o_ref.dtype)


def matmul(a, b, out_dtype=F32, tm=1024):
    m, k = a.shape
    n = b.shape[1]
    tm = min(tm, m)
    while m % tm:
        tm //= 2
    return pl.pallas_call(
        _matmul_kernel,
        grid=(m // tm,),
        in_specs=[pl.BlockSpec((tm, k), lambda i: (i, 0)),
                  pl.BlockSpec((k, n), lambda i: (0, 0))],
        out_specs=pl.BlockSpec((tm, n), lambda i: (i, 0)),
        out_shape=jax.ShapeDtypeStruct((m, n), out_dtype),
        compiler_params=_params("parallel"),
        name="matmul",
    )(a, b)


def _mod_kernel(a_ref, w_ref, b_ref, o_ref):
    o_ref[0] = _bdot(a_ref[...], w_ref[0]) + b_ref[0]


def modulation(cond, w_mod, b_mod, tn=1536):
    depth, d, n = w_mod.shape
    r = cond.shape[0]
    return pl.pallas_call(
        _mod_kernel,
        grid=(depth, n // tn),
        in_specs=[pl.BlockSpec((r, d), lambda l, j: (0, 0)),
                  pl.BlockSpec((1, d, tn), lambda l, j: (l, 0, j)),
                  pl.BlockSpec((1, 1, tn), lambda l, j: (l, 0, j))],
        out_specs=pl.BlockSpec((1, r, tn), lambda l, j: (l, 0, j)),
        out_shape=jax.ShapeDtypeStruct((depth, r, n), F32),
        compiler_params=_params("parallel", "parallel"),
        name="modulation",
    )(cond, w_mod, b_mod)


def _rms(x, g):
    return x * lax.rsqrt(jnp.mean(x * x, axis=-1, keepdims=True) + RMS_EPS) * g


def _in_proj_kernel(x_ref, mod_ref, g_ref, wna_ref, wgqa_ref, wmla_ref, wrw_ref,
                    na_ref, gqa_ref, mla_ref, rw_ref):
    mod = mod_ref[0, 0]
    h = _rms(x_ref[0], g_ref[...]) * (1.0 + mod[1:2]) + mod[0:1]
    hb = h.astype(BF16)
    na_ref[0] = jnp.dot(hb, wna_ref[...], preferred_element_type=F32).astype(na_ref.dtype)
    gqa_ref[0] = jnp.dot(hb, wgqa_ref[...], preferred_element_type=F32)
    mla_ref[0] = jnp.dot(hb, wmla_ref[...], preferred_element_type=F32)
    rw_ref[0] = jnp.dot(hb, wrw_ref[...], preferred_element_type=F32)


def in_projection(x_all, mod, g, w_in, ctx_tiles):
    bsz, n_tok, d = x_all.shape
    tm = TOKEN_TILE
    bounds = np.cumsum([0, NA_COLS, GQA_COLS, MLA_COLS, RWKV_COLS])
    ws = [w_in[:, bounds[i]:bounds[i + 1]] for i in range(4)]
    tok = lambda b, i: (b, i, 0)
    full = lambda b, i: (0, 0)
    return pl.pallas_call(
        _in_proj_kernel,
        grid=(bsz, n_tok // tm),
        in_specs=[pl.BlockSpec((1, tm, d), tok),
                  pl.BlockSpec((1, 1, 6, d), lambda b, i: (b, jnp.where(i < ctx_tiles, 0, 1), 0, 0)),
                  pl.BlockSpec((1, d), full)]
                 + [pl.BlockSpec(w.shape, full) for w in ws],
        out_specs=[pl.BlockSpec((1, tm, w.shape[1]), tok) for w in ws],
        out_shape=[jax.ShapeDtypeStruct((bsz, n_tok, NA_COLS), BF16)]
                  + [jax.ShapeDtypeStruct((bsz, n_tok, w.shape[1]), F32) for w in ws[1:]],
        compiler_params=_params("parallel", "parallel"),
        name="in_projection",
    )(x_all, mod, g, *ws)


def _out_mlp_kernel(x_ref, mix_ref, mod_ref, g2_ref, gf_ref, wout_ref, w1_ref, w2_ref, o_ref,
                    *, hidden_tile, final_norm):
    mod = mod_ref[0, 0]
    x1 = x_ref[0] + mod[2:3] * jnp.dot(mix_ref[0], wout_ref[...], preferred_element_type=F32)
    hb = (_rms(x1, g2_ref[...]) * (1.0 + mod[4:5]) + mod[3:4]).astype(BF16)
    acc = jnp.zeros_like(x1)
    for j in range(w1_ref.shape[1] // hidden_tile):
        cols = pl.ds(j * hidden_tile, hidden_tile)
        u = jnp.maximum(jnp.dot(hb, w1_ref[:, cols], preferred_element_type=F32), 0.0)
        acc = acc + jnp.dot((u * u).astype(BF16), w2_ref[cols, :], preferred_element_type=F32)
    x2 = x1 + mod[5:6] * acc
    if final_norm:
        x2 = _rms(x2, gf_ref[...])
    o_ref[0] = x2


def out_projection_mlp(x_all, mix, mod, g2, g_final, w_out, w_fc1, w_fc2, ctx_tiles, final_norm):
    bsz, n_tok, d = x_all.shape
    tm = TOKEN_TILE
    tok = lambda b, i: (b, i, 0)
    full = lambda b, i: (0, 0)
    kern = functools.partial(_out_mlp_kernel, hidden_tile=1024, final_norm=final_norm)
    return pl.pallas_call(
        kern,
        grid=(bsz, n_tok // tm),
        in_specs=[pl.BlockSpec((1, tm, d), tok),
                  pl.BlockSpec((1, tm, mix.shape[2]), tok),
                  pl.BlockSpec((1, 1, 6, d), lambda b, i: (b, jnp.where(i < ctx_tiles, 0, 1), 0, 0)),
                  pl.BlockSpec((1, d), full),
                  pl.BlockSpec((1, d), full),
                  pl.BlockSpec(w_out.shape, full, pipeline_mode=pl.Buffered(1)),
                  pl.BlockSpec(w_fc1.shape, full, pipeline_mode=pl.Buffered(1)),
                  pl.BlockSpec(w_fc2.shape, full, pipeline_mode=pl.Buffered(1))],
        out_specs=pl.BlockSpec((1, tm, d), tok),
        out_shape=jax.ShapeDtypeStruct((bsz, n_tok, d), F32),
        compiler_params=_params("parallel", "parallel"),
        name="out_projection_mlp",
    )(x_all, mix, mod, g2, g_final, w_out, w_fc1, w_fc2)


def _key_chunks(n_keys):
    first = min(n_keys, ATTN_Q_TILE)
    chunks = [(0, first)]
    while chunks[-1][0] + chunks[-1][1] < n_keys:
        start = chunks[-1][0] + chunks[-1][1]
        chunks.append((start, min(ATTN_KV_CHUNK, n_keys - start)))
    return chunks


def _attention_kernel(q_ref, k_ref, v_ref, o_ref):
    grp, tq, dk = q_ref.shape[2], q_ref.shape[3], q_ref.shape[4]
    dv = o_ref.shape[-1]
    rows = grp * tq
    q = q_ref[0, 0].reshape(rows, dk)
    chunks = _key_chunks(k_ref.shape[2])
    scores = lambda c: _bdot_nt(q, k_ref[0, 0, pl.ds(c[0], c[1]), :])
    m = jnp.full((rows, 1), NEG_INF, F32)
    acc = jnp.zeros((rows, v_ref.shape[-1]), F32)
    s_next = scores(chunks[0])
    for j, (start, size) in enumerate(chunks):
        s = s_next
        if j + 1 < len(chunks):
            s_next = scores(chunks[j + 1])
        m_new = jnp.maximum(m, jnp.max(s, axis=-1, keepdims=True))
        p = jnp.exp2(s - m_new)
        acc = acc * jnp.exp2(m - m_new) + jnp.dot(p.astype(BF16), v_ref[0, 0, pl.ds(start, size), :],
                                                  preferred_element_type=F32)
        m = m_new
    o = acc[:, :dv] / acc[:, dv:dv + 1]
    o_ref[0, 0] = o.reshape(grp, tq, dv).astype(o_ref.dtype)


def attention(q, k, v, q_tile0, n_q_tiles, n_keys, dv):
    bsz, hk, grp, _, dk = q.shape
    tq = ATTN_Q_TILE
    return pl.pallas_call(
        _attention_kernel,
        grid=(bsz, hk, n_q_tiles),
        in_specs=[pl.BlockSpec((1, 1, grp, tq, dk), lambda b, h, i: (b, h, 0, i + q_tile0, 0)),
                  pl.BlockSpec((1, 1, n_keys, dk), lambda b, h, i: (b, h, 0, 0)),
                  pl.BlockSpec((1, 1, n_keys, v.shape[-1]), lambda b, h, i: (b, h, 0, 0))],
        out_specs=pl.BlockSpec((1, 1, grp, tq, dv), lambda b, h, i: (b, h, 0, i, 0)),
        out_shape=jax.ShapeDtypeStruct((bsz, hk, grp, n_q_tiles * tq, dv), BF16),
        compiler_params=_params("parallel", "parallel", "parallel"),
        name="attention",
    )(q, k, v)


def _na_kernel(q_ref, k_ref, v_ref, bias_ref, o_ref, *, n_ctx, n_row_tiles):
    i = pl.program_id(2)
    n_win = NA_KEY_ROWS * GRID_W
    max_row0 = n_row_tiles * NA_Q_ROWS - NA_KEY_ROWS
    row0 = jnp.clip(i * NA_Q_ROWS - NA_WIN_ROWS // 2, 0, max_row0)
    start = pl.multiple_of(n_ctx + row0 * GRID_W, GRID_W)
    q = q_ref[0, 0]
    s_win = _bdot_nt(q, k_ref[0, 0, pl.ds(start, n_win), :]) + bias_ref[0, 0]
    s_ctx = _bdot_nt(q, k_ref[0, 0, pl.ds(0, n_ctx), :])
    m = jnp.maximum(jnp.max(s_win, axis=-1, keepdims=True), jnp.max(s_ctx, axis=-1, keepdims=True))
    p_win = jnp.exp(s_win - m)
    p_ctx = jnp.exp(s_ctx - m)
    denom = jnp.sum(p_win, axis=-1, keepdims=True) + jnp.sum(p_ctx, axis=-1, keepdims=True)
    o = (jnp.dot(p_win.astype(BF16), v_ref[0, 0, pl.ds(start, n_win), :], preferred_element_type=F32)
         + jnp.dot(p_ctx.astype(BF16), v_ref[0, 0, pl.ds(0, n_ctx), :], preferred_element_type=F32))
    o_ref[0, 0] = (o / denom).astype(o_ref.dtype)


def na_bias_table(rpb, rows):
    n_tiles = rows // NA_Q_ROWS
    n_heads, n_dr, n_dc = rpb.shape
    qc = np.arange(GRID_W)[:, None]
    kc = np.arange(GRID_W)[None, :]
    dc = np.clip(kc - qc + NA_WIN_COLS - 1, 0, n_dc - 1)
    onehot = jnp.asarray(dc.reshape(-1)[None, :] == np.arange(n_dc)[:, None], F32)
    by_col = jnp.dot(rpb.reshape(-1, n_dc).astype(F32), onehot, precision=lax.Precision.HIGHEST)
    pad = NA_KEY_ROWS
    by_col = jnp.pad(by_col.reshape(n_heads, n_dr, GRID_W, GRID_W), ((0, 0), (pad, pad), (0, 0), (0, 0)))
    c_start = np.clip(qc - NA_WIN_COLS // 2, 0, GRID_W - NA_WIN_COLS)
    col_ok = (kc >= c_start) & (kc < c_start + NA_WIN_COLS)
    tables = []
    for tile in (0, 1, n_tiles - 1):
        row0 = int(np.clip(tile * NA_Q_ROWS - NA_WIN_ROWS // 2, 0, rows - NA_KEY_ROWS))
        per_row = []
        for a in range(NA_Q_ROWS):
            qr = tile * NA_Q_ROWS + a
            dr0 = row0 - qr + NA_WIN_ROWS - 1 + pad
            kr = row0 + np.arange(NA_KEY_ROWS)
            r_start = int(np.clip(qr - NA_WIN_ROWS // 2, 0, rows - NA_WIN_ROWS))
            row_ok = (kr >= r_start) & (kr < r_start + NA_WIN_ROWS)
            valid = row_ok[None, :, None] & col_ok[:, None, :]
            sl = jnp.swapaxes(by_col[:, dr0:dr0 + NA_KEY_ROWS], 1, 2)
            per_row.append(jnp.where(valid[None], sl, NEG_INF))
        bias = jnp.stack(per_row, axis=1)
        tables.append(bias.reshape(n_heads, NA_Q_ROWS * GRID_W, NA_KEY_ROWS * GRID_W))
    return jnp.stack(tables, axis=1)


def neighbourhood_attention(q, k, v, bias, n_ctx):
    bsz, n_heads, n_tok, d = q.shape
    tq = ATTN_Q_TILE
    n_tiles = (n_tok - n_ctx) // tq
    ctx_tiles = n_ctx // tq
    kern = functools.partial(_na_kernel, n_ctx=n_ctx, n_row_tiles=n_tiles)

    def bias_map(b, h, i):
        return (h, jnp.where(i == 0, 0, jnp.where(i == n_tiles - 1, 2, 1)), 0, 0)

    return pl.pallas_call(
        kern,
        grid=(bsz, n_heads, n_tiles),
        in_specs=[pl.BlockSpec((1, 1, tq, d), lambda b, h, i: (b, h, i + ctx_tiles, 0)),
                  pl.BlockSpec((1, 1, n_tok, d), lambda b, h, i: (b, h, 0, 0)),
                  pl.BlockSpec((1, 1, n_tok, d), lambda b, h, i: (b, h, 0, 0)),
                  pl.BlockSpec((1, 1) + bias.shape[2:], bias_map)],
        out_specs=pl.BlockSpec((1, 1, tq, d), lambda b, h, i: (b, h, i, 0)),
        out_shape=jax.ShapeDtypeStruct((bsz, n_heads, n_tok - n_ctx, d), BF16),
        compiler_params=_params("parallel", "parallel", "parallel"),
        name="neighbourhood_attention",
    )(q, k, v, bias)


def _softplus(z):
    return jnp.maximum(z, 0.0) + jnp.log(1.0 + jnp.exp(-jnp.abs(z)))


def _sigmoid(z):
    return 1.0 / (1.0 + jnp.exp(-z))


def _hdot(a, b):
    return jnp.dot(a, b, precision=lax.Precision.HIGHEST, preferred_element_type=F32)


def _rwkv_prep_kernel(u_ref, halo_ref, taps_ref, w0_ref, w2_ref, a0_ref, a2_ref, kk_ref, ka_ref, rk_ref,
                      g2_ref, *out_refs):
    dir_refs = (out_refs[0:6], out_refs[6:12])
    v_out, gt_refs, bonus_ref, gate_ref = out_refs[12], out_refs[13:15], out_refs[15], out_refs[16]
    tm = u_ref.shape[1]
    c = RWKV_CHUNK
    n_chunks = tm // c
    gw = GROUP_WIDTH
    u = u_ref[0]
    halo = halo_ref[0, 0]
    tok = lax.broadcasted_iota(jnp.int32, u.shape, 0)
    prev = jnp.where(tok == 0, halo[0:1], pltpu.roll(u, 1, axis=0))
    nxt = jnp.where(tok == tm - 1, halo[1:2], pltpu.roll(u, tm - 1, axis=0))
    taps = taps_ref[...]
    s = prev * taps[0:1] + u * taps[1:2] + nxt * taps[2:3]

    r, k, v = s[:, 0:gw], s[:, gw:2 * gw], s[:, 2 * gw:3 * gw]
    lo = 3 * gw
    w_low = jnp.tanh(s[:, lo:lo + 2 * RWKV_DECAY_RANK])
    lo += 2 * RWKV_DECAY_RANK
    a_low = s[:, lo:lo + 2 * RWKV_ICLR_RANK]
    lo += 2 * RWKV_ICLR_RANK
    g_low = _sigmoid(s[:, lo:lo + RWKV_GATE_RANK])
    w_log = -_softplus(-(w0_ref[...] + _bdot(w_low, w2_ref[...]))) - 0.5
    log_decay = -jnp.exp(w_log)
    iclr = _sigmoid(a0_ref[...] + _bdot(a_low, a2_ref[...]))
    gate_ref[0] = _bdot(g_low, g2_ref[...])

    row = lax.broadcasted_iota(jnp.int32, (tm, tm), 0)
    col = lax.broadcasted_iota(jnp.int32, (tm, tm), 1)
    shift = c.bit_length() - 1
    same = (row >> shift) == (col >> shift)
    ones_blk = same.astype(F32)

    def head_major(ref, val):
        for h in range(N_HEADS):
            ref[0, h] = val[:, h * HEAD_DIM:(h + 1) * HEAD_DIM].astype(ref.dtype)

    head_major(v_out, v)
    bonus = jnp.zeros_like(r)
    for d in range(2):
        sl = slice(d * gw, (d + 1) * gw)
        lw = log_decay[:, sl]
        kk = k * kk_ref[:, sl]
        kk = kk * lax.rsqrt(jnp.maximum(_hdot(kk * kk, ones_blk), 1e-24))
        k_d = k * (1.0 + (iclr[:, sl] - 1.0) * ka_ref[:, sl])
        b_vec = kk * iclr[:, sl]
        tri = (same & ((row >= col) if d == 0 else (row <= col))).astype(F32)
        cum = _hdot(tri, lw)
        ends = [cum[i * c + (c - 1 if d == 0 else 0)][None] for i in range(n_chunks)]
        total = jnp.concatenate([jnp.broadcast_to(e, (c, gw)) for e in ends], axis=0)
        e_inv = jnp.exp(-cum)
        e_end = jnp.exp(total - cum)
        streams = (r * jnp.exp(cum), -kk * jnp.exp(cum - lw), b_vec * e_inv, k_d * e_inv,
                   b_vec * e_end, k_d * e_end)
        for ref, val in zip(dir_refs[d], streams):
            head_major(ref, val)
        g_tot = jnp.exp(jnp.concatenate(ends, axis=0))
        for h in range(N_HEADS):
            gt_refs[d][0, 0, h * n_chunks:(h + 1) * n_chunks, :] = g_tot[:, h * HEAD_DIM:(h + 1) * HEAD_DIM]
        bonus = bonus + r * k_d * rk_ref[:, sl]
    bonus_ref[0] = _hdot(bonus, ones_blk) * v


def rwkv_prepare(rw, halo, taps, w0, w2, a0, a2, k_k, k_a, r_k, g2):
    bsz, n_tok, width = rw.shape
    tm = TOKEN_TILE
    n_tiles = n_tok // tm
    n_chunks = tm // RWKV_CHUNK
    full = lambda b, i: (0, 0)
    consts = (taps, w0, w2, a0, a2, k_k, k_a, r_k, g2)
    stream = jax.ShapeDtypeStruct((bsz, N_HEADS, n_tok, HEAD_DIM), BF16)
    stream_spec = pl.BlockSpec((1, N_HEADS, tm, HEAD_DIM), lambda b, i: (b, 0, i, 0))
    decay = jax.ShapeDtypeStruct((bsz, n_tiles, N_HEADS * n_chunks, HEAD_DIM), F32)
    decay_spec = pl.BlockSpec((1, 1, N_HEADS * n_chunks, HEAD_DIM), lambda b, i: (b, i, 0, 0))
    tokens = jax.ShapeDtypeStruct((bsz, n_tok, GROUP_WIDTH), F32)
    tokens_spec = pl.BlockSpec((1, tm, GROUP_WIDTH), lambda b, i: (b, i, 0))
    return pl.pallas_call(
        _rwkv_prep_kernel,
        grid=(bsz, n_tiles),
        in_specs=[pl.BlockSpec((1, tm, width), lambda b, i: (b, i, 0)),
                  pl.BlockSpec((1, 1, 2, width), lambda b, i: (b, i, 0, 0))]
                 + [pl.BlockSpec(t.shape, full) for t in consts],
        out_specs=[stream_spec] * 13 + [decay_spec] * 2 + [tokens_spec] * 2,
        out_shape=[stream] * 13 + [decay] * 2 + [tokens] * 2,
        compiler_params=_params("parallel", "parallel"),
        name="rwkv_prepare",
    )(rw, halo, *consts)


def _bmm(a, b):
    return lax.dot_general(a.astype(BF16), b.astype(BF16), (((2,), (1,)), ((0,), (0,))),
                           preferred_element_type=F32)


def _bmm_nt(a, b):
    return lax.dot_general(a.astype(BF16), b.astype(BF16), (((2,), (2,)), ((0,), (0,))),
                           preferred_element_type=F32)


def _bmm_tn(a, b):
    return lax.dot_general(a.astype(BF16), b.astype(BF16), (((1,), (1,)), ((0,), (0,))),
                           preferred_element_type=F32)


def _rwkv_chunk_kernel(rt_ref, at_ref, bt_ref, kt_ref, be_ref, ke_ref, v_ref, gt_ref, y_ref, s_ref,
                       *, reverse):
    n_heads, tb, n = v_ref.shape[1:]
    c = RWKV_CHUNK
    n_chunks = tb // c
    g = n_heads * n_chunks

    @pl.when(pl.program_id(1) == 0)
    def _():
        s_ref[...] = jnp.zeros_like(s_ref)

    load = lambda ref: ref[0].reshape(g, c, n)
    rt, at, bt, kt, be, ke, v = (load(ref) for ref in (rt_ref, at_ref, bt_ref, kt_ref, be_ref, ke_ref, v_ref))
    row = lax.broadcasted_iota(jnp.int32, (c, c), 0)
    col = lax.broadcasted_iota(jnp.int32, (c, c), 1)
    if reverse:
        row, col = col, row
    strict = row > col
    incl = row >= col
    a_ab = jnp.where(strict, _bmm_nt(at, bt), 0.0)
    a_ak = jnp.where(strict, _bmm_nt(at, kt), 0.0)
    a_rb = jnp.where(incl, _bmm_nt(rt, bt), 0.0)
    a_rk = jnp.where(incl, _bmm_nt(rt, kt), 0.0)
    av = _bmm(a_ak, v)
    n1 = jnp.where((row >> 3) == (col >> 3), a_ab, 0.0)
    n2 = _bmm(n1, n1)
    n4 = _bmm(n2, n2)
    p = (row == col).astype(F32) + n1 + n2 + _bmm(n1, n2)
    inv = p + _bmm(p, n4)
    for level in range(3, c.bit_length() - 1):
        rb, cb = row >> level, col >> level
        off = jnp.where((rb == cb + 1) & ((rb & 1) == 1), a_ab, 0.0)
        inv = inv + _bmm(_bmm(inv, off), inv)
    w = _bmm(inv, at)
    u0 = _bmm(inv, av)
    q_eff = rt.astype(F32) + _bmm(a_rb, w)
    y0 = _bmm(a_rb, u0) + _bmm(a_rk, v)

    per_chunk = lambda t: t.reshape((n_heads, n_chunks) + t.shape[1:])
    q_eff, w, u0, y0, be, ke, v = (per_chunk(t) for t in (q_eff, w, u0, y0, be, ke, v))
    g_tot = gt_ref[0, 0].reshape(n_heads, n_chunks, 1, n)
    state = s_ref[...]
    for i in (range(n_chunks - 1, -1, -1) if reverse else range(n_chunks)):
        y_ref[0, :, i * c:(i + 1) * c, :] = _bmm_nt(q_eff[:, i], state) + y0[:, i]
        u = _bmm_nt(w[:, i], state) + u0[:, i]
        state = g_tot[:, i] * state + _bmm_tn(u, be[:, i]) + _bmm_tn(v[:, i], ke[:, i])
    s_ref[...] = state


def rwkv_chunk_scan(streams, v, g_tot, ctx_blocks, reverse):
    bsz, n_heads, n_tok, n = v.shape
    tb = TOKEN_TILE
    n_blocks = n_tok // tb

    def block(i):
        if not reverse:
            return i
        return jnp.where(i < ctx_blocks, ctx_blocks - 1 - i, n_blocks - 1 + ctx_blocks - i)

    spec = pl.BlockSpec((1, n_heads, tb, n), lambda b, i: (b, 0, block(i), 0))
    return pl.pallas_call(
        functools.partial(_rwkv_chunk_kernel, reverse=reverse),
        grid=(bsz, n_blocks),
        in_specs=[spec] * 7 + [pl.BlockSpec((1, 1) + g_tot.shape[2:], lambda b, i: (b, block(i), 0, 0))],
        out_specs=spec,
        out_shape=jax.ShapeDtypeStruct((bsz, n_heads, n_tok, n), F32),
        scratch_shapes=[pltpu.VMEM((n_heads, n, n), F32)],
        compiler_params=_params("parallel", "arbitrary"),
        name="rwkv_chunk_scan_bwd" if reverse else "rwkv_chunk_scan_fwd",
    )(*streams, v, g_tot)


def _rwkv_post_kernel(yf_ref, yb_ref, bonus_ref, gate_ref, lw_ref, lb_ref, o_ref):
    for h in range(N_HEADS):
        sl = slice(h * HEAD_DIM, (h + 1) * HEAD_DIM)
        y = yf_ref[0, h] + yb_ref[0, h]
        mu = jnp.mean(y, axis=-1, keepdims=True)
        var = jnp.mean(jnp.square(y - mu), axis=-1, keepdims=True)
        yn = (y - mu) * lax.rsqrt(var + GN_EPS)
        yn = yn * lw_ref[:, sl] + lb_ref[:, sl] + bonus_ref[0, :, sl]
        o_ref[0, :, sl] = (yn * gate_ref[0, :, sl]).astype(o_ref.dtype)


def rwkv_finish(y_f, y_b, bonus, gate, lnx_w, lnx_b):
    bsz, n_heads, n_tok, n = y_f.shape
    tm = TOKEN_TILE
    heads = pl.BlockSpec((1, n_heads, tm, n), lambda b, i: (b, 0, i, 0))
    tokens = pl.BlockSpec((1, tm, n_heads * n), lambda b, i: (b, i, 0))
    vec = pl.BlockSpec((1, n_heads * n), lambda b, i: (0, 0))
    return pl.pallas_call(
        _rwkv_post_kernel,
        grid=(bsz, n_tok // tm),
        in_specs=[heads, heads, tokens, tokens, vec, vec],
        out_specs=tokens,
        out_shape=jax.ShapeDtypeStruct((bsz, n_tok, n_heads * n), BF16),
        compiler_params=_params("parallel", "parallel"),
        name="rwkv_finish",
    )(y_f, y_b, bonus, gate, lnx_w, lnx_b)


def _rope_tables(n_ctx, n_lat, rot_dim):
    t = jnp.arange(n_lat, dtype=jnp.int32)
    rows = (t // GRID_W).astype(F32)
    cols = (t % GRID_W).astype(F32)
    per_axis = rot_dim // 2
    inv_freq = ROPE_THETA ** (-jnp.arange(0, per_axis, 2, dtype=F32) / per_axis)
    ang = jnp.concatenate([rows[:, None] * inv_freq, cols[:, None] * inv_freq], axis=-1)
    ang = jnp.concatenate([jnp.zeros((n_ctx, ang.shape[1]), F32), ang], axis=0)
    return jnp.cos(ang), jnp.sin(ang)


def _rope(x, cos, sin):
    c = cos[None, :, None, :]
    s = sin[None, :, None, :]
    x1, x2 = x[..., 0::2], x[..., 1::2]
    return jnp.stack([x1 * c - x2 * s, x1 * s + x2 * c], axis=-1).reshape(x.shape)


def _heads(u, n_heads):
    return u.reshape(u.shape[:-1] + (n_heads, u.shape[-1] // n_heads))


def _head_major(u):
    return jnp.swapaxes(u, 1, 2)


def _tok_matmul(u, w, out_dtype=F32):
    lead = u.shape[:-1]
    return matmul(u.reshape(-1, u.shape[-1]), w.astype(BF16), out_dtype).reshape(lead + (w.shape[1],))


def _block_diag(m0, m1):
    z01 = jnp.zeros((m0.shape[0], m1.shape[1]), m0.dtype)
    z10 = jnp.zeros((m1.shape[0], m0.shape[1]), m0.dtype)
    return jnp.concatenate([jnp.concatenate([m0, z01], axis=1), jnp.concatenate([z10, m1], axis=1)], axis=0)


def _shift_halo(rw, ctx_tiles):
    tm = TOKEN_TILE
    zero = jnp.zeros_like(rw[:, :1])
    prev = jnp.concatenate([zero, rw[:, tm - 1::tm][:, :-1]], axis=1)
    nxt = jnp.concatenate([rw[:, ::tm][:, 1:], zero], axis=1)
    tile = jnp.arange(prev.shape[1])[None, :, None]
    prev = jnp.where(tile == ctx_tiles, 0.0, prev)
    nxt = jnp.where(tile == ctx_tiles - 1, 0.0, nxt)
    return jnp.stack([prev, nxt], axis=2)


def _rwkv_mixer(rw, ctx_tiles, shift_taps, w0, w2, a0, a2, k_k, k_a, r_k, g2, lnx_w, lnx_b):
    both = lambda t: t.reshape(1, -1)
    outs = rwkv_prepare(rw, _shift_halo(rw, ctx_tiles), shift_taps, both(w0),
                        _block_diag(w2[0], w2[1]).astype(BF16), both(a0),
                        _block_diag(a2[0], a2[1]).astype(BF16), both(k_k), both(k_a), both(r_k),
                        g2.astype(BF16))
    v, bonus, gate = outs[12], outs[15], outs[16]
    y_f = rwkv_chunk_scan(outs[0:6], v, outs[13], ctx_tiles, False)
    y_b = rwkv_chunk_scan(outs[6:12], v, outs[14], ctx_tiles, True)
    return rwkv_finish(y_f, y_b, bonus, gate, lnx_w[None], lnx_b[None])


def _head_rms(u, g):
    return u * lax.rsqrt(jnp.mean(u * u, axis=-1, keepdims=True) + RMS_EPS) * g


def _merge_heads(o_lat, o_ctx):
    bsz, n_heads, n_lat, d = o_lat.shape
    if o_ctx is None:
        o_ctx = jnp.zeros((bsz, n_heads, 0, d), o_lat.dtype)
    o = jnp.swapaxes(jnp.concatenate([o_ctx, o_lat], axis=2), 1, 2)
    return o.reshape(bsz, o.shape[1], n_heads * d)


def _with_ones(v):
    fill = jnp.zeros(v.shape[:-1] + (V_EXT_WIDTH - v.shape[-1],), v.dtype).at[..., 0].set(1)
    return jnp.concatenate([v, fill], axis=-1)


def _dense_branch(q, k, v, n_ctx, want_ctx):
    bsz, hk, grp, n_tok, _ = q.shape
    dv = v.shape[-1]
    v = _with_ones(v)
    ctx_tiles = n_ctx // ATTN_Q_TILE
    o_lat = attention(q, k, v, ctx_tiles, (n_tok - n_ctx) // ATTN_Q_TILE, n_tok, dv)
    o_lat = o_lat.reshape(bsz, hk * grp, n_tok - n_ctx, dv)
    o_ctx = None
    if want_ctx:
        o_ctx = attention(q, k, v, 0, ctx_tiles, n_ctx, dv).reshape(bsz, hk * grp, n_ctx, dv)
    return o_lat, o_ctx


def kernel(x, c, ctx, c_ctx, w_mod, b_mod, norm1_g, norm2_g, w_in, rwkv_shift, na_rpb, gqa_q_norm, gqa_k_norm, mla_q_norm, mla_kv_norm, mla_w_uq, mla_w_ukv, rwkv_w0, rwkv_w2, rwkv_a0, rwkv_a2, rwkv_k_k, rwkv_k_a, rwkv_r_k, rwkv_g2, rwkv_lnx_w, rwkv_lnx_b, w_out, w_fc1, w_fc2, final_norm_g):
    bsz, n_lat, d_model = x.shape
    n_ctx = ctx.shape[1]
    depth = w_mod.shape[0]
    rows = n_lat // GRID_W
    ctx_tiles = n_ctx // TOKEN_TILE
    assert n_ctx % TOKEN_TILE == 0 and n_lat % TOKEN_TILE == 0 and rows % NA_Q_ROWS == 0

    cos_h, sin_h = _rope_tables(n_ctx, n_lat, HEAD_DIM)
    cos_m, sin_m = _rope_tables(n_ctx, n_lat, MLA_ROPE_DIM)

    cond = jnp.concatenate([jax.nn.silu(c), jax.nn.silu(c_ctx)[None]], axis=0)
    n_cond = -(-cond.shape[0] // 8) * 8
    cond = jnp.pad(cond, ((0, n_cond - cond.shape[0]), (0, 0)))
    mod_all = modulation(cond, w_mod, b_mod[:, None, :])

    x_all = jnp.concatenate([ctx, x], axis=1)
    qk_scale = HEAD_DIM ** -0.5
    qk_scale2 = LOG2E * qk_scale
    mla_scale2 = LOG2E * MLA_QK_DIM ** -0.5

    for layer in range(depth):
        want_ctx = layer < depth - 1
        mod = mod_all[layer].reshape(n_cond, 6, d_model)
        mod = jnp.stack([jnp.broadcast_to(mod[bsz], (bsz, 6, d_model)), mod[:bsz]], axis=1)

        na, gqa, mla, rw = in_projection(x_all, mod, norm1_g[layer][None], w_in[layer].astype(BF16), ctx_tiles)

        a_q, a_k, a_v = [_head_major(_heads(t, N_HEADS)) for t in jnp.split(na, 3, axis=-1)]
        out_a = neighbourhood_attention(a_q * jnp.asarray(qk_scale, BF16), a_k, a_v,
                                        na_bias_table(na_rpb[layer], rows), n_ctx)
        out_ac = None
        if want_ctx:
            a_q2 = (a_q.astype(F32) * qk_scale2).astype(BF16)[:, :, None]
            out_ac = attention(a_q2, a_k, _with_ones(a_v), 0, ctx_tiles, n_ctx, HEAD_DIM)[:, :, 0]
        mix_a = _merge_heads(out_a, out_ac)

        b_q = _head_rms(_heads(gqa[..., :GROUP_WIDTH], N_HEADS), gqa_q_norm[layer])
        b_k, b_v = jnp.split(gqa[..., GROUP_WIDTH:], 2, axis=-1)
        b_k = _head_rms(_heads(b_k, GQA_KV_HEADS), gqa_k_norm[layer])
        b_q = _head_major(_rope(b_q, cos_h, sin_h) * qk_scale2).astype(BF16)
        b_q = b_q.reshape(bsz, GQA_KV_HEADS, N_HEADS // GQA_KV_HEADS, b_q.shape[2], HEAD_DIM)
        b_k = _head_major(_rope(b_k, cos_h, sin_h)).astype(BF16)
        b_v = _head_major(_heads(b_v, GQA_KV_HEADS)).astype(BF16)
        mix_b = _merge_heads(*_dense_branch(b_q, b_k, b_v, n_ctx, want_ctx))

        c_q = _tok_matmul(_head_rms(mla[..., :MLA_Q_RANK], mla_q_norm[layer]), mla_w_uq[layer])
        c_q = _heads(c_q, N_HEADS)
        c_q = jnp.concatenate([c_q[..., :HEAD_DIM], _rope(c_q[..., HEAD_DIM:], cos_m, sin_m)], axis=-1)
        c_q = _head_major(c_q * mla_scale2).astype(BF16)[:, :, None]
        c_kv = _tok_matmul(_head_rms(mla[..., MLA_Q_RANK:MLA_Q_RANK + MLA_KV_RANK], mla_kv_norm[layer]),
                           mla_w_ukv[layer])
        c_kv = _heads(c_kv, N_HEADS)
        c_kr = _rope(mla[..., MLA_Q_RANK + MLA_KV_RANK:][:, :, None, :], cos_m, sin_m)
        c_k = jnp.concatenate([c_kv[..., :HEAD_DIM],
                               jnp.broadcast_to(c_kr, c_kv.shape[:-1] + (MLA_ROPE_DIM,))], axis=-1)
        c_k = _head_major(c_k).astype(BF16)
        c_v = _head_major(c_kv[..., HEAD_DIM:]).astype(BF16)
        mix_c = _merge_heads(*_dense_branch(c_q, c_k, c_v, n_ctx, want_ctx))

        mix_d = _rwkv_mixer(rw, ctx_tiles, rwkv_shift[layer], rwkv_w0[layer], rwkv_w2[layer], rwkv_a0[layer],
                            rwkv_a2[layer], rwkv_k_k[layer], rwkv_k_a[layer], rwkv_r_k[layer],
                            rwkv_g2[layer], rwkv_lnx_w[layer], rwkv_lnx_b[layer])

        if not want_ctx:
            pad = lambda t: jnp.pad(t, ((0, 0), (n_ctx, 0), (0, 0)))
            mix_a, mix_b, mix_c = pad(mix_a), pad(mix_b), pad(mix_c)
        mix = jnp.concatenate([mix_a, mix_b, mix_c, mix_d], axis=-1)

        x_all = out_projection_mlp(x_all, mix, mod, norm2_g[layer][None], final_norm_g[None],
                                   w_out[layer].astype(BF16), w_fc1[layer].astype(BF16),
                                   w_fc2[layer].astype(BF16), ctx_tiles, layer == depth - 1)
    return x_all[:, n_ctx:]
```

```python
import functools

import numpy as np
import jax
import jax.numpy as jnp
from jax import lax
from jax.experimental import pallas as pl
from jax.experimental.pallas import tpu as pltpu

F32 = jnp.float32
BF16 = jnp.bfloat16

GRID_W = 64
HEAD_DIM = 64
N_HEADS = 4
GROUP_WIDTH = N_HEADS * HEAD_DIM
GQA_KV_HEADS = 2
NA_WIN_ROWS = 8
NA_WIN_COLS = 16
MLA_Q_RANK = 256
MLA_KV_RANK = 128
MLA_ROPE_DIM = 32
MLA_QK_DIM = HEAD_DIM + MLA_ROPE_DIM
RWKV_DECAY_RANK = 64
RWKV_ICLR_RANK = 64
RWKV_GATE_RANK = 160
NA_COLS = 3 * GROUP_WIDTH
GQA_COLS = (N_HEADS + 2 * GQA_KV_HEADS) * HEAD_DIM
MLA_COLS = MLA_Q_RANK + MLA_KV_RANK + MLA_ROPE_DIM
RWKV_COLS = 3 * GROUP_WIDTH + 2 * RWKV_DECAY_RANK + 2 * RWKV_ICLR_RANK + RWKV_GATE_RANK
ROPE_THETA = 10000.0
RMS_EPS = 1e-6
GN_EPS = 64e-5
NEG_INF = -1e30

TOKEN_TILE = 256
ATTN_Q_TILE = 256
ATTN_KV_CHUNK = 512
V_EXT_WIDTH = 128
LOG2E = float(np.log2(np.e))
NA_Q_ROWS = ATTN_Q_TILE // GRID_W
NA_KEY_ROWS = NA_Q_ROWS + NA_WIN_ROWS - 1
RWKV_CHUNK = 64
VMEM_LIMIT = 56 * 1024 * 1024


def _params(*sem):
    return pltpu.CompilerParams(dimension_semantics=sem, vmem_limit_bytes=VMEM_LIMIT)


def _bdot(a, b):
    return jnp.dot(a.astype(BF16), b.astype(BF16), preferred_element_type=F32)


def _bdot_nt(a, b):
    return lax.dot_general(a.astype(BF16), b.astype(BF16), (((1,), (1,)), ((), ())),
                           preferred_element_type=F32)


def _bdot_tn(a, b):
    return lax.dot_general(a.astype(BF16), b.astype(BF16), (((0,), (0,)), ((), ())),
                           preferred_element_type=F32)


def _matmul_kernel(a_ref, b_ref, o_ref):
    o_ref[...] = _bdot(a_ref[...], b_ref[...]).astype(o_ref.dtype)


def matmul(a, b, out_dtype=F32, tm=1024):
    m, k = a.shape
    n = b.shape[1]
    tm = min(tm, m)
    while m % tm:
        tm //= 2
    return pl.pallas_call(
        _matmul_kernel,
        grid=(m // tm,),
        in_specs=[pl.BlockSpec((tm, k), lambda i: (i, 0)),
                  pl.BlockSpec((k, n), lambda i: (0, 0))],
        out_specs=pl.BlockSpec((tm, n), lambda i: (i, 0)),
        out_shape=jax.ShapeDtypeStruct((m, n), out_dtype),
        compiler_params=_params("parallel"),
        name="matmul",
    )(a, b)


def _mod_kernel(a_ref, w_ref, b_ref, o_ref):
    o_ref[0] = _bdot(a_ref[...], w_ref[0]) + b_ref[0]


def modulation(cond, w_mod, b_mod, tn=1536):
    depth, d, n = w_mod.shape
    r = cond.shape[0]
    return pl.pallas_call(
        _mod_kernel,
        grid=(depth, n // tn),
        in_specs=[pl.BlockSpec((r, d), lambda l, j: (0, 0)),
                  pl.BlockSpec((1, d, tn), lambda l, j: (l, 0, j)),
                  pl.BlockSpec((1, 1, tn), lambda l, j: (l, 0, j))],
        out_specs=pl.BlockSpec((1, r, tn), lambda l, j: (l, 0, j)),
        out_shape=jax.ShapeDtypeStruct((depth, r, n), F32),
        compiler_params=_params("parallel", "parallel"),
        name="modulation",
    )(cond, w_mod, b_mod)


def _rms(x, g):
    return x * lax.rsqrt(jnp.mean(x * x, axis=-1, keepdims=True) + RMS_EPS) * g


def _hdot(a, b):
    return jnp.dot(a, b, precision=lax.Precision.HIGHEST, preferred_element_type=F32)


def _pair_swap(x):
    axis = x.ndim - 1
    lane = lax.broadcasted_iota(jnp.int32, x.shape, axis)
    return jnp.where((lane & 1) == 0, pltpu.roll(x, x.shape[axis] - 1, axis=axis), pltpu.roll(x, 1, axis=axis))


def _rotate(x, cos, sin):
    return x * cos + _pair_swap(x) * sin


def _in_proj_kernel(x_ref, mod_ref, g_ref, wna_ref, wgqa_ref, wmla_ref, wrw_ref, gq_ref, gk_ref,
                    mq_ref, mkv_ref, wuq_ref, wukv_ref, cosh_ref, sinh_ref, cosm_ref, sinm_ref,
                    naq_ref, nak_ref, nav_ref, bq_ref, bk_ref, bv_ref, cq_ref, ck_ref, cv_ref, rw_ref):
    hd, gw = HEAD_DIM, GROUP_WIDTH
    mod = mod_ref[0, 0]
    hb = (_rms(x_ref[0], g_ref[...]) * (1.0 + mod[1:2]) + mod[0:1]).astype(BF16)
    tm = hb.shape[0]
    ones_col = (lax.broadcasted_iota(jnp.int32, (tm, V_EXT_WIDTH - hd), 1) == 0).astype(F32)
    qk_scale = LOG2E * hd ** -0.5

    def put_heads(ref, val, n_heads, width):
        for h in range(n_heads):
            ref[0, h] = val[:, h * width:(h + 1) * width].astype(ref.dtype)

    def put_values(ref, val, n_heads):
        for h in range(n_heads):
            ref[0, h] = jnp.concatenate([val[:, h * hd:(h + 1) * hd], ones_col], axis=-1).astype(ref.dtype)

    na = jnp.dot(hb, wna_ref[...], preferred_element_type=F32)
    put_heads(naq_ref, na[:, :gw] * qk_scale, N_HEADS, hd)
    put_heads(nak_ref, na[:, gw:2 * gw], N_HEADS, hd)
    put_values(nav_ref, na[:, 2 * gw:], N_HEADS)

    gqa = jnp.dot(hb, wgqa_ref[...], preferred_element_type=F32)
    row = lax.broadcasted_iota(jnp.int32, (gw, gw), 0)
    col = lax.broadcasted_iota(jnp.int32, (gw, gw), 1)
    shift = hd.bit_length() - 1
    head_sum = ((row >> shift) == (col >> shift)).astype(F32)
    cos_h, sin_h = cosh_ref[...], sinh_ref[...]
    kw = GQA_KV_HEADS * hd
    q = gqa[:, :gw]
    q = q * lax.rsqrt(_hdot(q * q, head_sum) * (1.0 / hd) + RMS_EPS) * gq_ref[...]
    put_heads(bq_ref, _rotate(q, cos_h, sin_h) * qk_scale, N_HEADS, hd)
    k = gqa[:, gw:gw + kw]
    k = k * lax.rsqrt(_hdot(k * k, head_sum[:kw, :kw]) * (1.0 / hd) + RMS_EPS) * gk_ref[...]
    put_heads(bk_ref, _rotate(k, cos_h[:, :kw], sin_h[:, :kw]), GQA_KV_HEADS, hd)
    put_values(bv_ref, gqa[:, gw + kw:], GQA_KV_HEADS)

    mla = jnp.dot(hb, wmla_ref[...], preferred_element_type=F32)
    cos_m, sin_m = cosm_ref[...], sinm_ref[...]
    mla_scale = LOG2E * MLA_QK_DIM ** -0.5
    uq = jnp.dot(_rms(mla[:, :MLA_Q_RANK], mq_ref[...]).astype(BF16), wuq_ref[...], preferred_element_type=F32)
    q_nope = uq[:, :gw] * mla_scale
    q_rope = _rotate(uq[:, gw:], cos_m, sin_m) * mla_scale
    lo = MLA_Q_RANK + MLA_KV_RANK
    ukv = jnp.dot(_rms(mla[:, MLA_Q_RANK:lo], mkv_ref[...]).astype(BF16), wukv_ref[...],
                  preferred_element_type=F32)
    k_rope = _rotate(mla[:, lo:], cos_m, sin_m)[:, :MLA_ROPE_DIM]
    rd = MLA_ROPE_DIM
    for h in range(N_HEADS):
        cq_ref[0, h] = jnp.concatenate([q_nope[:, h * hd:(h + 1) * hd], q_rope[:, h * rd:(h + 1) * rd]],
                                       axis=-1).astype(cq_ref.dtype)
        ck_ref[0, h] = jnp.concatenate([ukv[:, h * hd:(h + 1) * hd], k_rope], axis=-1).astype(ck_ref.dtype)
    put_values(cv_ref, ukv[:, gw:], N_HEADS)

    rw_ref[0] = jnp.dot(hb, wrw_ref[...], preferred_element_type=F32)


def in_projection(x_all, mod, g, weights, tables, ctx_tiles):
    bsz, n_tok, d = x_all.shape
    tm = TOKEN_TILE
    tok = lambda b, i: (b, i, 0)
    full = lambda b, i: (0, 0)
    heads = lambda n, w: (jax.ShapeDtypeStruct((bsz, n, n_tok, w), BF16),
                          pl.BlockSpec((1, n, tm, w), lambda b, i: (b, 0, i, 0)))
    outs = [heads(N_HEADS, HEAD_DIM), heads(N_HEADS, HEAD_DIM), heads(N_HEADS, V_EXT_WIDTH),
            heads(N_HEADS, HEAD_DIM), heads(GQA_KV_HEADS, HEAD_DIM), heads(GQA_KV_HEADS, V_EXT_WIDTH),
            heads(N_HEADS, MLA_QK_DIM), heads(N_HEADS, MLA_QK_DIM), heads(N_HEADS, V_EXT_WIDTH),
            (jax.ShapeDtypeStruct((bsz, n_tok, RWKV_COLS), F32), pl.BlockSpec((1, tm, RWKV_COLS), tok))]
    return pl.pallas_call(
        _in_proj_kernel,
        grid=(bsz, n_tok // tm),
        in_specs=[pl.BlockSpec((1, tm, d), tok),
                  pl.BlockSpec((1, 1, 6, d), lambda b, i: (b, jnp.where(i < ctx_tiles, 0, 1), 0, 0)),
                  pl.BlockSpec((1, d), full)]
                 + [pl.BlockSpec(w.shape, full) for w in weights]
                 + [pl.BlockSpec((tm, t.shape[1]), lambda b, i: (i, 0)) for t in tables],
        out_specs=[o[1] for o in outs],
        out_shape=[o[0] for o in outs],
        compiler_params=_params("parallel", "parallel"),
        name="in_projection",
    )(x_all, mod, g, *weights, *tables)


def _out_mlp_kernel(*refs, hidden_tile, final_norm, ctx_tiles, n_branches):
    x_ref = refs[0]
    per_branch = 2 if ctx_tiles else 1
    branch_refs = refs[1:1 + per_branch * n_branches]
    mixd_ref, mod_ref, g2_ref, gf_ref, wout_ref, w1_ref, w2_ref, o_ref = refs[1 + per_branch * n_branches:]
    parts = []
    for j in range(n_branches):
        lat = branch_refs[per_branch * j][0]
        if ctx_tiles:
            lat = jnp.where(pl.program_id(1) < ctx_tiles, branch_refs[per_branch * j + 1][0], lat)
        parts.append(lat)
    parts.append(mixd_ref[0])
    mod = mod_ref[0, 0]
    proj = jnp.zeros(x_ref.shape[1:], F32)
    lo = 0
    for part in parts:
        proj = proj + jnp.dot(part, wout_ref[lo:lo + part.shape[1], :], preferred_element_type=F32)
        lo += part.shape[1]
    x1 = x_ref[0] + mod[2:3] * proj
    hb = (_rms(x1, g2_ref[...]) * (1.0 + mod[4:5]) + mod[3:4]).astype(BF16)
    acc = jnp.zeros_like(x1)
    for j in range(w1_ref.shape[1] // hidden_tile):
        cols = pl.ds(j * hidden_tile, hidden_tile)
        u = jnp.maximum(jnp.dot(hb, w1_ref[:, cols], preferred_element_type=F32), 0.0)
        acc = acc + jnp.dot((u * u).astype(BF16), w2_ref[cols, :], preferred_element_type=F32)
    x2 = x1 + mod[5:6] * acc
    if final_norm:
        x2 = _rms(x2, gf_ref[...])
    o_ref[0] = x2


def out_projection_mlp(x_all, branches, mix_d, mod, g2, g_final, w_out, w_fc1, w_fc2, ctx_tiles, last):
    bsz, n_tok, d = x_all.shape
    tm = TOKEN_TILE
    n_tiles = n_tok // tm
    skip = ctx_tiles if last else 0
    tok = lambda b, i: (b, i + skip, 0)
    full = lambda b, i: (0, 0)
    lat_map = lambda b, i: (b, jnp.maximum(i + skip - ctx_tiles, 0), 0)
    ctx_map = lambda b, i: (b, jnp.minimum(i, ctx_tiles - 1), 0)
    branch_args, branch_specs = [], []
    for lat, ctx_rows in branches:
        branch_args.append(lat)
        branch_specs.append(pl.BlockSpec((1, tm, lat.shape[2]), lat_map))
        if not last:
            branch_args.append(ctx_rows)
            branch_specs.append(pl.BlockSpec((1, tm, ctx_rows.shape[2]), ctx_map))
    kern = functools.partial(_out_mlp_kernel, hidden_tile=1024, final_norm=last,
                             ctx_tiles=0 if last else ctx_tiles, n_branches=len(branches))
    return pl.pallas_call(
        kern,
        grid=(bsz, n_tiles - skip),
        in_specs=[pl.BlockSpec((1, tm, d), tok)] + branch_specs
                 + [pl.BlockSpec((1, tm, mix_d.shape[2]), tok),
                    pl.BlockSpec((1, 1, 6, d), lambda b, i: (b, jnp.where(i + skip < ctx_tiles, 0, 1), 0, 0)),
                    pl.BlockSpec((1, d), full),
                    pl.BlockSpec((1, d), full),
                    pl.BlockSpec(w_out.shape, full, pipeline_mode=pl.Buffered(1)),
                    pl.BlockSpec(w_fc1.shape, full, pipeline_mode=pl.Buffered(1)),
                    pl.BlockSpec(w_fc2.shape, full, pipeline_mode=pl.Buffered(1))],
        out_specs=pl.BlockSpec((1, tm, d), lambda b, i: (b, i, 0)),
        out_shape=jax.ShapeDtypeStruct((bsz, n_tok - skip * tm, d), F32),
        compiler_params=_params("parallel", "parallel"),
        name="out_projection_mlp",
    )(x_all, *branch_args, mix_d, mod, g2, g_final, w_out, w_fc1, w_fc2)


def _key_chunks(n_keys):
    first = min(n_keys, ATTN_Q_TILE)
    chunks = [(0, first)]
    while chunks[-1][0] + chunks[-1][1] < n_keys:
        start = chunks[-1][0] + chunks[-1][1]
        chunks.append((start, min(ATTN_KV_CHUNK, n_keys - start)))
    return chunks


def _online_softmax(q, k_at, v_at, chunks):
    rows = q.shape[0]
    scores = lambda c: _bdot_nt(q, k_at(*c))
    m = jnp.full((rows, 1), NEG_INF, F32)
    acc = jnp.zeros((rows, V_EXT_WIDTH), F32)
    s_next = scores(chunks[0])
    for j, chunk in enumerate(chunks):
        s = s_next
        if j + 1 < len(chunks):
            s_next = scores(chunks[j + 1])
        m_new = jnp.maximum(m, jnp.max(s, axis=-1, keepdims=True))
        p = jnp.exp2(s - m_new)
        acc = acc * jnp.exp2(m - m_new) + jnp.dot(p.astype(BF16), v_at(*chunk), preferred_element_type=F32)
        m = m_new
    return acc


def _attention_kernel(q_ref, k_ref, v_ref, o_ref):
    n_q, tq, dk = q_ref.shape[2], q_ref.shape[3], q_ref.shape[4]
    dv = o_ref.shape[-1] // n_q
    chunks = _key_chunks(k_ref.shape[3])
    if k_ref.shape[2] == 1:
        groups = [(q_ref[0, 0].reshape(n_q * tq, dk), 0)]
    else:
        groups = [(q_ref[0, 0, g], g) for g in range(n_q)]
    outs = []
    for q, g in groups:
        acc = _online_softmax(q, lambda s, n: k_ref[0, 0, g, pl.ds(s, n), :],
                              lambda s, n: v_ref[0, 0, g, pl.ds(s, n), :], chunks)
        o = acc[:, :dv] / acc[:, dv:dv + 1]
        outs.extend(o[i * tq:(i + 1) * tq] for i in range(o.shape[0] // tq))
    o_ref[0] = jnp.concatenate(outs, axis=-1).astype(o_ref.dtype)


def attention(q, k, v, q_tile0, n_q_tiles, n_keys, dv):
    bsz, n_pairs, n_q, _, dk = q.shape
    n_kv = k.shape[2]
    tq = ATTN_Q_TILE
    return pl.pallas_call(
        _attention_kernel,
        grid=(bsz, n_pairs, n_q_tiles),
        in_specs=[pl.BlockSpec((1, 1, n_q, tq, dk), lambda b, h, i: (b, h, 0, i + q_tile0, 0)),
                  pl.BlockSpec((1, 1, n_kv, n_keys, dk), lambda b, h, i: (b, h, 0, 0, 0)),
                  pl.BlockSpec((1, 1, n_kv, n_keys, v.shape[-1]), lambda b, h, i: (b, h, 0, 0, 0))],
        out_specs=pl.BlockSpec((1, tq, n_q * dv), lambda b, h, i: (b, i, h)),
        out_shape=jax.ShapeDtypeStruct((bsz, n_q_tiles * tq, n_pairs * n_q * dv), BF16),
        compiler_params=_params("parallel", "parallel", "parallel"),
        name="attention",
    )(q, k, v)


def _na_kernel(q_ref, k_ref, v_ref, bias_ref, o_ref, *, n_ctx, n_row_tiles):
    i = pl.program_id(2)
    n_win = NA_KEY_ROWS * GRID_W
    max_row0 = n_row_tiles * NA_Q_ROWS - NA_KEY_ROWS
    row0 = jnp.clip(i * NA_Q_ROWS - NA_WIN_ROWS // 2, 0, max_row0)
    start = pl.multiple_of(n_ctx + row0 * GRID_W, GRID_W)
    dv = o_ref.shape[-1] // q_ref.shape[2]
    outs = []
    for g in range(q_ref.shape[2]):
        q = q_ref[0, 0, g]
        s_win = _bdot_nt(q, k_ref[0, 0, g, pl.ds(start, n_win), :]) + bias_ref[0, g, 0]
        s_ctx = _bdot_nt(q, k_ref[0, 0, g, pl.ds(0, n_ctx), :])
        m = jnp.maximum(jnp.max(s_win, axis=-1, keepdims=True), jnp.max(s_ctx, axis=-1, keepdims=True))
        p_win = jnp.exp2(s_win - m).astype(BF16)
        p_ctx = jnp.exp2(s_ctx - m).astype(BF16)
        o = (jnp.dot(p_win, v_ref[0, 0, g, pl.ds(start, n_win), :], preferred_element_type=F32)
             + jnp.dot(p_ctx, v_ref[0, 0, g, pl.ds(0, n_ctx), :], preferred_element_type=F32))
        outs.append(o[:, :dv] / o[:, dv:dv + 1])
    o_ref[0] = jnp.concatenate(outs, axis=-1).astype(o_ref.dtype)


def na_bias_table(rpb, rows):
    n_tiles = rows // NA_Q_ROWS
    n_heads, n_dr, n_dc = rpb.shape
    qc = np.arange(GRID_W)[:, None]
    kc = np.arange(GRID_W)[None, :]
    dc = np.clip(kc - qc + NA_WIN_COLS - 1, 0, n_dc - 1)
    onehot = jnp.asarray(dc.reshape(-1)[None, :] == np.arange(n_dc)[:, None], F32)
    by_col = jnp.dot(rpb.reshape(-1, n_dc).astype(F32), onehot, precision=lax.Precision.HIGHEST)
    pad = NA_KEY_ROWS
    by_col = jnp.pad(by_col.reshape(n_heads, n_dr, GRID_W, GRID_W), ((0, 0), (pad, pad), (0, 0), (0, 0)))
    c_start = np.clip(qc - NA_WIN_COLS // 2, 0, GRID_W - NA_WIN_COLS)
    col_ok = (kc >= c_start) & (kc < c_start + NA_WIN_COLS)
    tables = []
    for tile in (0, 1, n_tiles - 1):
        row0 = int(np.clip(tile * NA_Q_ROWS - NA_WIN_ROWS // 2, 0, rows - NA_KEY_ROWS))
        per_row = []
        for a in range(NA_Q_ROWS):
            qr = tile * NA_Q_ROWS + a
            dr0 = row0 - qr + NA_WIN_ROWS - 1 + pad
            kr = row0 + np.arange(NA_KEY_ROWS)
            r_start = int(np.clip(qr - NA_WIN_ROWS // 2, 0, rows - NA_WIN_ROWS))
            row_ok = (kr >= r_start) & (kr < r_start + NA_WIN_ROWS)
            valid = row_ok[None, :, None] & col_ok[:, None, :]
            sl = jnp.swapaxes(by_col[:, dr0:dr0 + NA_KEY_ROWS], 1, 2)
            per_row.append(jnp.where(valid[None], sl * LOG2E, NEG_INF))
        bias = jnp.stack(per_row, axis=1)
        tables.append(bias.reshape(n_heads, NA_Q_ROWS * GRID_W, NA_KEY_ROWS * GRID_W))
    return jnp.stack(tables, axis=1)


def neighbourhood_attention(q, k, v, bias, n_ctx):
    bsz, n_pairs, n_q, n_tok, d = q.shape
    tq = ATTN_Q_TILE
    n_tiles = (n_tok - n_ctx) // tq
    ctx_tiles = n_ctx // tq
    kern = functools.partial(_na_kernel, n_ctx=n_ctx, n_row_tiles=n_tiles)

    def bias_map(b, h, i):
        return (h, 0, jnp.where(i == 0, 0, jnp.where(i == n_tiles - 1, 2, 1)), 0, 0)

    whole = lambda b, h, i: (b, h, 0, 0, 0)
    return pl.pallas_call(
        kern,
        grid=(bsz, n_pairs, n_tiles),
        in_specs=[pl.BlockSpec((1, 1, n_q, tq, d), lambda b, h, i: (b, h, 0, i + ctx_tiles, 0)),
                  pl.BlockSpec((1, 1, n_q, n_tok, d), whole),
                  pl.BlockSpec((1, 1, n_q, n_tok, v.shape[-1]), whole),
                  pl.BlockSpec((1, n_q, 1) + bias.shape[3:], bias_map)],
        out_specs=pl.BlockSpec((1, tq, n_q * d), lambda b, h, i: (b, i, h)),
        out_shape=jax.ShapeDtypeStruct((bsz, n_tok - n_ctx, n_pairs * n_q * d), BF16),
        compiler_params=_params("parallel", "parallel", "parallel"),
        name="neighbourhood_attention",
    )(q, k, v, bias)


def _softplus(z):
    return jnp.maximum(z, 0.0) + jnp.log(1.0 + jnp.exp(-jnp.abs(z)))


def _sigmoid(z):
    return 1.0 / (1.0 + jnp.exp(-z))


def _rwkv_prep_kernel(u_ref, halo_ref, taps_ref, w0_ref, w2_ref, a0_ref, a2_ref, kk_ref, ka_ref, rk_ref,
                      g2_ref, *out_refs):
    dir_refs = (out_refs[0:6], out_refs[6:12])
    v_out, gt_refs, bonus_ref, gate_ref = out_refs[12], out_refs[13:15], out_refs[15], out_refs[16]
    tm = u_ref.shape[1]
    c = RWKV_CHUNK
    n_chunks = tm // c
    gw = GROUP_WIDTH
    u = u_ref[0]
    halo = halo_ref[0, 0]
    tok = lax.broadcasted_iota(jnp.int32, u.shape, 0)
    prev = jnp.where(tok == 0, halo[0:1], pltpu.roll(u, 1, axis=0))
    nxt = jnp.where(tok == tm - 1, halo[1:2], pltpu.roll(u, tm - 1, axis=0))
    taps = taps_ref[...]
    s = prev * taps[0:1] + u * taps[1:2] + nxt * taps[2:3]

    r, k, v = s[:, 0:gw], s[:, gw:2 * gw], s[:, 2 * gw:3 * gw]
    lo = 3 * gw
    w_low = jnp.tanh(s[:, lo:lo + 2 * RWKV_DECAY_RANK])
    lo += 2 * RWKV_DECAY_RANK
    a_low = s[:, lo:lo + 2 * RWKV_ICLR_RANK]
    lo += 2 * RWKV_ICLR_RANK
    g_low = _sigmoid(s[:, lo:lo + RWKV_GATE_RANK])
    w_log = -_softplus(-(w0_ref[...] + _bdot(w_low, w2_ref[...]))) - 0.5
    log_decay = -jnp.exp(w_log)
    iclr = _sigmoid(a0_ref[...] + _bdot(a_low, a2_ref[...]))
    gate_ref[0] = _bdot(g_low, g2_ref[...])

    row = lax.broadcasted_iota(jnp.int32, (tm, tm), 0)
    col = lax.broadcasted_iota(jnp.int32, (tm, tm), 1)
    shift = c.bit_length() - 1
    same = (row >> shift) == (col >> shift)
    ones_blk = same.astype(F32)

    def head_major(ref, val):
        for h in range(N_HEADS):
            ref[0, h] = val[:, h * HEAD_DIM:(h + 1) * HEAD_DIM].astype(ref.dtype)

    head_major(v_out, v)
    bonus = jnp.zeros_like(r)
    for d in range(2):
        sl = slice(d * gw, (d + 1) * gw)
        lw = log_decay[:, sl]
        kk = k * kk_ref[:, sl]
        kk = kk * lax.rsqrt(jnp.maximum(_hdot(kk * kk, ones_blk), 1e-24))
        k_d = k * (1.0 + (iclr[:, sl] - 1.0) * ka_ref[:, sl])
        b_vec = kk * iclr[:, sl]
        tri = (same & ((row >= col) if d == 0 else (row <= col))).astype(F32)
        cum = _hdot(tri, lw)
        ends = [cum[i * c + (c - 1 if d == 0 else 0)][None] for i in range(n_chunks)]
        total = jnp.concatenate([jnp.broadcast_to(e, (c, gw)) for e in ends], axis=0)
        e_inv = jnp.exp(-cum)
        e_end = jnp.exp(total - cum)
        streams = (r * jnp.exp(cum), -kk * jnp.exp(cum - lw), b_vec * e_inv, k_d * e_inv,
                   b_vec * e_end, k_d * e_end)
        for ref, val in zip(dir_refs[d], streams):
            head_major(ref, val)
        g_tot = jnp.exp(jnp.concatenate(ends, axis=0))
        for h in range(N_HEADS):
            gt_refs[d][0, 0, h * n_chunks:(h + 1) * n_chunks, :] = g_tot[:, h * HEAD_DIM:(h + 1) * HEAD_DIM]
        bonus = bonus + r * k_d * rk_ref[:, sl]
    bonus_ref[0] = _hdot(bonus, ones_blk) * v


def rwkv_prepare(rw, halo, taps, w0, w2, a0, a2, k_k, k_a, r_k, g2):
    bsz, n_tok, width = rw.shape
    tm = TOKEN_TILE
    n_tiles = n_tok // tm
    n_chunks = tm // RWKV_CHUNK
    full = lambda b, i: (0, 0)
    consts = (taps, w0, w2, a0, a2, k_k, k_a, r_k, g2)
    stream = jax.ShapeDtypeStruct((bsz, N_HEADS, n_tok, HEAD_DIM), BF16)
    stream_spec = pl.BlockSpec((1, N_HEADS, tm, HEAD_DIM), lambda b, i: (b, 0, i, 0))
    decay = jax.ShapeDtypeStruct((bsz, n_tiles, N_HEADS * n_chunks, HEAD_DIM), F32)
    decay_spec = pl.BlockSpec((1, 1, N_HEADS * n_chunks, HEAD_DIM), lambda b, i: (b, i, 0, 0))
    tokens = jax.ShapeDtypeStruct((bsz, n_tok, GROUP_WIDTH), F32)
    tokens_spec = pl.BlockSpec((1, tm, GROUP_WIDTH), lambda b, i: (b, i, 0))
    return pl.pallas_call(
        _rwkv_prep_kernel,
        grid=(bsz, n_tiles),
        in_specs=[pl.BlockSpec((1, tm, width), lambda b, i: (b, i, 0)),
                  pl.BlockSpec((1, 1, 2, width), lambda b, i: (b, i, 0, 0))]
                 + [pl.BlockSpec(t.shape, full) for t in consts],
        out_specs=[stream_spec] * 13 + [decay_spec] * 2 + [tokens_spec] * 2,
        out_shape=[stream] * 13 + [decay] * 2 + [tokens] * 2,
        compiler_params=_params("parallel", "parallel"),
        name="rwkv_prepare",
    )(rw, halo, *consts)


def _bmm(a, b):
    return lax.dot_general(a.astype(BF16), b.astype(BF16), (((2,), (1,)), ((0,), (0,))),
                           preferred_element_type=F32)


def _bmm_nt(a, b):
    return lax.dot_general(a.astype(BF16), b.astype(BF16), (((2,), (2,)), ((0,), (0,))),
                           preferred_element_type=F32)


def _bmm_tn(a, b):
    return lax.dot_general(a.astype(BF16), b.astype(BF16), (((1,), (1,)), ((0,), (0,))),
                           preferred_element_type=F32)


def _rwkv_chunk_kernel(rt_ref, at_ref, bt_ref, kt_ref, be_ref, ke_ref, v_ref, gt_ref, y_ref, s_ref,
                       *, reverse):
    n_heads, tb, n = v_ref.shape[1:]
    c = RWKV_CHUNK
    n_chunks = tb // c
    g = n_heads * n_chunks

    @pl.when(pl.program_id(1) == 0)
    def _():
        s_ref[...] = jnp.zeros_like(s_ref)

    load = lambda ref: ref[0].reshape(g, c, n)
    rt, at, bt, kt, be, ke, v = (load(ref) for ref in (rt_ref, at_ref, bt_ref, kt_ref, be_ref, ke_ref, v_ref))
    row = lax.broadcasted_iota(jnp.int32, (c, c), 0)
    col = lax.broadcasted_iota(jnp.int32, (c, c), 1)
    if reverse:
        row, col = col, row
    strict = row > col
    incl = row >= col
    a_ab = jnp.where(strict, _bmm_nt(at, bt), 0.0)
    a_ak = jnp.where(strict, _bmm_nt(at, kt), 0.0)
    a_rb = jnp.where(incl, _bmm_nt(rt, bt), 0.0)
    a_rk = jnp.where(incl, _bmm_nt(rt, kt), 0.0)
    av = _bmm(a_ak, v)
    n1 = jnp.where((row >> 3) == (col >> 3), a_ab, 0.0)
    n2 = _bmm(n1, n1)
    n4 = _bmm(n2, n2)
    p = (row == col).astype(F32) + n1 + n2 + _bmm(n1, n2)
    inv = p + _bmm(p, n4)
    for level in range(3, c.bit_length() - 1):
        rb, cb = row >> level, col >> level
        off = jnp.where((rb == cb + 1) & ((rb & 1) == 1), a_ab, 0.0)
        inv = inv + _bmm(_bmm(inv, off), inv)
    w = _bmm(inv, at)
    u0 = _bmm(inv, av)
    q_eff = rt.astype(F32) + _bmm(a_rb, w)
    y0 = _bmm(a_rb, u0) + _bmm(a_rk, v)

    per_chunk = lambda t: t.reshape((n_heads, n_chunks) + t.shape[1:])
    q_eff, w, u0, y0, be, ke, v = (per_chunk(t) for t in (q_eff, w, u0, y0, be, ke, v))
    g_tot = gt_ref[0, 0].reshape(n_heads, n_chunks, 1, n)
    state = s_ref[...]
    for i in (range(n_chunks - 1, -1, -1) if reverse else range(n_chunks)):
        y_ref[0, :, i * c:(i + 1) * c, :] = _bmm_nt(q_eff[:, i], state) + y0[:, i]
        u = _bmm_nt(w[:, i], state) + u0[:, i]
        state = g_tot[:, i] * state + _bmm_tn(u, be[:, i]) + _bmm_tn(v[:, i], ke[:, i])
    s_ref[...] = state


def rwkv_chunk_scan(streams, v, g_tot, ctx_blocks, reverse):
    bsz, n_heads, n_tok, n = v.shape
    tb = TOKEN_TILE
    n_blocks = n_tok // tb

    def block(i):
        if not reverse:
            return i
        return jnp.where(i < ctx_blocks, ctx_blocks - 1 - i, n_blocks - 1 + ctx_blocks - i)

    spec = pl.BlockSpec((1, n_heads, tb, n), lambda b, i: (b, 0, block(i), 0))
    return pl.pallas_call(
        functools.partial(_rwkv_chunk_kernel, reverse=reverse),
        grid=(bsz, n_blocks),
        in_specs=[spec] * 7 + [pl.BlockSpec((1, 1) + g_tot.shape[2:], lambda b, i: (b, block(i), 0, 0))],
        out_specs=spec,
        out_shape=jax.ShapeDtypeStruct((bsz, n_heads, n_tok, n), F32),
        scratch_shapes=[pltpu.VMEM((n_heads, n, n), F32)],
        compiler_params=_params("parallel", "arbitrary"),
        name="rwkv_chunk_scan_bwd" if reverse else "rwkv_chunk_scan_fwd",
    )(*streams, v, g_tot)


def _rwkv_post_kernel(yf_ref, yb_ref, bonus_ref, gate_ref, lw_ref, lb_ref, o_ref):
    for h in range(N_HEADS):
        sl = slice(h * HEAD_DIM, (h + 1) * HEAD_DIM)
        y = yf_ref[0, h] + yb_ref[0, h]
        mu = jnp.mean(y, axis=-1, keepdims=True)
        var = jnp.mean(jnp.square(y - mu), axis=-1, keepdims=True)
        yn = (y - mu) * lax.rsqrt(var + GN_EPS)
        yn = yn * lw_ref[:, sl] + lb_ref[:, sl] + bonus_ref[0, :, sl]
        o_ref[0, :, sl] = (yn * gate_ref[0, :, sl]).astype(o_ref.dtype)


def rwkv_finish(y_f, y_b, bonus, gate, lnx_w, lnx_b):
    bsz, n_heads, n_tok, n = y_f.shape
    tm = TOKEN_TILE
    heads = pl.BlockSpec((1, n_heads, tm, n), lambda b, i: (b, 0, i, 0))
    tokens = pl.BlockSpec((1, tm, n_heads * n), lambda b, i: (b, i, 0))
    vec = pl.BlockSpec((1, n_heads * n), lambda b, i: (0, 0))
    return pl.pallas_call(
        _rwkv_post_kernel,
        grid=(bsz, n_tok // tm),
        in_specs=[heads, heads, tokens, tokens, vec, vec],
        out_specs=tokens,
        out_shape=jax.ShapeDtypeStruct((bsz, n_tok, n_heads * n), BF16),
        compiler_params=_params("parallel", "parallel"),
        name="rwkv_finish",
    )(y_f, y_b, bonus, gate, lnx_w, lnx_b)


def _rope_tables(n_ctx, n_lat, rot_dim, n_rep):
    t = jnp.arange(n_lat, dtype=jnp.int32)
    rows = (t // GRID_W).astype(F32)
    cols = (t % GRID_W).astype(F32)
    per_axis = rot_dim // 2
    inv_freq = ROPE_THETA ** (-jnp.arange(0, per_axis, 2, dtype=F32) / per_axis)
    ang = jnp.concatenate([rows[:, None] * inv_freq, cols[:, None] * inv_freq], axis=-1)
    ang = jnp.concatenate([jnp.zeros((n_ctx, ang.shape[1]), F32), ang], axis=0)
    sign = jnp.tile(jnp.array([-1.0, 1.0], F32), rot_dim // 2)
    cos = jnp.repeat(jnp.cos(ang), 2, axis=-1)
    sin = jnp.repeat(jnp.sin(ang), 2, axis=-1) * sign
    return jnp.tile(cos, (1, n_rep)), jnp.tile(sin, (1, n_rep))


def _pairs(t):
    return t.reshape((t.shape[0], t.shape[1] // 2, 2) + t.shape[2:])


def _block_diag(m0, m1):
    z01 = jnp.zeros((m0.shape[0], m1.shape[1]), m0.dtype)
    z10 = jnp.zeros((m1.shape[0], m0.shape[1]), m0.dtype)
    return jnp.concatenate([jnp.concatenate([m0, z01], axis=1), jnp.concatenate([z10, m1], axis=1)], axis=0)


def _shift_halo(rw, ctx_tiles):
    tm = TOKEN_TILE
    zero = jnp.zeros_like(rw[:, :1])
    prev = jnp.concatenate([zero, rw[:, tm - 1::tm][:, :-1]], axis=1)
    nxt = jnp.concatenate([rw[:, ::tm][:, 1:], zero], axis=1)
    tile = jnp.arange(prev.shape[1])[None, :, None]
    prev = jnp.where(tile == ctx_tiles, 0.0, prev)
    nxt = jnp.where(tile == ctx_tiles - 1, 0.0, nxt)
    return jnp.stack([prev, nxt], axis=2)


def _rwkv_mixer(rw, ctx_tiles, shift_taps, w0, w2, a0, a2, k_k, k_a, r_k, g2, lnx_w, lnx_b):
    both = lambda t: t.reshape(1, -1)
    outs = rwkv_prepare(rw, _shift_halo(rw, ctx_tiles), shift_taps, both(w0),
                        _block_diag(w2[0], w2[1]).astype(BF16), both(a0),
                        _block_diag(a2[0], a2[1]).astype(BF16), both(k_k), both(k_a), both(r_k),
                        g2.astype(BF16))
    v, bonus, gate = outs[12], outs[15], outs[16]
    y_f = rwkv_chunk_scan(outs[0:6], v, outs[13], ctx_tiles, False)
    y_b = rwkv_chunk_scan(outs[6:12], v, outs[14], ctx_tiles, True)
    return rwkv_finish(y_f, y_b, bonus, gate, lnx_w[None], lnx_b[None])


def _dense_branch(q, k, v, n_ctx, want_ctx):
    n_tok = q.shape[3]
    ctx_tiles = n_ctx // ATTN_Q_TILE
    o_lat = attention(q, k, v, ctx_tiles, (n_tok - n_ctx) // ATTN_Q_TILE, n_tok, HEAD_DIM)
    o_ctx = attention(q, k, v, 0, ctx_tiles, n_ctx, HEAD_DIM) if want_ctx else None
    return o_lat, o_ctx


def kernel(x, c, ctx, c_ctx, w_mod, b_mod, norm1_g, norm2_g, w_in, rwkv_shift, na_rpb, gqa_q_norm, gqa_k_norm, mla_q_norm, mla_kv_norm, mla_w_uq, mla_w_ukv, rwkv_w0, rwkv_w2, rwkv_a0, rwkv_a2, rwkv_k_k, rwkv_k_a, rwkv_r_k, rwkv_g2, rwkv_lnx_w, rwkv_lnx_b, w_out, w_fc1, w_fc2, final_norm_g):
    bsz, n_lat, d_model = x.shape
    n_ctx = ctx.shape[1]
    depth = w_mod.shape[0]
    rows = n_lat // GRID_W
    ctx_tiles = n_ctx // TOKEN_TILE
    assert n_ctx % TOKEN_TILE == 0 and n_lat % TOKEN_TILE == 0 and rows % NA_Q_ROWS == 0

    tables = (_rope_tables(n_ctx, n_lat, HEAD_DIM, N_HEADS) + _rope_tables(n_ctx, n_lat, MLA_ROPE_DIM, N_HEADS))

    cond = jnp.concatenate([jax.nn.silu(c), jax.nn.silu(c_ctx)[None]], axis=0)
    n_cond = -(-cond.shape[0] // 8) * 8
    cond = jnp.pad(cond, ((0, n_cond - cond.shape[0]), (0, 0)))
    mod_all = modulation(cond, w_mod, b_mod[:, None, :])

    x_all = jnp.concatenate([ctx, x], axis=1)
    bounds = np.cumsum([0, NA_COLS, GQA_COLS, MLA_COLS, RWKV_COLS])
    mla_pad = -MLA_COLS % 128

    for layer in range(depth):
        want_ctx = layer < depth - 1
        mod = mod_all[layer].reshape(n_cond, 6, d_model)
        mod = jnp.stack([jnp.broadcast_to(mod[bsz], (bsz, 6, d_model)), mod[:bsz]], axis=1)

        w_na, w_gqa, w_mla, w_rw = [w_in[layer][:, bounds[i]:bounds[i + 1]].astype(BF16) for i in range(4)]
        w_uq = mla_w_uq[layer].reshape(MLA_Q_RANK, N_HEADS, MLA_QK_DIM)
        w_uq = jnp.concatenate([w_uq[:, :, :HEAD_DIM].reshape(MLA_Q_RANK, -1),
                                w_uq[:, :, HEAD_DIM:].reshape(MLA_Q_RANK, -1)], axis=1)
        w_ukv = mla_w_ukv[layer].reshape(MLA_KV_RANK, N_HEADS, 2 * HEAD_DIM)
        w_ukv = jnp.concatenate([w_ukv[:, :, :HEAD_DIM].reshape(MLA_KV_RANK, -1),
                                 w_ukv[:, :, HEAD_DIM:].reshape(MLA_KV_RANK, -1)], axis=1)
        weights = (w_na, w_gqa, jnp.pad(w_mla, ((0, 0), (0, mla_pad))), w_rw,
                   jnp.tile(gqa_q_norm[layer], N_HEADS)[None], jnp.tile(gqa_k_norm[layer], GQA_KV_HEADS)[None],
                   mla_q_norm[layer][None], mla_kv_norm[layer][None], w_uq.astype(BF16), w_ukv.astype(BF16))
        (a_q, a_k, a_v, b_q, b_k, b_v, c_q, c_k, c_v, rw) = in_projection(
            x_all, mod, norm1_g[layer][None], weights, tables, ctx_tiles)

        a_q, a_k, a_v = _pairs(a_q), _pairs(a_k), _pairs(a_v)
        bias = na_bias_table(na_rpb[layer], rows)
        out_a = neighbourhood_attention(a_q, a_k, a_v, bias.reshape((N_HEADS // 2, 2) + bias.shape[1:]), n_ctx)
        out_ac = attention(a_q, a_k, a_v, 0, ctx_tiles, n_ctx, HEAD_DIM) if want_ctx else None
        mix_b = _dense_branch(_pairs(b_q), b_k[:, :, None], b_v[:, :, None], n_ctx, want_ctx)
        mix_c = _dense_branch(_pairs(c_q), _pairs(c_k), _pairs(c_v), n_ctx, want_ctx)
        mix_d = _rwkv_mixer(rw, ctx_tiles, rwkv_shift[layer], rwkv_w0[layer], rwkv_w2[layer], rwkv_a0[layer],
                            rwkv_a2[layer], rwkv_k_k[layer], rwkv_k_a[layer], rwkv_r_k[layer],
                            rwkv_g2[layer], rwkv_lnx_w[layer], rwkv_lnx_b[layer])

        x_all = out_projection_mlp(x_all, [(out_a, out_ac), mix_b, mix_c], mix_d, mod, norm2_g[layer][None],
                                   final_norm_g[None], w_out[layer].astype(BF16), w_fc1[layer].astype(BF16),
                                   w_fc2[layer].astype(BF16), ctx_tiles, not want_ctx)
    return x_all
```

```python
import functools

import numpy as np
import jax
import jax.numpy as jnp
from jax import lax
from jax.experimental import pallas as pl
from jax.experimental.pallas import tpu as pltpu

F32 = jnp.float32
BF16 = jnp.bfloat16

GRID_W = 64
HEAD_DIM = 64
N_HEADS = 4
GROUP_WIDTH = N_HEADS * HEAD_DIM
GQA_KV_HEADS = 2
NA_WIN_ROWS = 8
NA_WIN_COLS = 16
MLA_Q_RANK = 256
MLA_KV_RANK = 128
MLA_ROPE_DIM = 32
MLA_QK_DIM = HEAD_DIM + MLA_ROPE_DIM
RWKV_DECAY_RANK = 64
RWKV_ICLR_RANK = 64
RWKV_GATE_RANK = 160
NA_COLS = 3 * GROUP_WIDTH
GQA_COLS = (N_HEADS + 2 * GQA_KV_HEADS) * HEAD_DIM
MLA_COLS = MLA_Q_RANK + MLA_KV_RANK + MLA_ROPE_DIM
RWKV_COLS = 3 * GROUP_WIDTH + 2 * RWKV_DECAY_RANK + 2 * RWKV_ICLR_RANK + RWKV_GATE_RANK
ROPE_THETA = 10000.0
RMS_EPS = 1e-6
GN_EPS = 64e-5
NEG_INF = -1e30

TOKEN_TILE = 256
ATTN_Q_TILE = 256
ATTN_KV_CHUNK = 1024
V_EXT_WIDTH = 128
LOG2E = float(np.log2(np.e))
NA_Q_ROWS = ATTN_Q_TILE // GRID_W
NA_KEY_ROWS = NA_Q_ROWS + NA_WIN_ROWS - 1
RWKV_CHUNK = 64
RWKV_ROWS_PER_STEP = 4
VMEM_LIMIT = 56 * 1024 * 1024


def _params(*sem):
    return pltpu.CompilerParams(dimension_semantics=sem, vmem_limit_bytes=VMEM_LIMIT)


def _bdot(a, b):
    return jnp.dot(a.astype(BF16), b.astype(BF16), preferred_element_type=F32)


def _bdot_nt(a, b):
    return lax.dot_general(a.astype(BF16), b.astype(BF16), (((1,), (1,)), ((), ())),
                           preferred_element_type=F32)


def _bdot_tn(a, b):
    return lax.dot_general(a.astype(BF16), b.astype(BF16), (((0,), (0,)), ((), ())),
                           preferred_element_type=F32)


def _matmul_kernel(a_ref, b_ref, o_ref):
    o_ref[...] = _bdot(a_ref[...], b_ref[...]).astype(o_ref.dtype)


def matmul(a, b, out_dtype=F32, tm=1024):
    m, k = a.shape
    n = b.shape[1]
    tm = min(tm, m)
    while m % tm:
        tm //= 2
    return pl.pallas_call(
        _matmul_kernel,
        grid=(m // tm,),
        in_specs=[pl.BlockSpec((tm, k), lambda i: (i, 0)),
                  pl.BlockSpec((k, n), lambda i: (0, 0))],
        out_specs=pl.BlockSpec((tm, n), lambda i: (i, 0)),
        out_shape=jax.ShapeDtypeStruct((m, n), out_dtype),
        compiler_params=_params("parallel"),
        name="matmul",
    )(a, b)


def _mod_kernel(a_ref, w_ref, b_ref, o_ref):
    o_ref[0] = _bdot(a_ref[...], w_ref[0]) + b_ref[0]


def modulation(cond, w_mod, b_mod, tn=1536):
    depth, d, n = w_mod.shape
    r = cond.shape[0]
    return pl.pallas_call(
        _mod_kernel,
        grid=(depth, n // tn),
        in_specs=[pl.BlockSpec((r, d), lambda l, j: (0, 0)),
                  pl.BlockSpec((1, d, tn), lambda l, j: (l, 0, j)),
                  pl.BlockSpec((1, 1, tn), lambda l, j: (l, 0, j))],
        out_specs=pl.BlockSpec((1, r, tn), lambda l, j: (l, 0, j)),
        out_shape=jax.ShapeDtypeStruct((depth, r, n), F32),
        compiler_params=_params("parallel", "parallel"),
        name="modulation",
    )(cond, w_mod, b_mod)


def _rms(x, g):
    return x * lax.rsqrt(jnp.mean(x * x, axis=-1, keepdims=True) + RMS_EPS) * g


def _split3(x):
    hi = x.astype(BF16)
    rest = x - hi.astype(F32)
    mid = rest.astype(BF16)
    return hi, mid, (rest - mid.astype(F32)).astype(BF16)


def _select_rows(sel, x):
    sel = sel.astype(BF16)
    hi, mid, lo = _split3(x)
    dot = lambda part: jnp.dot(sel, part, preferred_element_type=F32)
    return dot(lo) + dot(mid) + dot(hi)


def _select_cols(x, sel):
    sel = sel.astype(BF16)
    hi, mid, lo = _split3(x)
    dot = lambda part: jnp.dot(part, sel, preferred_element_type=F32)
    return dot(lo) + dot(mid) + dot(hi)


def _pair_swap(x):
    axis = x.ndim - 1
    lane = lax.broadcasted_iota(jnp.int32, x.shape, axis)
    return jnp.where((lane & 1) == 0, pltpu.roll(x, x.shape[axis] - 1, axis=axis), pltpu.roll(x, 1, axis=axis))


def _rotate(x, cos, sin):
    return x * cos + _pair_swap(x) * sin


def _in_proj_kernel(x_ref, mod_ref, g_ref, wna_ref, wgqa_ref, wmla_ref, wrw_ref, gq_ref, gk_ref,
                    mq_ref, mkv_ref, wuq_ref, wukv_ref, cosh_ref, sinh_ref, cosm_ref, sinm_ref,
                    naq_ref, nak_ref, nav_ref, bq_ref, bk_ref, bv_ref, cq_ref, ck_ref, cv_ref, rw_ref):
    hd, gw = HEAD_DIM, GROUP_WIDTH
    mod = mod_ref[0, 0]
    hb = (_rms(x_ref[0], g_ref[...]) * (1.0 + mod[1:2]) + mod[0:1]).astype(BF16)
    tm = hb.shape[0]
    ones_col = (lax.broadcasted_iota(jnp.int32, (tm, V_EXT_WIDTH - hd), 1) == 0).astype(F32)
    qk_scale = LOG2E * hd ** -0.5

    def put_heads(ref, val, n_heads, width):
        for h in range(n_heads):
            ref[0, h] = val[:, h * width:(h + 1) * width].astype(ref.dtype)

    def put_values(ref, val, n_heads):
        for h in range(n_heads):
            ref[0, h] = jnp.concatenate([val[:, h * hd:(h + 1) * hd], ones_col], axis=-1).astype(ref.dtype)

    na = jnp.dot(hb, wna_ref[...], preferred_element_type=F32)
    put_heads(naq_ref, na[:, :gw] * qk_scale, N_HEADS, hd)
    put_heads(nak_ref, na[:, gw:2 * gw], N_HEADS, hd)
    put_values(nav_ref, na[:, 2 * gw:], N_HEADS)

    gqa = jnp.dot(hb, wgqa_ref[...], preferred_element_type=F32)
    row = lax.broadcasted_iota(jnp.int32, (gw, gw), 0)
    col = lax.broadcasted_iota(jnp.int32, (gw, gw), 1)
    shift = hd.bit_length() - 1
    head_sum = ((row >> shift) == (col >> shift)).astype(F32)
    cos_h, sin_h = cosh_ref[...], sinh_ref[...]
    kw = GQA_KV_HEADS * hd
    q = gqa[:, :gw]
    q = q * lax.rsqrt(_select_cols(q * q, head_sum) * (1.0 / hd) + RMS_EPS) * gq_ref[...]
    put_heads(bq_ref, _rotate(q, cos_h, sin_h) * qk_scale, N_HEADS, hd)
    k = gqa[:, gw:gw + kw]
    k = k * lax.rsqrt(_select_cols(k * k, head_sum[:kw, :kw]) * (1.0 / hd) + RMS_EPS) * gk_ref[...]
    put_heads(bk_ref, _rotate(k, cos_h[:, :kw], sin_h[:, :kw]), GQA_KV_HEADS, hd)
    put_values(bv_ref, gqa[:, gw + kw:], GQA_KV_HEADS)

    mla = jnp.dot(hb, wmla_ref[...], preferred_element_type=F32)
    cos_m, sin_m = cosm_ref[...], sinm_ref[...]
    mla_scale = LOG2E * MLA_QK_DIM ** -0.5
    uq = jnp.dot(_rms(mla[:, :MLA_Q_RANK], mq_ref[...]).astype(BF16), wuq_ref[...], preferred_element_type=F32)
    q_nope = uq[:, :gw] * mla_scale
    q_rope = _rotate(uq[:, gw:], cos_m, sin_m) * mla_scale
    lo = MLA_Q_RANK + MLA_KV_RANK
    ukv = jnp.dot(_rms(mla[:, MLA_Q_RANK:lo], mkv_ref[...]).astype(BF16), wukv_ref[...],
                  preferred_element_type=F32)
    k_rope = _rotate(mla[:, lo:], cos_m, sin_m)[:, :MLA_ROPE_DIM]
    rd = MLA_ROPE_DIM
    for h in range(N_HEADS):
        cq_ref[0, h] = jnp.concatenate([q_nope[:, h * hd:(h + 1) * hd], q_rope[:, h * rd:(h + 1) * rd]],
                                       axis=-1).astype(cq_ref.dtype)
        ck_ref[0, h] = jnp.concatenate([ukv[:, h * hd:(h + 1) * hd], k_rope], axis=-1).astype(ck_ref.dtype)
    put_values(cv_ref, ukv[:, gw:], N_HEADS)

    rw_ref[0] = jnp.dot(hb, wrw_ref[...], preferred_element_type=F32)


def in_projection(x_all, mod, g, weights, tables, ctx_tiles):
    bsz, n_tok, d = x_all.shape
    tm = TOKEN_TILE
    tok = lambda b, i: (b, i, 0)
    full = lambda b, i: (0, 0)
    heads = lambda n, w: (jax.ShapeDtypeStruct((bsz, n, n_tok, w), BF16),
                          pl.BlockSpec((1, n, tm, w), lambda b, i: (b, 0, i, 0)))
    outs = [heads(N_HEADS, HEAD_DIM), heads(N_HEADS, HEAD_DIM), heads(N_HEADS, V_EXT_WIDTH),
            heads(N_HEADS, HEAD_DIM), heads(GQA_KV_HEADS, HEAD_DIM), heads(GQA_KV_HEADS, V_EXT_WIDTH),
            heads(N_HEADS, MLA_QK_DIM), heads(N_HEADS, MLA_QK_DIM), heads(N_HEADS, V_EXT_WIDTH),
            (jax.ShapeDtypeStruct((bsz, n_tok, RWKV_COLS), F32), pl.BlockSpec((1, tm, RWKV_COLS), tok))]
    return pl.pallas_call(
        _in_proj_kernel,
        grid=(bsz, n_tok // tm),
        in_specs=[pl.BlockSpec((1, tm, d), tok),
                  pl.BlockSpec((1, 1, 6, d), lambda b, i: (b, jnp.where(i < ctx_tiles, 0, 1), 0, 0)),
                  pl.BlockSpec((1, d), full)]
                 + [pl.BlockSpec(w.shape, full) for w in weights]
                 + [pl.BlockSpec((tm, t.shape[1]), lambda b, i: (i, 0)) for t in tables],
        out_specs=[o[1] for o in outs],
        out_shape=[o[0] for o in outs],
        compiler_params=_params("parallel", "parallel"),
        name="in_projection",
    )(x_all, mod, g, *weights, *tables)


def _out_mlp_kernel(*refs, hidden_tile, final_norm, ctx_tiles, n_branches):
    x_ref = refs[0]
    per_branch = 2 if ctx_tiles else 1
    branch_refs = refs[1:1 + per_branch * n_branches]
    mixd_ref, mod_ref, g2_ref, gf_ref, wout_ref, w1_ref, w2_ref, o_ref = refs[1 + per_branch * n_branches:]
    parts = []
    for j in range(n_branches):
        lat = branch_refs[per_branch * j][0]
        if ctx_tiles:
            lat = jnp.where(pl.program_id(1) < ctx_tiles, branch_refs[per_branch * j + 1][0], lat)
        parts.append(lat)
    parts.append(mixd_ref[0])
    mod = mod_ref[0, 0]
    proj = jnp.zeros(x_ref.shape[1:], F32)
    lo = 0
    for part in parts:
        proj = proj + jnp.dot(part, wout_ref[lo:lo + part.shape[1], :], preferred_element_type=F32)
        lo += part.shape[1]
    x1 = x_ref[0] + mod[2:3] * proj
    hb = (_rms(x1, g2_ref[...]) * (1.0 + mod[4:5]) + mod[3:4]).astype(BF16)
    acc = jnp.zeros_like(x1)
    for j in range(w1_ref.shape[1] // hidden_tile):
        cols = pl.ds(j * hidden_tile, hidden_tile)
        u = jnp.maximum(jnp.dot(hb, w1_ref[:, cols], preferred_element_type=F32), 0.0)
        acc = acc + jnp.dot((u * u).astype(BF16), w2_ref[cols, :], preferred_element_type=F32)
    x2 = x1 + mod[5:6] * acc
    if final_norm:
        x2 = _rms(x2, gf_ref[...])
    o_ref[0] = x2


def out_projection_mlp(x_all, branches, mix_d, mod, g2, g_final, w_out, w_fc1, w_fc2, ctx_tiles, last):
    bsz, n_tok, d = x_all.shape
    tm = TOKEN_TILE
    n_tiles = n_tok // tm
    skip = ctx_tiles if last else 0
    tok = lambda b, i: (b, i + skip, 0)
    full = lambda b, i: (0, 0)
    lat_map = lambda b, i: (b, jnp.maximum(i + skip - ctx_tiles, 0), 0)
    ctx_map = lambda b, i: (b, jnp.minimum(i, ctx_tiles - 1), 0)
    branch_args, branch_specs = [], []
    for lat, ctx_rows in branches:
        branch_args.append(lat)
        branch_specs.append(pl.BlockSpec((1, tm, lat.shape[2]), lat_map))
        if not last:
            branch_args.append(ctx_rows)
            branch_specs.append(pl.BlockSpec((1, tm, ctx_rows.shape[2]), ctx_map))
    kern = functools.partial(_out_mlp_kernel, hidden_tile=1024, final_norm=last,
                             ctx_tiles=0 if last else ctx_tiles, n_branches=len(branches))
    return pl.pallas_call(
        kern,
        grid=(bsz, n_tiles - skip),
        in_specs=[pl.BlockSpec((1, tm, d), tok)] + branch_specs
                 + [pl.BlockSpec((1, tm, mix_d.shape[2]), tok),
                    pl.BlockSpec((1, 1, 6, d), lambda b, i: (b, jnp.where(i + skip < ctx_tiles, 0, 1), 0, 0)),
                    pl.BlockSpec((1, d), full),
                    pl.BlockSpec((1, d), full),
                    pl.BlockSpec(w_out.shape, full, pipeline_mode=pl.Buffered(1)),
                    pl.BlockSpec(w_fc1.shape, full, pipeline_mode=pl.Buffered(1)),
                    pl.BlockSpec(w_fc2.shape, full, pipeline_mode=pl.Buffered(1))],
        out_specs=pl.BlockSpec((1, tm, d), lambda b, i: (b, i, 0)),
        out_shape=jax.ShapeDtypeStruct((bsz, n_tok - skip * tm, d), F32),
        compiler_params=_params("parallel", "parallel"),
        name="out_projection_mlp",
    )(x_all, *branch_args, mix_d, mod, g2, g_final, w_out, w_fc1, w_fc2)


def _key_chunks(n_keys):
    first = min(n_keys, ATTN_Q_TILE)
    chunks = [(0, first)]
    while chunks[-1][0] + chunks[-1][1] < n_keys:
        start = chunks[-1][0] + chunks[-1][1]
        chunks.append((start, min(ATTN_KV_CHUNK, n_keys - start)))
    return chunks


def _online_softmax(q, k_at, v_at, chunks):
    rows = q.shape[0]
    scores = lambda c: _bdot_nt(q, k_at(*c))
    m = jnp.full((rows, 1), NEG_INF, F32)
    acc = jnp.zeros((rows, V_EXT_WIDTH), F32)
    s_next = scores(chunks[0])
    for j, chunk in enumerate(chunks):
        s = s_next
        if j + 1 < len(chunks):
            s_next = scores(chunks[j + 1])
        m_new = jnp.maximum(m, jnp.max(s, axis=-1, keepdims=True))
        p = jnp.exp2(s - m_new)
        acc = acc * jnp.exp2(m - m_new) + jnp.dot(p.astype(BF16), v_at(*chunk), preferred_element_type=F32)
        m = m_new
    return acc


def _attention_kernel(q_ref, k_ref, v_ref, o_ref):
    n_q, tq, dk = q_ref.shape[2], q_ref.shape[3], q_ref.shape[4]
    dv = o_ref.shape[-1] // n_q
    chunks = _key_chunks(k_ref.shape[3])
    if k_ref.shape[2] == 1:
        groups = [(q_ref[0, 0].reshape(n_q * tq, dk), 0)]
    else:
        groups = [(q_ref[0, 0, g], g) for g in range(n_q)]
    outs = []
    for q, g in groups:
        acc = _online_softmax(q, lambda s, n: k_ref[0, 0, g, pl.ds(s, n), :],
                              lambda s, n: v_ref[0, 0, g, pl.ds(s, n), :], chunks)
        o = acc[:, :dv] / acc[:, dv:dv + 1]
        outs.extend(o[i * tq:(i + 1) * tq] for i in range(o.shape[0] // tq))
    o_ref[0] = jnp.concatenate(outs, axis=-1).astype(o_ref.dtype)


def attention(q, k, v, q_tile0, n_q_tiles, n_keys, dv):
    bsz, n_pairs, n_q, _, dk = q.shape
    n_kv = k.shape[2]
    tq = ATTN_Q_TILE
    return pl.pallas_call(
        _attention_kernel,
        grid=(bsz, n_pairs, n_q_tiles),
        in_specs=[pl.BlockSpec((1, 1, n_q, tq, dk), lambda b, h, i: (b, h, 0, i + q_tile0, 0)),
                  pl.BlockSpec((1, 1, n_kv, n_keys, dk), lambda b, h, i: (b, h, 0, 0, 0)),
                  pl.BlockSpec((1, 1, n_kv, n_keys, v.shape[-1]), lambda b, h, i: (b, h, 0, 0, 0))],
        out_specs=pl.BlockSpec((1, tq, n_q * dv), lambda b, h, i: (b, i, h)),
        out_shape=jax.ShapeDtypeStruct((bsz, n_q_tiles * tq, n_pairs * n_q * dv), BF16),
        compiler_params=_params("parallel", "parallel", "parallel"),
        name="attention",
    )(q, k, v)


def _na_kernel(q_ref, k_ref, v_ref, bias_ref, o_ref, *, n_ctx, n_row_tiles):
    i = pl.program_id(2)
    n_win = NA_KEY_ROWS * GRID_W
    max_row0 = n_row_tiles * NA_Q_ROWS - NA_KEY_ROWS
    row0 = jnp.clip(i * NA_Q_ROWS - NA_WIN_ROWS // 2, 0, max_row0)
    start = pl.multiple_of(n_ctx + row0 * GRID_W, GRID_W)
    dv = o_ref.shape[-1] // q_ref.shape[2]
    outs = []
    for g in range(q_ref.shape[2]):
        q = q_ref[0, 0, g]
        s_win = _bdot_nt(q, k_ref[0, 0, g, pl.ds(start, n_win), :]) + bias_ref[0, g, 0]
        s_ctx = _bdot_nt(q, k_ref[0, 0, g, pl.ds(0, n_ctx), :])
        m = jnp.maximum(jnp.max(s_win, axis=-1, keepdims=True), jnp.max(s_ctx, axis=-1, keepdims=True))
        p_win = jnp.exp2(s_win - m).astype(BF16)
        p_ctx = jnp.exp2(s_ctx - m).astype(BF16)
        o = (jnp.dot(p_win, v_ref[0, 0, g, pl.ds(start, n_win), :], preferred_element_type=F32)
             + jnp.dot(p_ctx, v_ref[0, 0, g, pl.ds(0, n_ctx), :], preferred_element_type=F32))
        outs.append(o[:, :dv] / o[:, dv:dv + 1])
    o_ref[0] = jnp.concatenate(outs, axis=-1).astype(o_ref.dtype)


def na_bias_table(rpb, rows):
    n_tiles = rows // NA_Q_ROWS
    n_heads, n_dr, n_dc = rpb.shape
    qc = np.arange(GRID_W)[:, None]
    kc = np.arange(GRID_W)[None, :]
    dc = np.clip(kc - qc + NA_WIN_COLS - 1, 0, n_dc - 1)
    onehot = jnp.asarray(dc.reshape(-1)[None, :] == np.arange(n_dc)[:, None], F32)
    by_col = jnp.dot(rpb.reshape(-1, n_dc).astype(F32), onehot, precision=lax.Precision.HIGHEST)
    pad = NA_KEY_ROWS
    by_col = jnp.pad(by_col.reshape(n_heads, n_dr, GRID_W, GRID_W), ((0, 0), (pad, pad), (0, 0), (0, 0)))
    c_start = np.clip(qc - NA_WIN_COLS // 2, 0, GRID_W - NA_WIN_COLS)
    col_ok = (kc >= c_start) & (kc < c_start + NA_WIN_COLS)
    tables = []
    for tile in (0, 1, n_tiles - 1):
        row0 = int(np.clip(tile * NA_Q_ROWS - NA_WIN_ROWS // 2, 0, rows - NA_KEY_ROWS))
        per_row = []
        for a in range(NA_Q_ROWS):
            qr = tile * NA_Q_ROWS + a
            dr0 = row0 - qr + NA_WIN_ROWS - 1 + pad
            kr = row0 + np.arange(NA_KEY_ROWS)
            r_start = int(np.clip(qr - NA_WIN_ROWS // 2, 0, rows - NA_WIN_ROWS))
            row_ok = (kr >= r_start) & (kr < r_start + NA_WIN_ROWS)
            valid = row_ok[None, :, None] & col_ok[:, None, :]
            sl = jnp.swapaxes(by_col[:, dr0:dr0 + NA_KEY_ROWS], 1, 2)
            per_row.append(jnp.where(valid[None], sl * LOG2E, NEG_INF))
        bias = jnp.stack(per_row, axis=1)
        tables.append(bias.reshape(n_heads, NA_Q_ROWS * GRID_W, NA_KEY_ROWS * GRID_W))
    return jnp.stack(tables, axis=1)


def neighbourhood_attention(q, k, v, bias, n_ctx):
    bsz, n_pairs, n_q, n_tok, d = q.shape
    tq = ATTN_Q_TILE
    n_tiles = (n_tok - n_ctx) // tq
    ctx_tiles = n_ctx // tq
    kern = functools.partial(_na_kernel, n_ctx=n_ctx, n_row_tiles=n_tiles)

    def bias_map(b, h, i):
        return (h, 0, jnp.where(i == 0, 0, jnp.where(i == n_tiles - 1, 2, 1)), 0, 0)

    whole = lambda b, h, i: (b, h, 0, 0, 0)
    return pl.pallas_call(
        kern,
        grid=(bsz, n_pairs, n_tiles),
        in_specs=[pl.BlockSpec((1, 1, n_q, tq, d), lambda b, h, i: (b, h, 0, i + ctx_tiles, 0)),
                  pl.BlockSpec((1, 1, n_q, n_tok, d), whole),
                  pl.BlockSpec((1, 1, n_q, n_tok, v.shape[-1]), whole),
                  pl.BlockSpec((1, n_q, 1) + bias.shape[3:], bias_map)],
        out_specs=pl.BlockSpec((1, tq, n_q * d), lambda b, h, i: (b, i, h)),
        out_shape=jax.ShapeDtypeStruct((bsz, n_tok - n_ctx, n_pairs * n_q * d), BF16),
        compiler_params=_params("parallel", "parallel", "parallel"),
        name="neighbourhood_attention",
    )(q, k, v, bias)


def _softplus(z):
    return jnp.maximum(z, 0.0) + jnp.log(1.0 + jnp.exp(-jnp.abs(z)))


def _sigmoid(z):
    return 1.0 / (1.0 + jnp.exp(-z))


def _rwkv_prep_kernel(u_ref, halo_ref, taps_ref, w0_ref, w2_ref, a0_ref, a2_ref, kk_ref, ka_ref, rk_ref,
                      g2_ref, *out_refs):
    dir_refs = (out_refs[0:6], out_refs[6:12])
    v_out, gt_refs, bonus_ref, gate_ref = out_refs[12], out_refs[13:15], out_refs[15], out_refs[16]
    tm = u_ref.shape[1]
    c = RWKV_CHUNK
    n_chunks = tm // c
    gw = GROUP_WIDTH
    u = u_ref[0]
    halo = halo_ref[0, 0]
    tok = lax.broadcasted_iota(jnp.int32, u.shape, 0)
    prev = jnp.where(tok == 0, halo[0:1], pltpu.roll(u, 1, axis=0))
    nxt = jnp.where(tok == tm - 1, halo[1:2], pltpu.roll(u, tm - 1, axis=0))
    taps = taps_ref[...]
    s = prev * taps[0:1] + u * taps[1:2] + nxt * taps[2:3]

    r, k, v = s[:, 0:gw], s[:, gw:2 * gw], s[:, 2 * gw:3 * gw]
    lo = 3 * gw
    w_low = jnp.tanh(s[:, lo:lo + 2 * RWKV_DECAY_RANK])
    lo += 2 * RWKV_DECAY_RANK
    a_low = s[:, lo:lo + 2 * RWKV_ICLR_RANK]
    lo += 2 * RWKV_ICLR_RANK
    g_low = _sigmoid(s[:, lo:lo + RWKV_GATE_RANK])
    w_log = -_softplus(-(w0_ref[...] + _bdot(w_low, w2_ref[...]))) - 0.5
    log_decay = -jnp.exp(w_log)
    iclr = _sigmoid(a0_ref[...] + _bdot(a_low, a2_ref[...]))
    gate_ref[0] = _bdot(g_low, g2_ref[...])

    row = lax.broadcasted_iota(jnp.int32, (tm, tm), 0)
    col = lax.broadcasted_iota(jnp.int32, (tm, tm), 1)
    shift = c.bit_length() - 1
    same = (row >> shift) == (col >> shift)
    ones_blk = same.astype(F32)

    def head_major(ref, val):
        for h in range(N_HEADS):
            ref[0, h] = val[:, h * HEAD_DIM:(h + 1) * HEAD_DIM].astype(ref.dtype)

    head_major(v_out, v)
    bonus = jnp.zeros_like(r)
    for d in range(2):
        sl = slice(d * gw, (d + 1) * gw)
        lw = log_decay[:, sl]
        kk = k * kk_ref[:, sl]
        kk = kk * lax.rsqrt(jnp.maximum(_select_cols(kk * kk, ones_blk), 1e-24))
        k_d = k * (1.0 + (iclr[:, sl] - 1.0) * ka_ref[:, sl])
        b_vec = kk * iclr[:, sl]
        tri = (same & ((row >= col) if d == 0 else (row <= col))).astype(F32)
        cum = _select_rows(tri, lw)
        ends = [cum[i * c + (c - 1 if d == 0 else 0)][None] for i in range(n_chunks)]
        total = jnp.concatenate([jnp.broadcast_to(e, (c, gw)) for e in ends], axis=0)
        e_inv = jnp.exp(-cum)
        e_end = jnp.exp(total - cum)
        streams = (r * jnp.exp(cum), -kk * jnp.exp(cum - lw), b_vec * e_inv, k_d * e_inv,
                   b_vec * e_end, k_d * e_end)
        for ref, val in zip(dir_refs[d], streams):
            head_major(ref, val)
        g_tot = jnp.exp(jnp.concatenate(ends, axis=0))
        for h in range(N_HEADS):
            gt_refs[d][0, 0, h * n_chunks:(h + 1) * n_chunks, :] = g_tot[:, h * HEAD_DIM:(h + 1) * HEAD_DIM]
        bonus = bonus + r * k_d * rk_ref[:, sl]
    bonus_ref[0] = _select_cols(bonus, ones_blk) * v


def rwkv_prepare(rw, halo, taps, w0, w2, a0, a2, k_k, k_a, r_k, g2):
    bsz, n_tok, width = rw.shape
    tm = TOKEN_TILE
    n_tiles = n_tok // tm
    n_chunks = tm // RWKV_CHUNK
    full = lambda b, i: (0, 0)
    consts = (taps, w0, w2, a0, a2, k_k, k_a, r_k, g2)
    stream = jax.ShapeDtypeStruct((bsz, N_HEADS, n_tok, HEAD_DIM), BF16)
    stream_spec = pl.BlockSpec((1, N_HEADS, tm, HEAD_DIM), lambda b, i: (b, 0, i, 0))
    decay = jax.ShapeDtypeStruct((bsz, n_tiles, N_HEADS * n_chunks, HEAD_DIM), F32)
    decay_spec = pl.BlockSpec((1, 1, N_HEADS * n_chunks, HEAD_DIM), lambda b, i: (b, i, 0, 0))
    tokens = jax.ShapeDtypeStruct((bsz, n_tok, GROUP_WIDTH), F32)
    tokens_spec = pl.BlockSpec((1, tm, GROUP_WIDTH), lambda b, i: (b, i, 0))
    return pl.pallas_call(
        _rwkv_prep_kernel,
        grid=(bsz, n_tiles),
        in_specs=[pl.BlockSpec((1, tm, width), lambda b, i: (b, i, 0)),
                  pl.BlockSpec((1, 1, 2, width), lambda b, i: (b, i, 0, 0))]
                 + [pl.BlockSpec(t.shape, full) for t in consts],
        out_specs=[stream_spec] * 13 + [decay_spec] * 2 + [tokens_spec] * 2,
        out_shape=[stream] * 13 + [decay] * 2 + [tokens] * 2,
        compiler_params=_params("parallel", "parallel"),
        name="rwkv_prepare",
    )(rw, halo, *consts)


def _bmm(a, b):
    return lax.dot_general(a.astype(BF16), b.astype(BF16), (((2,), (1,)), ((0,), (0,))),
                           preferred_element_type=F32)


def _bmm_nt(a, b):
    return lax.dot_general(a.astype(BF16), b.astype(BF16), (((2,), (2,)), ((0,), (0,))),
                           preferred_element_type=F32)


def _bmm_tn(a, b):
    return lax.dot_general(a.astype(BF16), b.astype(BF16), (((1,), (1,)), ((0,), (0,))),
                           preferred_element_type=F32)


def _rwkv_chunk_kernel(rt_ref, at_ref, bt_ref, kt_ref, be_ref, ke_ref, v_ref, gt_ref, y_ref, s_ref,
                       *, reverse):
    n_rows, n_heads, tb, n = v_ref.shape
    c = RWKV_CHUNK
    n_chunks = tb // c
    n_seq = n_rows * n_heads
    g = n_seq * n_chunks

    @pl.when(pl.program_id(1) == 0)
    def _():
        s_ref[...] = jnp.zeros_like(s_ref)

    load = lambda ref: ref[...].reshape(g, c, n)
    rt, at, bt, kt, be, ke, v = (load(ref) for ref in (rt_ref, at_ref, bt_ref, kt_ref, be_ref, ke_ref, v_ref))
    row = lax.broadcasted_iota(jnp.int32, (c, c), 0)
    col = lax.broadcasted_iota(jnp.int32, (c, c), 1)
    if reverse:
        row, col = col, row
    row2 = lax.broadcasted_iota(jnp.int32, (c, 2 * c), 0)
    col2 = lax.broadcasted_iota(jnp.int32, (c, 2 * c), 1) & (c - 1)
    if reverse:
        row2, col2 = col2, row2
    strict = row2 > col2
    incl = row2 >= col2
    cat = lambda a, b: jnp.concatenate([a, b], axis=1)
    zeros = jnp.zeros_like(v)
    prod = _bmm_nt(cat(at, rt), cat(bt, kt))
    top = jnp.where(strict, prod[:, :c], 0.0)
    bot = jnp.where(incl, prod[:, c:], 0.0)
    a_ab = top[:, :, :c]
    av = _bmm(top, cat(zeros, v))
    n1 = jnp.where((row >> 3) == (col >> 3), a_ab, 0.0)
    n2 = _bmm(n1, n1)
    n4 = _bmm(n2, n2)
    p = (row == col).astype(F32) + n1 + n2 + _bmm(n1, n2)
    inv = p + _bmm(p, n4)
    for level in range(3, c.bit_length() - 1):
        rb, cb = row >> level, col >> level
        off = jnp.where((rb == cb + 1) & ((rb & 1) == 1), a_ab, 0.0)
        inv = inv + _bmm(_bmm(inv, off), inv)
    w = _bmm(inv, at)
    u0 = _bmm(inv, av)
    q_eff = rt.astype(F32) + _bmm(bot, cat(w, zeros))
    y0 = _bmm(bot, cat(u0, v))

    per_chunk = lambda t: t.reshape((n_seq, n_chunks) + t.shape[1:])
    qw = per_chunk(cat(q_eff, w))
    u0, y0, v = per_chunk(u0), per_chunk(y0), per_chunk(v)
    bk_end = per_chunk(cat(be, ke))
    g_tot = gt_ref[:, 0].reshape(n_seq, n_chunks, 1, n)
    state = s_ref[...]
    for i in (range(n_chunks - 1, -1, -1) if reverse else range(n_chunks)):
        yu = _bmm_nt(qw[:, i], state)
        y_ref[:, :, i * c:(i + 1) * c, :] = (yu[:, :c] + y0[:, i]).reshape(n_rows, n_heads, c, n)
        state = g_tot[:, i] * state + _bmm_tn(cat(yu[:, c:] + u0[:, i], v[:, i]), bk_end[:, i])
    s_ref[...] = state


def rwkv_chunk_scan(streams, v, g_tot, ctx_blocks, reverse):
    bsz, n_heads, n_tok, n = v.shape
    tb = TOKEN_TILE
    n_blocks = n_tok // tb

    def block(i):
        if not reverse:
            return i
        return jnp.where(i < ctx_blocks, ctx_blocks - 1 - i, n_blocks - 1 + ctx_blocks - i)

    rows = RWKV_ROWS_PER_STEP
    spec = pl.BlockSpec((rows, n_heads, tb, n), lambda b, i: (b, 0, block(i), 0))
    return pl.pallas_call(
        functools.partial(_rwkv_chunk_kernel, reverse=reverse),
        grid=(bsz // rows, n_blocks),
        in_specs=[spec] * 7 + [pl.BlockSpec((rows, 1) + g_tot.shape[2:], lambda b, i: (b, block(i), 0, 0))],
        out_specs=spec,
        out_shape=jax.ShapeDtypeStruct((bsz, n_heads, n_tok, n), F32),
        scratch_shapes=[pltpu.VMEM((rows * n_heads, n, n), F32)],
        compiler_params=_params("parallel", "arbitrary"),
        name="rwkv_chunk_scan_bwd" if reverse else "rwkv_chunk_scan_fwd",
    )(*streams, v, g_tot)


def _rwkv_post_kernel(yf_ref, yb_ref, bonus_ref, gate_ref, lw_ref, lb_ref, o_ref):
    for h in range(N_HEADS):
        sl = slice(h * HEAD_DIM, (h + 1) * HEAD_DIM)
        y = yf_ref[0, h] + yb_ref[0, h]
        mu = jnp.mean(y, axis=-1, keepdims=True)
        var = jnp.mean(jnp.square(y - mu), axis=-1, keepdims=True)
        yn = (y - mu) * lax.rsqrt(var + GN_EPS)
        yn = yn * lw_ref[:, sl] + lb_ref[:, sl] + bonus_ref[0, :, sl]
        o_ref[0, :, sl] = (yn * gate_ref[0, :, sl]).astype(o_ref.dtype)


def rwkv_finish(y_f, y_b, bonus, gate, lnx_w, lnx_b):
    bsz, n_heads, n_tok, n = y_f.shape
    tm = TOKEN_TILE
    heads = pl.BlockSpec((1, n_heads, tm, n), lambda b, i: (b, 0, i, 0))
    tokens = pl.BlockSpec((1, tm, n_heads * n), lambda b, i: (b, i, 0))
    vec = pl.BlockSpec((1, n_heads * n), lambda b, i: (0, 0))
    return pl.pallas_call(
        _rwkv_post_kernel,
        grid=(bsz, n_tok // tm),
        in_specs=[heads, heads, tokens, tokens, vec, vec],
        out_specs=tokens,
        out_shape=jax.ShapeDtypeStruct((bsz, n_tok, n_heads * n), BF16),
        compiler_params=_params("parallel", "parallel"),
        name="rwkv_finish",
    )(y_f, y_b, bonus, gate, lnx_w, lnx_b)


def _rope_tables(n_ctx, n_lat, rot_dim, n_rep):
    t = jnp.arange(n_lat, dtype=jnp.int32)
    rows = (t // GRID_W).astype(F32)
    cols = (t % GRID_W).astype(F32)
    per_axis = rot_dim // 2
    inv_freq = ROPE_THETA ** (-jnp.arange(0, per_axis, 2, dtype=F32) / per_axis)
    ang = jnp.concatenate([rows[:, None] * inv_freq, cols[:, None] * inv_freq], axis=-1)
    ang = jnp.concatenate([jnp.zeros((n_ctx, ang.shape[1]), F32), ang], axis=0)
    sign = jnp.tile(jnp.array([-1.0, 1.0], F32), rot_dim // 2)
    cos = jnp.repeat(jnp.cos(ang), 2, axis=-1)
    sin = jnp.repeat(jnp.sin(ang), 2, axis=-1) * sign
    return jnp.tile(cos, (1, n_rep)), jnp.tile(sin, (1, n_rep))


def _pairs(t):
    return t.reshape((t.shape[0], t.shape[1] // 2, 2) + t.shape[2:])


def _block_diag(m0, m1):
    z01 = jnp.zeros((m0.shape[0], m1.shape[1]), m0.dtype)
    z10 = jnp.zeros((m1.shape[0], m0.shape[1]), m0.dtype)
    return jnp.concatenate([jnp.concatenate([m0, z01], axis=1), jnp.concatenate([z10, m1], axis=1)], axis=0)


def _shift_halo(rw, ctx_tiles):
    tm = TOKEN_TILE
    zero = jnp.zeros_like(rw[:, :1])
    prev = jnp.concatenate([zero, rw[:, tm - 1::tm][:, :-1]], axis=1)
    nxt = jnp.concatenate([rw[:, ::tm][:, 1:], zero], axis=1)
    tile = jnp.arange(prev.shape[1])[None, :, None]
    prev = jnp.where(tile == ctx_tiles, 0.0, prev)
    nxt = jnp.where(tile == ctx_tiles - 1, 0.0, nxt)
    return jnp.stack([prev, nxt], axis=2)


def _rwkv_mixer(rw, ctx_tiles, shift_taps, w0, w2, a0, a2, k_k, k_a, r_k, g2, lnx_w, lnx_b):
    both = lambda t: t.reshape(1, -1)
    outs = rwkv_prepare(rw, _shift_halo(rw, ctx_tiles), shift_taps, both(w0),
                        _block_diag(w2[0], w2[1]).astype(BF16), both(a0),
                        _block_diag(a2[0], a2[1]).astype(BF16), both(k_k), both(k_a), both(r_k),
                        g2.astype(BF16))
    v, bonus, gate = outs[12], outs[15], outs[16]
    y_f = rwkv_chunk_scan(outs[0:6], v, outs[13], ctx_tiles, False)
    y_b = rwkv_chunk_scan(outs[6:12], v, outs[14], ctx_tiles, True)
    return rwkv_finish(y_f, y_b, bonus, gate, lnx_w[None], lnx_b[None])


def _dense_branch(q, k, v, n_ctx, want_ctx):
    n_tok = q.shape[3]
    ctx_tiles = n_ctx // ATTN_Q_TILE
    o_lat = attention(q, k, v, ctx_tiles, (n_tok - n_ctx) // ATTN_Q_TILE, n_tok, HEAD_DIM)
    o_ctx = attention(q, k, v, 0, ctx_tiles, n_ctx, HEAD_DIM) if want_ctx else None
    return o_lat, o_ctx


def kernel(x, c, ctx, c_ctx, w_mod, b_mod, norm1_g, norm2_g, w_in, rwkv_shift, na_rpb, gqa_q_norm, gqa_k_norm, mla_q_norm, mla_kv_norm, mla_w_uq, mla_w_ukv, rwkv_w0, rwkv_w2, rwkv_a0, rwkv_a2, rwkv_k_k, rwkv_k_a, rwkv_r_k, rwkv_g2, rwkv_lnx_w, rwkv_lnx_b, w_out, w_fc1, w_fc2, final_norm_g):
    bsz, n_lat, d_model = x.shape
    n_ctx = ctx.shape[1]
    depth = w_mod.shape[0]
    rows = n_lat // GRID_W
    ctx_tiles = n_ctx // TOKEN_TILE
    assert n_ctx % TOKEN_TILE == 0 and n_lat % TOKEN_TILE == 0 and rows % NA_Q_ROWS == 0

    tables = (_rope_tables(n_ctx, n_lat, HEAD_DIM, N_HEADS) + _rope_tables(n_ctx, n_lat, MLA_ROPE_DIM, N_HEADS))

    cond = jnp.concatenate([jax.nn.silu(c), jax.nn.silu(c_ctx)[None]], axis=0)
    n_cond = -(-cond.shape[0] // 8) * 8
    cond = jnp.pad(cond, ((0, n_cond - cond.shape[0]), (0, 0)))
    mod_all = modulation(cond, w_mod, b_mod[:, None, :])

    x_all = jnp.concatenate([ctx, x], axis=1)
    bounds = np.cumsum([0, NA_COLS, GQA_COLS, MLA_COLS, RWKV_COLS])
    mla_pad = -MLA_COLS % 128

    for layer in range(depth):
        want_ctx = layer < depth - 1
        mod = mod_all[layer].reshape(n_cond, 6, d_model)
        mod = jnp.stack([jnp.broadcast_to(mod[bsz], (bsz, 6, d_model)), mod[:bsz]], axis=1)

        w_na, w_gqa, w_mla, w_rw = [w_in[layer][:, bounds[i]:bounds[i + 1]].astype(BF16) for i in range(4)]
        w_uq = mla_w_uq[layer].reshape(MLA_Q_RANK, N_HEADS, MLA_QK_DIM)
        w_uq = jnp.concatenate([w_uq[:, :, :HEAD_DIM].reshape(MLA_Q_RANK, -1),
                                w_uq[:, :, HEAD_DIM:].reshape(MLA_Q_RANK, -1)], axis=1)
        w_ukv = mla_w_ukv[layer].reshape(MLA_KV_RANK, N_HEADS, 2 * HEAD_DIM)
        w_ukv = jnp.concatenate([w_ukv[:, :, :HEAD_DIM].reshape(MLA_KV_RANK, -1),
                                 w_ukv[:, :, HEAD_DIM:].reshape(MLA_KV_RANK, -1)], axis=1)
        weights = (w_na, w_gqa, jnp.pad(w_mla, ((0, 0), (0, mla_pad))), w_rw,
                   jnp.tile(gqa_q_norm[layer], N_HEADS)[None], jnp.tile(gqa_k_norm[layer], GQA_KV_HEADS)[None],
                   mla_q_norm[layer][None], mla_kv_norm[layer][None], w_uq.astype(BF16), w_ukv.astype(BF16))
        (a_q, a_k, a_v, b_q, b_k, b_v, c_q, c_k, c_v, rw) = in_projection(
            x_all, mod, norm1_g[layer][None], weights, tables, ctx_tiles)

        a_q, a_k, a_v = _pairs(a_q), _pairs(a_k), _pairs(a_v)
        bias = na_bias_table(na_rpb[layer], rows)
        out_a = neighbourhood_attention(a_q, a_k, a_v, bias.reshape((N_HEADS // 2, 2) + bias.shape[1:]), n_ctx)
        out_ac = attention(a_q, a_k, a_v, 0, ctx_tiles, n_ctx, HEAD_DIM) if want_ctx else None
        mix_b = _dense_branch(_pairs(b_q), b_k[:, :, None], b_v[:, :, None], n_ctx, want_ctx)
        mix_c = _dense_branch(_pairs(c_q), _pairs(c_k), _pairs(c_v), n_ctx, want_ctx)
        mix_d = _rwkv_mixer(rw, ctx_tiles, rwkv_shift[layer], rwkv_w0[layer], rwkv_w2[layer], rwkv_a0[layer],
                            rwkv_a2[layer], rwkv_k_k[layer], rwkv_k_a[layer], rwkv_r_k[layer],
                            rwkv_g2[layer], rwkv_lnx_w[layer], rwkv_lnx_b[layer])

        x_all = out_projection_mlp(x_all, [(out_a, out_ac), mix_b, mix_c], mix_d, mod, norm2_g[layer][None],
                                   final_norm_g[None], w_out[layer].astype(BF16), w_fc1[layer].astype(BF16),
                                   w_fc2[layer].astype(BF16), ctx_tiles, not want_ctx)
    return x_all
```

```python
import functools

import numpy as np
import jax
import jax.numpy as jnp
from jax import lax
from jax.experimental import pallas as pl
from jax.experimental.pallas import tpu as pltpu

F32 = jnp.float32
BF16 = jnp.bfloat16

GRID_W = 64
HEAD_DIM = 64
N_HEADS = 4
GROUP_WIDTH = N_HEADS * HEAD_DIM
GQA_KV_HEADS = 2
NA_WIN_ROWS = 8
NA_WIN_COLS = 16
MLA_Q_RANK = 256
MLA_KV_RANK = 128
MLA_ROPE_DIM = 32
MLA_QK_DIM = HEAD_DIM + MLA_ROPE_DIM
RWKV_DECAY_RANK = 64
RWKV_ICLR_RANK = 64
RWKV_GATE_RANK = 160
NA_COLS = 3 * GROUP_WIDTH
GQA_COLS = (N_HEADS + 2 * GQA_KV_HEADS) * HEAD_DIM
MLA_COLS = MLA_Q_RANK + MLA_KV_RANK + MLA_ROPE_DIM
RWKV_COLS = 3 * GROUP_WIDTH + 2 * RWKV_DECAY_RANK + 2 * RWKV_ICLR_RANK + RWKV_GATE_RANK
ROPE_THETA = 10000.0
RMS_EPS = 1e-6
GN_EPS = 64e-5
NEG_INF = -1e30

TOKEN_TILE = 256
ATTN_Q_TILE = 256
ATTN_KV_CHUNK = 1024
V_EXT_WIDTH = 128
LOG2E = float(np.log2(np.e))
NA_Q_ROWS = ATTN_Q_TILE // GRID_W
NA_KEY_ROWS = NA_Q_ROWS + NA_WIN_ROWS - 1
RWKV_CHUNK = 64
RWKV_ROWS_PER_STEP = 4
VMEM_LIMIT = 56 * 1024 * 1024


def _params(*sem):
    return pltpu.CompilerParams(dimension_semantics=sem, vmem_limit_bytes=VMEM_LIMIT)


def _bdot(a, b):
    return jnp.dot(a.astype(BF16), b.astype(BF16), preferred_element_type=F32)


def _bdot_nt(a, b):
    return lax.dot_general(a.astype(BF16), b.astype(BF16), (((1,), (1,)), ((), ())),
                           preferred_element_type=F32)


def _bdot_tn(a, b):
    return lax.dot_general(a.astype(BF16), b.astype(BF16), (((0,), (0,)), ((), ())),
                           preferred_element_type=F32)


def _matmul_kernel(a_ref, b_ref, o_ref):
    o_ref[...] = _bdot(a_ref[...], b_ref[...]).astype(o_ref.dtype)


def matmul(a, b, out_dtype=F32, tm=1024):
    m, k = a.shape
    n = b.shape[1]
    tm = min(tm, m)
    while m % tm:
        tm //= 2
    return pl.pallas_call(
        _matmul_kernel,
        grid=(m // tm,),
        in_specs=[pl.BlockSpec((tm, k), lambda i: (i, 0)),
                  pl.BlockSpec((k, n), lambda i: (0, 0))],
        out_specs=pl.BlockSpec((tm, n), lambda i: (i, 0)),
        out_shape=jax.ShapeDtypeStruct((m, n), out_dtype),
        compiler_params=_params("parallel"),
        name="matmul",
    )(a, b)


def _mod_kernel(a_ref, w_ref, b_ref, o_ref):
    o_ref[0] = _bdot(a_ref[...], w_ref[0]) + b_ref[0]


def modulation(cond, w_mod, b_mod, tn=1536):
    depth, d, n = w_mod.shape
    r = cond.shape[0]
    return pl.pallas_call(
        _mod_kernel,
        grid=(depth, n // tn),
        in_specs=[pl.BlockSpec((r, d), lambda l, j: (0, 0)),
                  pl.BlockSpec((1, d, tn), lambda l, j: (l, 0, j)),
                  pl.BlockSpec((1, 1, tn), lambda l, j: (l, 0, j))],
        out_specs=pl.BlockSpec((1, r, tn), lambda l, j: (l, 0, j)),
        out_shape=jax.ShapeDtypeStruct((depth, r, n), F32),
        compiler_params=_params("parallel", "parallel"),
        name="modulation",
    )(cond, w_mod, b_mod)


def _rms(x, g):
    return x * lax.rsqrt(jnp.mean(x * x, axis=-1, keepdims=True) + RMS_EPS) * g


def _split3(x):
    hi = x.astype(BF16)
    rest = x - hi.astype(F32)
    mid = rest.astype(BF16)
    return hi, mid, (rest - mid.astype(F32)).astype(BF16)


def _select_rows(sel, x):
    sel = sel.astype(BF16)
    hi, mid, lo = _split3(x)
    dot = lambda part: jnp.dot(sel, part, preferred_element_type=F32)
    return dot(lo) + dot(mid) + dot(hi)


def _select_cols(x, sel):
    sel = sel.astype(BF16)
    hi, mid, lo = _split3(x)
    dot = lambda part: jnp.dot(part, sel, preferred_element_type=F32)
    return dot(lo) + dot(mid) + dot(hi)


def _pair_swap(x):
    axis = x.ndim - 1
    lane = lax.broadcasted_iota(jnp.int32, x.shape, axis)
    return jnp.where((lane & 1) == 0, pltpu.roll(x, x.shape[axis] - 1, axis=axis), pltpu.roll(x, 1, axis=axis))


def _rotate(x, cos, sin):
    return x * cos + _pair_swap(x) * sin


def _in_proj_kernel(x_ref, mod_ref, g_ref, wna_ref, wgqa_ref, wmla_ref, wrw_ref, gq_ref, gk_ref,
                    mq_ref, mkv_ref, wuq_ref, wukv_ref, cosh_ref, sinh_ref, cosm_ref, sinm_ref,
                    naq_ref, nak_ref, nav_ref, bq_ref, bk_ref, bv_ref, cq_ref, ck_ref, cv_ref, rw_ref, edge_ref):
    hd, gw = HEAD_DIM, GROUP_WIDTH
    mod = mod_ref[0, 0]
    hb = (_rms(x_ref[0], g_ref[...]) * (1.0 + mod[1:2]) + mod[0:1]).astype(BF16)
    tm = hb.shape[0]
    ones_col = (lax.broadcasted_iota(jnp.int32, (tm, V_EXT_WIDTH - hd), 1) == 0).astype(F32)
    qk_scale = LOG2E * hd ** -0.5

    def put_heads(ref, val, n_heads, width):
        for h in range(n_heads):
            ref[0, h] = val[:, h * width:(h + 1) * width].astype(ref.dtype)

    def put_values(ref, val, n_heads):
        for h in range(n_heads):
            ref[0, h] = jnp.concatenate([val[:, h * hd:(h + 1) * hd], ones_col], axis=-1).astype(ref.dtype)

    na = jnp.dot(hb, wna_ref[...], preferred_element_type=F32)
    put_heads(naq_ref, na[:, :gw] * qk_scale, N_HEADS, hd)
    put_heads(nak_ref, na[:, gw:2 * gw], N_HEADS, hd)
    put_values(nav_ref, na[:, 2 * gw:], N_HEADS)

    gqa = jnp.dot(hb, wgqa_ref[...], preferred_element_type=F32)
    row = lax.broadcasted_iota(jnp.int32, (gw, gw), 0)
    col = lax.broadcasted_iota(jnp.int32, (gw, gw), 1)
    shift = hd.bit_length() - 1
    head_sum = ((row >> shift) == (col >> shift)).astype(F32)
    cos_h, sin_h = cosh_ref[...], sinh_ref[...]
    kw = GQA_KV_HEADS * hd
    q = gqa[:, :gw]
    q = q * lax.rsqrt(_select_cols(q * q, head_sum) * (1.0 / hd) + RMS_EPS) * gq_ref[...]
    put_heads(bq_ref, _rotate(q, cos_h, sin_h) * qk_scale, N_HEADS, hd)
    k = gqa[:, gw:gw + kw]
    k = k * lax.rsqrt(_select_cols(k * k, head_sum[:kw, :kw]) * (1.0 / hd) + RMS_EPS) * gk_ref[...]
    put_heads(bk_ref, _rotate(k, cos_h[:, :kw], sin_h[:, :kw]), GQA_KV_HEADS, hd)
    put_values(bv_ref, gqa[:, gw + kw:], GQA_KV_HEADS)

    mla = jnp.dot(hb, wmla_ref[...], preferred_element_type=F32)
    cos_m, sin_m = cosm_ref[...], sinm_ref[...]
    mla_scale = LOG2E * MLA_QK_DIM ** -0.5
    uq = jnp.dot(_rms(mla[:, :MLA_Q_RANK], mq_ref[...]).astype(BF16), wuq_ref[...], preferred_element_type=F32)
    q_nope = uq[:, :gw] * mla_scale
    q_rope = _rotate(uq[:, gw:], cos_m, sin_m) * mla_scale
    lo = MLA_Q_RANK + MLA_KV_RANK
    ukv = jnp.dot(_rms(mla[:, MLA_Q_RANK:lo], mkv_ref[...]).astype(BF16), wukv_ref[...],
                  preferred_element_type=F32)
    k_rope = _rotate(mla[:, lo:], cos_m, sin_m)[:, :MLA_ROPE_DIM]
    rd = MLA_ROPE_DIM
    for h in range(N_HEADS):
        cq_ref[0, h] = jnp.concatenate([q_nope[:, h * hd:(h + 1) * hd], q_rope[:, h * rd:(h + 1) * rd]],
                                       axis=-1).astype(cq_ref.dtype)
        ck_ref[0, h] = jnp.concatenate([ukv[:, h * hd:(h + 1) * hd], k_rope], axis=-1).astype(ck_ref.dtype)
    put_values(cv_ref, ukv[:, gw:], N_HEADS)

    rw = jnp.dot(hb, wrw_ref[...], preferred_element_type=F32)
    rw_ref[0] = rw
    edge_ref[0, 0] = jnp.concatenate([rw[0:1], rw[tm - 1:tm]], axis=0)


def in_projection(x_all, mod, g, weights, tables, ctx_tiles):
    bsz, n_tok, d = x_all.shape
    tm = TOKEN_TILE
    tok = lambda b, i: (b, i, 0)
    full = lambda b, i: (0, 0)
    heads = lambda n, w: (jax.ShapeDtypeStruct((bsz, n, n_tok, w), BF16),
                          pl.BlockSpec((1, n, tm, w), lambda b, i: (b, 0, i, 0)))
    outs = [heads(N_HEADS, HEAD_DIM), heads(N_HEADS, HEAD_DIM), heads(N_HEADS, V_EXT_WIDTH),
            heads(N_HEADS, HEAD_DIM), heads(GQA_KV_HEADS, HEAD_DIM), heads(GQA_KV_HEADS, V_EXT_WIDTH),
            heads(N_HEADS, MLA_QK_DIM), heads(N_HEADS, MLA_QK_DIM), heads(N_HEADS, V_EXT_WIDTH),
            (jax.ShapeDtypeStruct((bsz, n_tok, RWKV_COLS), F32), pl.BlockSpec((1, tm, RWKV_COLS), tok)),
            (jax.ShapeDtypeStruct((bsz, n_tok // tm, 2, RWKV_COLS), F32),
             pl.BlockSpec((1, 1, 2, RWKV_COLS), lambda b, i: (b, i, 0, 0)))]
    return pl.pallas_call(
        _in_proj_kernel,
        grid=(bsz, n_tok // tm),
        in_specs=[pl.BlockSpec((1, tm, d), tok),
                  pl.BlockSpec((1, 1, 6, d), lambda b, i: (b, jnp.where(i < ctx_tiles, 0, 1), 0, 0)),
                  pl.BlockSpec((1, d), full)]
                 + [pl.BlockSpec(w.shape, full) for w in weights]
                 + [pl.BlockSpec((tm, t.shape[1]), lambda b, i: (i, 0)) for t in tables],
        out_specs=[o[1] for o in outs],
        out_shape=[o[0] for o in outs],
        compiler_params=_params("parallel", "parallel"),
        name="in_projection",
    )(x_all, mod, g, *weights, *tables)


def _out_mlp_kernel(*refs, hidden_tile, final_norm, ctx_tiles, n_branches):
    x_ref = refs[0]
    per_branch = 2 if ctx_tiles else 1
    branch_refs = refs[1:1 + per_branch * n_branches]
    mixd_ref, mod_ref, g2_ref, gf_ref, wout_ref, w1_ref, w2_ref, o_ref = refs[1 + per_branch * n_branches:]
    parts = []
    for j in range(n_branches):
        lat = branch_refs[per_branch * j][0]
        if ctx_tiles:
            lat = jnp.where(pl.program_id(1) < ctx_tiles, branch_refs[per_branch * j + 1][0], lat)
        parts.append(lat)
    parts.append(mixd_ref[0])
    mod = mod_ref[0, 0]
    proj = jnp.zeros(x_ref.shape[1:], F32)
    lo = 0
    for part in parts:
        proj = proj + jnp.dot(part, wout_ref[lo:lo + part.shape[1], :], preferred_element_type=F32)
        lo += part.shape[1]
    x1 = x_ref[0] + mod[2:3] * proj
    hb = (_rms(x1, g2_ref[...]) * (1.0 + mod[4:5]) + mod[3:4]).astype(BF16)
    acc = jnp.zeros_like(x1)
    for j in range(w1_ref.shape[1] // hidden_tile):
        cols = pl.ds(j * hidden_tile, hidden_tile)
        u = jnp.maximum(jnp.dot(hb, w1_ref[:, cols], preferred_element_type=F32), 0.0)
        acc = acc + jnp.dot((u * u).astype(BF16), w2_ref[cols, :], preferred_element_type=F32)
    x2 = x1 + mod[5:6] * acc
    if final_norm:
        x2 = _rms(x2, gf_ref[...])
    o_ref[0] = x2


def out_projection_mlp(x_all, branches, mix_d, mod, g2, g_final, w_out, w_fc1, w_fc2, ctx_tiles, last):
    bsz, n_tok, d = x_all.shape
    tm = TOKEN_TILE
    n_tiles = n_tok // tm
    skip = ctx_tiles if last else 0
    tok = lambda b, i: (b, i + skip, 0)
    full = lambda b, i: (0, 0)
    lat_map = lambda b, i: (b, jnp.maximum(i + skip - ctx_tiles, 0), 0)
    ctx_map = lambda b, i: (b, jnp.minimum(i, ctx_tiles - 1), 0)
    branch_args, branch_specs = [], []
    for lat, ctx_rows in branches:
        branch_args.append(lat)
        branch_specs.append(pl.BlockSpec((1, tm, lat.shape[2]), lat_map))
        if not last:
            branch_args.append(ctx_rows)
            branch_specs.append(pl.BlockSpec((1, tm, ctx_rows.shape[2]), ctx_map))
    kern = functools.partial(_out_mlp_kernel, hidden_tile=1024, final_norm=last,
                             ctx_tiles=0 if last else ctx_tiles, n_branches=len(branches))
    return pl.pallas_call(
        kern,
        grid=(bsz, n_tiles - skip),
        in_specs=[pl.BlockSpec((1, tm, d), tok)] + branch_specs
                 + [pl.BlockSpec((1, tm, mix_d.shape[2]), tok),
                    pl.BlockSpec((1, 1, 6, d), lambda b, i: (b, jnp.where(i + skip < ctx_tiles, 0, 1), 0, 0)),
                    pl.BlockSpec((1, d), full),
                    pl.BlockSpec((1, d), full),
                    pl.BlockSpec(w_out.shape, full, pipeline_mode=pl.Buffered(1)),
                    pl.BlockSpec(w_fc1.shape, full, pipeline_mode=pl.Buffered(1)),
                    pl.BlockSpec(w_fc2.shape, full, pipeline_mode=pl.Buffered(1))],
        out_specs=pl.BlockSpec((1, tm, d), lambda b, i: (b, i, 0)),
        out_shape=jax.ShapeDtypeStruct((bsz, n_tok - skip * tm, d), F32),
        compiler_params=_params("parallel", "parallel"),
        name="out_projection_mlp",
    )(x_all, *branch_args, mix_d, mod, g2, g_final, w_out, w_fc1, w_fc2)


def _key_chunks(n_keys):
    first = min(n_keys, ATTN_Q_TILE)
    chunks = [(0, first)]
    while chunks[-1][0] + chunks[-1][1] < n_keys:
        start = chunks[-1][0] + chunks[-1][1]
        chunks.append((start, min(ATTN_KV_CHUNK, n_keys - start)))
    return chunks


def _online_softmax(q, k_at, v_at, chunks):
    rows = q.shape[0]
    scores = lambda c: _bdot_nt(q, k_at(*c))
    m = jnp.full((rows, 1), NEG_INF, F32)
    acc = jnp.zeros((rows, V_EXT_WIDTH), F32)
    s_next = scores(chunks[0])
    for j, chunk in enumerate(chunks):
        s = s_next
        if j + 1 < len(chunks):
            s_next = scores(chunks[j + 1])
        m_new = jnp.maximum(m, jnp.max(s, axis=-1, keepdims=True))
        p = jnp.exp2(s - m_new)
        acc = acc * jnp.exp2(m - m_new) + jnp.dot(p.astype(BF16), v_at(*chunk), preferred_element_type=F32)
        m = m_new
    return acc


def _attention_kernel(q_ref, k_ref, v_ref, o_ref):
    n_q, tq, dk = q_ref.shape[2], q_ref.shape[3], q_ref.shape[4]
    dv = o_ref.shape[-1] // n_q
    chunks = _key_chunks(k_ref.shape[3])
    if k_ref.shape[2] == 1:
        groups = [(q_ref[0, 0].reshape(n_q * tq, dk), 0)]
    else:
        groups = [(q_ref[0, 0, g], g) for g in range(n_q)]
    outs = []
    for q, g in groups:
        acc = _online_softmax(q, lambda s, n: k_ref[0, 0, g, pl.ds(s, n), :],
                              lambda s, n: v_ref[0, 0, g, pl.ds(s, n), :], chunks)
        o = acc[:, :dv] / acc[:, dv:dv + 1]
        outs.extend(o[i * tq:(i + 1) * tq] for i in range(o.shape[0] // tq))
    o_ref[0] = jnp.concatenate(outs, axis=-1).astype(o_ref.dtype)


def attention(q, k, v, q_tile0, n_q_tiles, n_keys, dv):
    bsz, n_pairs, n_q, _, dk = q.shape
    n_kv = k.shape[2]
    tq = ATTN_Q_TILE
    return pl.pallas_call(
        _attention_kernel,
        grid=(bsz, n_pairs, n_q_tiles),
        in_specs=[pl.BlockSpec((1, 1, n_q, tq, dk), lambda b, h, i: (b, h, 0, i + q_tile0, 0)),
                  pl.BlockSpec((1, 1, n_kv, n_keys, dk), lambda b, h, i: (b, h, 0, 0, 0)),
                  pl.BlockSpec((1, 1, n_kv, n_keys, v.shape[-1]), lambda b, h, i: (b, h, 0, 0, 0))],
        out_specs=pl.BlockSpec((1, tq, n_q * dv), lambda b, h, i: (b, i, h)),
        out_shape=jax.ShapeDtypeStruct((bsz, n_q_tiles * tq, n_pairs * n_q * dv), BF16),
        compiler_params=_params("parallel", "parallel", "parallel"),
        name="attention",
    )(q, k, v)


def _na_kernel(q_ref, k_ref, v_ref, bias_ref, o_ref, *, n_ctx, n_row_tiles):
    i = pl.program_id(2)
    n_win = NA_KEY_ROWS * GRID_W
    max_row0 = n_row_tiles * NA_Q_ROWS - NA_KEY_ROWS
    row0 = jnp.clip(i * NA_Q_ROWS - NA_WIN_ROWS // 2, 0, max_row0)
    start = pl.multiple_of(n_ctx + row0 * GRID_W, GRID_W)
    dv = o_ref.shape[-1] // q_ref.shape[2]
    outs = []
    for g in range(q_ref.shape[2]):
        q = q_ref[0, 0, g]
        s_win = _bdot_nt(q, k_ref[0, 0, g, pl.ds(start, n_win), :]) + bias_ref[0, g, 0]
        s_ctx = _bdot_nt(q, k_ref[0, 0, g, pl.ds(0, n_ctx), :])
        m = jnp.maximum(jnp.max(s_win, axis=-1, keepdims=True), jnp.max(s_ctx, axis=-1, keepdims=True))
        p_win = jnp.exp2(s_win - m).astype(BF16)
        p_ctx = jnp.exp2(s_ctx - m).astype(BF16)
        o = (jnp.dot(p_win, v_ref[0, 0, g, pl.ds(start, n_win), :], preferred_element_type=F32)
             + jnp.dot(p_ctx, v_ref[0, 0, g, pl.ds(0, n_ctx), :], preferred_element_type=F32))
        outs.append(o[:, :dv] / o[:, dv:dv + 1])
    o_ref[0] = jnp.concatenate(outs, axis=-1).astype(o_ref.dtype)


def na_bias_table(rpb, rows):
    n_tiles = rows // NA_Q_ROWS
    n_heads, n_dr, n_dc = rpb.shape
    qc = np.arange(GRID_W)[:, None]
    kc = np.arange(GRID_W)[None, :]
    dc = np.clip(kc - qc + NA_WIN_COLS - 1, 0, n_dc - 1)
    onehot = jnp.asarray(dc.reshape(-1)[None, :] == np.arange(n_dc)[:, None], F32)
    by_col = jnp.dot(rpb.reshape(-1, n_dc).astype(F32), onehot, precision=lax.Precision.HIGHEST)
    pad = NA_KEY_ROWS
    by_col = jnp.pad(by_col.reshape(n_heads, n_dr, GRID_W, GRID_W), ((0, 0), (pad, pad), (0, 0), (0, 0)))
    c_start = np.clip(qc - NA_WIN_COLS // 2, 0, GRID_W - NA_WIN_COLS)
    col_ok = (kc >= c_start) & (kc < c_start + NA_WIN_COLS)
    tables = []
    for tile in (0, 1, n_tiles - 1):
        row0 = int(np.clip(tile * NA_Q_ROWS - NA_WIN_ROWS // 2, 0, rows - NA_KEY_ROWS))
        per_row = []
        for a in range(NA_Q_ROWS):
            qr = tile * NA_Q_ROWS + a
            dr0 = row0 - qr + NA_WIN_ROWS - 1 + pad
            kr = row0 + np.arange(NA_KEY_ROWS)
            r_start = int(np.clip(qr - NA_WIN_ROWS // 2, 0, rows - NA_WIN_ROWS))
            row_ok = (kr >= r_start) & (kr < r_start + NA_WIN_ROWS)
            valid = row_ok[None, :, None] & col_ok[:, None, :]
            sl = jnp.swapaxes(by_col[:, dr0:dr0 + NA_KEY_ROWS], 1, 2)
            per_row.append(jnp.where(valid[None], sl * LOG2E, NEG_INF))
        bias = jnp.stack(per_row, axis=1)
        tables.append(bias.reshape(n_heads, NA_Q_ROWS * GRID_W, NA_KEY_ROWS * GRID_W))
    return jnp.stack(tables, axis=1)


def neighbourhood_attention(q, k, v, bias, n_ctx):
    bsz, n_pairs, n_q, n_tok, d = q.shape
    tq = ATTN_Q_TILE
    n_tiles = (n_tok - n_ctx) // tq
    ctx_tiles = n_ctx // tq
    kern = functools.partial(_na_kernel, n_ctx=n_ctx, n_row_tiles=n_tiles)

    def bias_map(b, h, i):
        return (h, 0, jnp.where(i == 0, 0, jnp.where(i == n_tiles - 1, 2, 1)), 0, 0)

    whole = lambda b, h, i: (b, h, 0, 0, 0)
    return pl.pallas_call(
        kern,
        grid=(bsz, n_pairs, n_tiles),
        in_specs=[pl.BlockSpec((1, 1, n_q, tq, d), lambda b, h, i: (b, h, 0, i + ctx_tiles, 0)),
                  pl.BlockSpec((1, 1, n_q, n_tok, d), whole),
                  pl.BlockSpec((1, 1, n_q, n_tok, v.shape[-1]), whole),
                  pl.BlockSpec((1, n_q, 1) + bias.shape[3:], bias_map)],
        out_specs=pl.BlockSpec((1, tq, n_q * d), lambda b, h, i: (b, i, h)),
        out_shape=jax.ShapeDtypeStruct((bsz, n_tok - n_ctx, n_pairs * n_q * d), BF16),
        compiler_params=_params("parallel", "parallel", "parallel"),
        name="neighbourhood_attention",
    )(q, k, v, bias)


def _softplus(z):
    return jnp.maximum(z, 0.0) + jnp.log(1.0 + jnp.exp(-jnp.abs(z)))


def _sigmoid(z):
    return 1.0 / (1.0 + jnp.exp(-z))


def _rwkv_prep_kernel(u_ref, halo_ref, taps_ref, w0_ref, w2_ref, a0_ref, a2_ref, kk_ref, ka_ref, rk_ref,
                      g2_ref, *out_refs):
    dir_refs = (out_refs[0:4], out_refs[4:8])
    v_out, gt_refs, bonus_ref, gate_ref = out_refs[8], out_refs[9:11], out_refs[11], out_refs[12]
    tm = u_ref.shape[1]
    c = RWKV_CHUNK
    n_chunks = tm // c
    gw = GROUP_WIDTH
    u = u_ref[0]
    halo = halo_ref[0, 0]
    tok = lax.broadcasted_iota(jnp.int32, u.shape, 0)
    prev = jnp.where(tok == 0, halo[0:1], pltpu.roll(u, 1, axis=0))
    nxt = jnp.where(tok == tm - 1, halo[1:2], pltpu.roll(u, tm - 1, axis=0))
    taps = taps_ref[...]
    s = prev * taps[0:1] + u * taps[1:2] + nxt * taps[2:3]

    r, k, v = s[:, 0:gw], s[:, gw:2 * gw], s[:, 2 * gw:3 * gw]
    lo = 3 * gw
    w_low = jnp.tanh(s[:, lo:lo + 2 * RWKV_DECAY_RANK])
    lo += 2 * RWKV_DECAY_RANK
    a_low = s[:, lo:lo + 2 * RWKV_ICLR_RANK]
    lo += 2 * RWKV_ICLR_RANK
    g_low = _sigmoid(s[:, lo:lo + RWKV_GATE_RANK])
    w_log = -_softplus(-(w0_ref[...] + _bdot(w_low, w2_ref[...]))) - 0.5
    log_decay = -jnp.exp(w_log)
    iclr = _sigmoid(a0_ref[...] + _bdot(a_low, a2_ref[...]))
    gate_ref[0] = _bdot(g_low, g2_ref[...])

    row = lax.broadcasted_iota(jnp.int32, (tm, tm), 0)
    col = lax.broadcasted_iota(jnp.int32, (tm, tm), 1)
    shift = c.bit_length() - 1
    same = (row >> shift) == (col >> shift)
    ones_blk = same.astype(F32)

    def head_major(ref, val):
        for h in range(N_HEADS):
            ref[0, h] = val[:, h * HEAD_DIM:(h + 1) * HEAD_DIM].astype(ref.dtype)

    head_major(v_out, v)
    bonus = jnp.zeros_like(r)
    for d in range(2):
        sl = slice(d * gw, (d + 1) * gw)
        lw = log_decay[:, sl]
        kk = k * kk_ref[:, sl]
        kk = kk * lax.rsqrt(jnp.maximum(_select_cols(kk * kk, ones_blk), 1e-24))
        k_d = k * (1.0 + (iclr[:, sl] - 1.0) * ka_ref[:, sl])
        b_vec = kk * iclr[:, sl]
        tri = (same & ((row >= col) if d == 0 else (row <= col))).astype(F32)
        cum = _select_rows(tri, lw)
        ends = [cum[i * c + (c - 1 if d == 0 else 0)][None] for i in range(n_chunks)]
        e_inv = jnp.exp(-cum)
        streams = (r * jnp.exp(cum), -kk * jnp.exp(cum - lw), b_vec * e_inv, k_d * e_inv)
        for ref, val in zip(dir_refs[d], streams):
            head_major(ref, val)
        g_tot = jnp.exp(jnp.concatenate(ends, axis=0))
        for h in range(N_HEADS):
            gt_refs[d][0, 0, h * n_chunks:(h + 1) * n_chunks, :] = g_tot[:, h * HEAD_DIM:(h + 1) * HEAD_DIM]
        bonus = bonus + r * k_d * rk_ref[:, sl]
    bonus_ref[0] = _select_cols(bonus, ones_blk) * v


def rwkv_prepare(rw, halo, taps, w0, w2, a0, a2, k_k, k_a, r_k, g2):
    bsz, n_tok, width = rw.shape
    tm = TOKEN_TILE
    n_tiles = n_tok // tm
    n_chunks = tm // RWKV_CHUNK
    full = lambda b, i: (0, 0)
    consts = (taps, w0, w2, a0, a2, k_k, k_a, r_k, g2)
    stream = jax.ShapeDtypeStruct((bsz, N_HEADS, n_tok, HEAD_DIM), BF16)
    stream_spec = pl.BlockSpec((1, N_HEADS, tm, HEAD_DIM), lambda b, i: (b, 0, i, 0))
    decay = jax.ShapeDtypeStruct((bsz, n_tiles, N_HEADS * n_chunks, HEAD_DIM), F32)
    decay_spec = pl.BlockSpec((1, 1, N_HEADS * n_chunks, HEAD_DIM), lambda b, i: (b, i, 0, 0))
    tokens = jax.ShapeDtypeStruct((bsz, n_tok, GROUP_WIDTH), F32)
    tokens_spec = pl.BlockSpec((1, tm, GROUP_WIDTH), lambda b, i: (b, i, 0))
    return pl.pallas_call(
        _rwkv_prep_kernel,
        grid=(bsz, n_tiles),
        in_specs=[pl.BlockSpec((1, tm, width), lambda b, i: (b, i, 0)),
                  pl.BlockSpec((1, 1, 2, width), lambda b, i: (b, i, 0, 0))]
                 + [pl.BlockSpec(t.shape, full) for t in consts],
        out_specs=[stream_spec] * 9 + [decay_spec] * 2 + [tokens_spec] * 2,
        out_shape=[stream] * 9 + [decay] * 2 + [tokens] * 2,
        compiler_params=_params("parallel", "parallel"),
        name="rwkv_prepare",
    )(rw, halo, *consts)


def _bmm(a, b):
    return lax.dot_general(a.astype(BF16), b.astype(BF16), (((2,), (1,)), ((0,), (0,))),
                           preferred_element_type=F32)


def _bmm_nt(a, b):
    return lax.dot_general(a.astype(BF16), b.astype(BF16), (((2,), (2,)), ((0,), (0,))),
                           preferred_element_type=F32)


def _bmm_tn(a, b):
    return lax.dot_general(a.astype(BF16), b.astype(BF16), (((1,), (1,)), ((0,), (0,))),
                           preferred_element_type=F32)


def _rwkv_chunk_kernel(rt_ref, at_ref, bt_ref, kt_ref, v_ref, gt_ref, y_ref, s_ref, *, reverse):
    n_rows, n_heads, tb, n = v_ref.shape
    c = RWKV_CHUNK
    n_chunks = tb // c
    n_seq = n_rows * n_heads
    g = n_seq * n_chunks

    @pl.when(pl.program_id(1) == 0)
    def _():
        s_ref[...] = jnp.zeros_like(s_ref)

    load = lambda ref: ref[...].reshape(g, c, n)
    rt, at, bt, kt, v = (load(ref) for ref in (rt_ref, at_ref, bt_ref, kt_ref, v_ref))
    row = lax.broadcasted_iota(jnp.int32, (c, c), 0)
    col = lax.broadcasted_iota(jnp.int32, (c, c), 1)
    if reverse:
        row, col = col, row
    row2 = lax.broadcasted_iota(jnp.int32, (c, 2 * c), 0)
    col2 = lax.broadcasted_iota(jnp.int32, (c, 2 * c), 1) & (c - 1)
    if reverse:
        row2, col2 = col2, row2
    strict = row2 > col2
    incl = row2 >= col2
    cat = lambda a, b: jnp.concatenate([a, b], axis=1)
    zeros = jnp.zeros_like(v)
    prod = _bmm_nt(cat(at, rt), cat(bt, kt))
    top = jnp.where(strict, prod[:, :c], 0.0)
    bot = jnp.where(incl, prod[:, c:], 0.0)
    a_ab = top[:, :, :c]
    av = _bmm(top, cat(zeros, v))
    n1 = jnp.where((row >> 3) == (col >> 3), a_ab, 0.0)
    n2 = _bmm(n1, n1)
    n4 = _bmm(n2, n2)
    p = (row == col).astype(F32) + n1 + n2 + _bmm(n1, n2)
    inv = p + _bmm(p, n4)
    for level in range(3, c.bit_length() - 1):
        rb, cb = row >> level, col >> level
        off = jnp.where((rb == cb + 1) & ((rb & 1) == 1), a_ab, 0.0)
        inv = inv + _bmm(_bmm(inv, off), inv)
    w = _bmm(inv, at)
    u0 = _bmm(inv, av)
    q_eff = rt.astype(F32) + _bmm(bot, cat(w, zeros))
    y0 = _bmm(bot, cat(u0, v))

    per_chunk = lambda t: t.reshape((n_seq, n_chunks) + t.shape[1:])
    qw = per_chunk(cat(q_eff, w))
    u0, y0, v = per_chunk(u0), per_chunk(y0), per_chunk(v)
    g_tot = gt_ref[:, 0].reshape(n_seq, n_chunks, 1, n)
    bk_end = per_chunk(cat(bt, kt)) * g_tot
    state = s_ref[...]
    for i in (range(n_chunks - 1, -1, -1) if reverse else range(n_chunks)):
        yu = _bmm_nt(qw[:, i], state)
        y_ref[:, :, i * c:(i + 1) * c, :] = (yu[:, :c] + y0[:, i]).reshape(n_rows, n_heads, c, n)
        state = g_tot[:, i] * state + _bmm_tn(cat(yu[:, c:] + u0[:, i], v[:, i]), bk_end[:, i])
    s_ref[...] = state


def rwkv_chunk_scan(streams, v, g_tot, ctx_blocks, reverse):
    bsz, n_heads, n_tok, n = v.shape
    tb = TOKEN_TILE
    n_blocks = n_tok // tb

    def block(i):
        if not reverse:
            return i
        return jnp.where(i < ctx_blocks, ctx_blocks - 1 - i, n_blocks - 1 + ctx_blocks - i)

    rows = RWKV_ROWS_PER_STEP
    spec = pl.BlockSpec((rows, n_heads, tb, n), lambda b, i: (b, 0, block(i), 0))
    return pl.pallas_call(
        functools.partial(_rwkv_chunk_kernel, reverse=reverse),
        grid=(bsz // rows, n_blocks),
        in_specs=[spec] * 5 + [pl.BlockSpec((rows, 1) + g_tot.shape[2:], lambda b, i: (b, block(i), 0, 0))],
        out_specs=spec,
        out_shape=jax.ShapeDtypeStruct((bsz, n_heads, n_tok, n), F32),
        scratch_shapes=[pltpu.VMEM((rows * n_heads, n, n), F32)],
        compiler_params=_params("parallel", "arbitrary"),
        name="rwkv_chunk_scan_bwd" if reverse else "rwkv_chunk_scan_fwd",
    )(*streams, v, g_tot)


def _rwkv_post_kernel(yf_ref, yb_ref, bonus_ref, gate_ref, lw_ref, lb_ref, o_ref):
    for h in range(N_HEADS):
        sl = slice(h * HEAD_DIM, (h + 1) * HEAD_DIM)
        y = yf_ref[0, h] + yb_ref[0, h]
        mu = jnp.mean(y, axis=-1, keepdims=True)
        var = jnp.mean(jnp.square(y - mu), axis=-1, keepdims=True)
        yn = (y - mu) * lax.rsqrt(var + GN_EPS)
        yn = yn * lw_ref[:, sl] + lb_ref[:, sl] + bonus_ref[0, :, sl]
        o_ref[0, :, sl] = (yn * gate_ref[0, :, sl]).astype(o_ref.dtype)


def rwkv_finish(y_f, y_b, bonus, gate, lnx_w, lnx_b):
    bsz, n_heads, n_tok, n = y_f.shape
    tm = TOKEN_TILE
    heads = pl.BlockSpec((1, n_heads, tm, n), lambda b, i: (b, 0, i, 0))
    tokens = pl.BlockSpec((1, tm, n_heads * n), lambda b, i: (b, i, 0))
    vec = pl.BlockSpec((1, n_heads * n), lambda b, i: (0, 0))
    return pl.pallas_call(
        _rwkv_post_kernel,
        grid=(bsz, n_tok // tm),
        in_specs=[heads, heads, tokens, tokens, vec, vec],
        out_specs=tokens,
        out_shape=jax.ShapeDtypeStruct((bsz, n_tok, n_heads * n), BF16),
        compiler_params=_params("parallel", "parallel"),
        name="rwkv_finish",
    )(y_f, y_b, bonus, gate, lnx_w, lnx_b)


def _rope_tables(n_ctx, n_lat, rot_dim, n_rep):
    t = jnp.arange(n_lat, dtype=jnp.int32)
    rows = (t // GRID_W).astype(F32)
    cols = (t % GRID_W).astype(F32)
    per_axis = rot_dim // 2
    inv_freq = ROPE_THETA ** (-jnp.arange(0, per_axis, 2, dtype=F32) / per_axis)
    ang = jnp.concatenate([rows[:, None] * inv_freq, cols[:, None] * inv_freq], axis=-1)
    ang = jnp.concatenate([jnp.zeros((n_ctx, ang.shape[1]), F32), ang], axis=0)
    sign = jnp.tile(jnp.array([-1.0, 1.0], F32), rot_dim // 2)
    cos = jnp.repeat(jnp.cos(ang), 2, axis=-1)
    sin = jnp.repeat(jnp.sin(ang), 2, axis=-1) * sign
    return jnp.tile(cos, (1, n_rep)), jnp.tile(sin, (1, n_rep))


def _pairs(t):
    return t.reshape((t.shape[0], t.shape[1] // 2, 2) + t.shape[2:])


def _block_diag(m0, m1):
    z01 = jnp.zeros((m0.shape[0], m1.shape[1]), m0.dtype)
    z10 = jnp.zeros((m1.shape[0], m0.shape[1]), m0.dtype)
    return jnp.concatenate([jnp.concatenate([m0, z01], axis=1), jnp.concatenate([z10, m1], axis=1)], axis=0)


def _shift_halo(edge, ctx_tiles):
    zero = jnp.zeros_like(edge[:, :1, 0])
    prev = jnp.concatenate([zero, edge[:, :-1, 1]], axis=1)
    nxt = jnp.concatenate([edge[:, 1:, 0], zero], axis=1)
    tile = jnp.arange(prev.shape[1])[None, :, None]
    prev = jnp.where(tile == ctx_tiles, 0.0, prev)
    nxt = jnp.where(tile == ctx_tiles - 1, 0.0, nxt)
    return jnp.stack([prev, nxt], axis=2)


def _rwkv_mixer(rw, rw_edge, ctx_tiles, shift_taps, w0, w2, a0, a2, k_k, k_a, r_k, g2, lnx_w, lnx_b):
    both = lambda t: t.reshape(1, -1)
    outs = rwkv_prepare(rw, _shift_halo(rw_edge, ctx_tiles), shift_taps, both(w0),
                        _block_diag(w2[0], w2[1]).astype(BF16), both(a0),
                        _block_diag(a2[0], a2[1]).astype(BF16), both(k_k), both(k_a), both(r_k),
                        g2.astype(BF16))
    v, bonus, gate = outs[8], outs[11], outs[12]
    y_f = rwkv_chunk_scan(outs[0:4], v, outs[9], ctx_tiles, False)
    y_b = rwkv_chunk_scan(outs[4:8], v, outs[10], ctx_tiles, True)
    return rwkv_finish(y_f, y_b, bonus, gate, lnx_w[None], lnx_b[None])


def _dense_branch(q, k, v, n_ctx, want_ctx):
    n_tok = q.shape[3]
    ctx_tiles = n_ctx // ATTN_Q_TILE
    o_lat = attention(q, k, v, ctx_tiles, (n_tok - n_ctx) // ATTN_Q_TILE, n_tok, HEAD_DIM)
    o_ctx = attention(q, k, v, 0, ctx_tiles, n_ctx, HEAD_DIM) if want_ctx else None
    return o_lat, o_ctx


def kernel(x, c, ctx, c_ctx, w_mod, b_mod, norm1_g, norm2_g, w_in, rwkv_shift, na_rpb, gqa_q_norm, gqa_k_norm, mla_q_norm, mla_kv_norm, mla_w_uq, mla_w_ukv, rwkv_w0, rwkv_w2, rwkv_a0, rwkv_a2, rwkv_k_k, rwkv_k_a, rwkv_r_k, rwkv_g2, rwkv_lnx_w, rwkv_lnx_b, w_out, w_fc1, w_fc2, final_norm_g):
    bsz, n_lat, d_model = x.shape
    n_ctx = ctx.shape[1]
    depth = w_mod.shape[0]
    rows = n_lat // GRID_W
    ctx_tiles = n_ctx // TOKEN_TILE
    assert n_ctx % TOKEN_TILE == 0 and n_lat % TOKEN_TILE == 0 and rows % NA_Q_ROWS == 0

    tables = (_rope_tables(n_ctx, n_lat, HEAD_DIM, N_HEADS) + _rope_tables(n_ctx, n_lat, MLA_ROPE_DIM, N_HEADS))

    cond = jnp.concatenate([jax.nn.silu(c), jax.nn.silu(c_ctx)[None]], axis=0)
    n_cond = -(-cond.shape[0] // 8) * 8
    cond = jnp.pad(cond, ((0, n_cond - cond.shape[0]), (0, 0)))
    mod_all = modulation(cond, w_mod, b_mod[:, None, :])

    x_all = jnp.concatenate([ctx, x], axis=1)
    bounds = np.cumsum([0, NA_COLS, GQA_COLS, MLA_COLS, RWKV_COLS])
    mla_pad = -MLA_COLS % 128

    for layer in range(depth):
        want_ctx = layer < depth - 1
        mod = mod_all[layer].reshape(n_cond, 6, d_model)
        mod = jnp.stack([jnp.broadcast_to(mod[bsz], (bsz, 6, d_model)), mod[:bsz]], axis=1)

        w_na, w_gqa, w_mla, w_rw = [w_in[layer][:, bounds[i]:bounds[i + 1]].astype(BF16) for i in range(4)]
        w_uq = mla_w_uq[layer].reshape(MLA_Q_RANK, N_HEADS, MLA_QK_DIM)
        w_uq = jnp.concatenate([w_uq[:, :, :HEAD_DIM].reshape(MLA_Q_RANK, -1),
                                w_uq[:, :, HEAD_DIM:].reshape(MLA_Q_RANK, -1)], axis=1)
        w_ukv = mla_w_ukv[layer].reshape(MLA_KV_RANK, N_HEADS, 2 * HEAD_DIM)
        w_ukv = jnp.concatenate([w_ukv[:, :, :HEAD_DIM].reshape(MLA_KV_RANK, -1),
                                 w_ukv[:, :, HEAD_DIM:].reshape(MLA_KV_RANK, -1)], axis=1)
        weights = (w_na, w_gqa, jnp.pad(w_mla, ((0, 0), (0, mla_pad))), w_rw,
                   jnp.tile(gqa_q_norm[layer], N_HEADS)[None], jnp.tile(gqa_k_norm[layer], GQA_KV_HEADS)[None],
                   mla_q_norm[layer][None], mla_kv_norm[layer][None], w_uq.astype(BF16), w_ukv.astype(BF16))
        (a_q, a_k, a_v, b_q, b_k, b_v, c_q, c_k, c_v, rw, rw_edge) = in_projection(
            x_all, mod, norm1_g[layer][None], weights, tables, ctx_tiles)

        a_q, a_k, a_v = _pairs(a_q), _pairs(a_k), _pairs(a_v)
        bias = na_bias_table(na_rpb[layer], rows)
        out_a = neighbourhood_attention(a_q, a_k, a_v, bias.reshape((N_HEADS // 2, 2) + bias.shape[1:]), n_ctx)
        out_ac = attention(a_q, a_k, a_v, 0, ctx_tiles, n_ctx, HEAD_DIM) if want_ctx else None
        mix_b = _dense_branch(_pairs(b_q), b_k[:, :, None], b_v[:, :, None], n_ctx, want_ctx)
        mix_c = _dense_branch(_pairs(c_q), _pairs(c_k), _pairs(c_v), n_ctx, want_ctx)
        mix_d = _rwkv_mixer(rw, rw_edge, ctx_tiles, rwkv_shift[layer], rwkv_w0[layer], rwkv_w2[layer], rwkv_a0[layer],
                            rwkv_a2[layer], rwkv_k_k[layer], rwkv_k_a[layer], rwkv_r_k[layer],
                            rwkv_g2[layer], rwkv_lnx_w[layer], rwkv_lnx_b[layer])

        x_all = out_projection_mlp(x_all, [(out_a, out_ac), mix_b, mix_c], mix_d, mod, norm2_g[layer][None],
                                   final_norm_g[None], w_out[layer].astype(BF16), w_fc1[layer].astype(BF16),
                                   w_fc2[layer].astype(BF16), ctx_tiles, not want_ctx)
    return x_all
```

```python
import functools
import math

import numpy as np
import jax
import jax.numpy as jnp
from jax import lax
from jax.experimental import pallas as pl
from jax.experimental.pallas import tpu as pltpu

F32 = jnp.float32
BF16 = jnp.bfloat16

GRID_W = 64
HEAD_DIM = 64
N_HEADS = 4
GROUP_WIDTH = N_HEADS * HEAD_DIM
GQA_KV_HEADS = 2
NA_WIN_ROWS = 8
NA_WIN_COLS = 16
MLA_Q_RANK = 256
MLA_KV_RANK = 128
MLA_ROPE_DIM = 32
MLA_QK_DIM = HEAD_DIM + MLA_ROPE_DIM
RWKV_DECAY_RANK = 64
RWKV_ICLR_RANK = 64
RWKV_GATE_RANK = 160
NA_COLS = 3 * GROUP_WIDTH
GQA_COLS = (N_HEADS + 2 * GQA_KV_HEADS) * HEAD_DIM
MLA_COLS = MLA_Q_RANK + MLA_KV_RANK + MLA_ROPE_DIM
RWKV_COLS = 3 * GROUP_WIDTH + 2 * RWKV_DECAY_RANK + 2 * RWKV_ICLR_RANK + RWKV_GATE_RANK
ROPE_THETA = 10000.0
RMS_EPS = 1e-6
GN_EPS = 64e-5
NEG_INF = -1e30

TOKEN_TILE = 256
ATTN_Q_TILE = 256
ATTN_KV_CHUNK = 1024
V_EXT_WIDTH = 128
LOG2E = float(np.log2(np.e))
NA_Q_ROWS = ATTN_Q_TILE // GRID_W
NA_KEY_ROWS = NA_Q_ROWS + NA_WIN_ROWS - 1
RWKV_CHUNK = 64
RWKV_ROWS_PER_STEP = 8
VMEM_LIMIT = 56 * 1024 * 1024


def _params(*sem):
    return pltpu.CompilerParams(dimension_semantics=sem, vmem_limit_bytes=VMEM_LIMIT)


def _bdot(a, b):
    return jnp.dot(a.astype(BF16), b.astype(BF16), preferred_element_type=F32)


def _bdot_nt(a, b):
    return lax.dot_general(a.astype(BF16), b.astype(BF16), (((1,), (1,)), ((), ())),
                           preferred_element_type=F32)


def _mod_kernel(a_ref, w_ref, b_ref, o_ref):
    o_ref[0] = _bdot(a_ref[...], w_ref[0]) + b_ref[0]


def modulation(cond, w_mod, b_mod, tn=1536):
    depth, d, n = w_mod.shape
    r = cond.shape[0]
    return pl.pallas_call(
        _mod_kernel,
        grid=(depth, n // tn),
        in_specs=[pl.BlockSpec((r, d), lambda l, j: (0, 0)),
                  pl.BlockSpec((1, d, tn), lambda l, j: (l, 0, j)),
                  pl.BlockSpec((1, 1, tn), lambda l, j: (l, 0, j))],
        out_specs=pl.BlockSpec((1, r, tn), lambda l, j: (l, 0, j)),
        out_shape=jax.ShapeDtypeStruct((depth, r, n), F32),
        compiler_params=_params("parallel", "parallel"),
        name="modulation",
    )(cond, w_mod, b_mod)


def _rms(x, g):
    return x * lax.rsqrt(jnp.mean(x * x, axis=-1, keepdims=True) + RMS_EPS) * g


def _split3(x):
    hi = x.astype(BF16)
    rest = x - hi.astype(F32)
    mid = rest.astype(BF16)
    return hi, mid, (rest - mid.astype(F32)).astype(BF16)


def _select_rows(sel, x):
    sel = sel.astype(BF16)
    hi, mid, lo = _split3(x)
    dot = lambda part: jnp.dot(sel, part, preferred_element_type=F32)
    return dot(lo) + dot(mid) + dot(hi)


def _select_cols(x, sel):
    sel = sel.astype(BF16)
    hi, mid, lo = _split3(x)
    dot = lambda part: jnp.dot(part, sel, preferred_element_type=F32)
    return dot(lo) + dot(mid) + dot(hi)


def _pair_swap(x):
    axis = x.ndim - 1
    lane = lax.broadcasted_iota(jnp.int32, x.shape, axis)
    return jnp.where((lane & 1) == 0, pltpu.roll(x, x.shape[axis] - 1, axis=axis), pltpu.roll(x, 1, axis=axis))


def _rotate(x, cos, sin):
    return x * cos + _pair_swap(x) * sin


def _in_proj_kernel(x_ref, mod_ref, g_ref, wna_ref, wgqa_ref, wmla_ref, wrw_ref, gq_ref, gk_ref,
                    mq_ref, mkv_ref, wuq_ref, wukv_ref, cosh_ref, sinh_ref, cosm_ref, sinm_ref,
                    naq_ref, nak_ref, nav_ref, bq_ref, bk_ref, bv_ref, cq_ref, ck_ref, cv_ref, rw_ref, edge_ref):
    hd, gw = HEAD_DIM, GROUP_WIDTH
    mod = mod_ref[0, 0]
    hb = (_rms(x_ref[0], g_ref[...]) * (1.0 + mod[1:2]) + mod[0:1]).astype(BF16)
    tm = hb.shape[0]
    ones_col = (lax.broadcasted_iota(jnp.int32, (tm, V_EXT_WIDTH - hd), 1) == 0).astype(F32)
    qk_scale = LOG2E * hd ** -0.5

    def put_heads(ref, val, n_heads, width):
        for h in range(n_heads):
            ref[0, h] = val[:, h * width:(h + 1) * width].astype(ref.dtype)

    def put_values(ref, val, n_heads):
        for h in range(n_heads):
            ref[0, h] = jnp.concatenate([val[:, h * hd:(h + 1) * hd], ones_col], axis=-1).astype(ref.dtype)

    na = jnp.dot(hb, wna_ref[...], preferred_element_type=F32)
    put_heads(naq_ref, na[:, :gw] * qk_scale, N_HEADS, hd)
    put_heads(nak_ref, na[:, gw:2 * gw], N_HEADS, hd)
    put_values(nav_ref, na[:, 2 * gw:], N_HEADS)

    gqa = jnp.dot(hb, wgqa_ref[...], preferred_element_type=F32)
    row = lax.broadcasted_iota(jnp.int32, (gw, gw), 0)
    col = lax.broadcasted_iota(jnp.int32, (gw, gw), 1)
    shift = hd.bit_length() - 1
    head_sum = ((row >> shift) == (col >> shift)).astype(F32)
    cos_h, sin_h = cosh_ref[...], sinh_ref[...]
    kw = GQA_KV_HEADS * hd
    q = gqa[:, :gw]
    q = q * lax.rsqrt(_select_cols(q * q, head_sum) * (1.0 / hd) + RMS_EPS) * gq_ref[...]
    put_heads(bq_ref, _rotate(q, cos_h, sin_h) * qk_scale, N_HEADS, hd)
    k = gqa[:, gw:gw + kw]
    k = k * lax.rsqrt(_select_cols(k * k, head_sum[:kw, :kw]) * (1.0 / hd) + RMS_EPS) * gk_ref[...]
    put_heads(bk_ref, _rotate(k, cos_h[:, :kw], sin_h[:, :kw]), GQA_KV_HEADS, hd)
    put_values(bv_ref, gqa[:, gw + kw:], GQA_KV_HEADS)

    mla = jnp.dot(hb, wmla_ref[...], preferred_element_type=F32)
    cos_m, sin_m = cosm_ref[...], sinm_ref[...]
    mla_scale = LOG2E * MLA_QK_DIM ** -0.5
    uq = jnp.dot(_rms(mla[:, :MLA_Q_RANK], mq_ref[...]).astype(BF16), wuq_ref[...], preferred_element_type=F32)
    q_nope = uq[:, :gw] * mla_scale
    q_rope = _rotate(uq[:, gw:], cos_m, sin_m) * mla_scale
    lo = MLA_Q_RANK + MLA_KV_RANK
    ukv = jnp.dot(_rms(mla[:, MLA_Q_RANK:lo], mkv_ref[...]).astype(BF16), wukv_ref[...],
                  preferred_element_type=F32)
    k_rope = _rotate(mla[:, lo:], cos_m, sin_m)[:, :MLA_ROPE_DIM]
    rd = MLA_ROPE_DIM
    for h in range(N_HEADS):
        cq_ref[0, h] = jnp.concatenate([q_nope[:, h * hd:(h + 1) * hd], q_rope[:, h * rd:(h + 1) * rd]],
                                       axis=-1).astype(cq_ref.dtype)
        ck_ref[0, h] = jnp.concatenate([ukv[:, h * hd:(h + 1) * hd], k_rope], axis=-1).astype(ck_ref.dtype)
    put_values(cv_ref, ukv[:, gw:], N_HEADS)

    rw = jnp.dot(hb, wrw_ref[...], preferred_element_type=F32)
    rw_ref[0] = rw
    edge_ref[0, 0] = jnp.concatenate([rw[0:1], rw[tm - 1:tm]], axis=0)


def in_projection(x_all, mod, g, weights, tables, ctx_tiles):
    bsz, n_tok, d = x_all.shape
    tm = TOKEN_TILE
    tok = lambda b, i: (b, i, 0)
    full = lambda b, i: (0, 0)
    heads = lambda n, w: (jax.ShapeDtypeStruct((bsz, n, n_tok, w), BF16),
                          pl.BlockSpec((1, n, tm, w), lambda b, i: (b, 0, i, 0)))
    outs = [heads(N_HEADS, HEAD_DIM), heads(N_HEADS, HEAD_DIM), heads(N_HEADS, V_EXT_WIDTH),
            heads(N_HEADS, HEAD_DIM), heads(GQA_KV_HEADS, HEAD_DIM), heads(GQA_KV_HEADS, V_EXT_WIDTH),
            heads(N_HEADS, MLA_QK_DIM), heads(N_HEADS, MLA_QK_DIM), heads(N_HEADS, V_EXT_WIDTH),
            (jax.ShapeDtypeStruct((bsz, n_tok, RWKV_COLS), F32), pl.BlockSpec((1, tm, RWKV_COLS), tok)),
            (jax.ShapeDtypeStruct((bsz, n_tok // tm, 2, RWKV_COLS), F32),
             pl.BlockSpec((1, 1, 2, RWKV_COLS), lambda b, i: (b, i, 0, 0)))]
    return pl.pallas_call(
        _in_proj_kernel,
        grid=(bsz, n_tok // tm),
        in_specs=[pl.BlockSpec((1, tm, d), tok),
                  pl.BlockSpec((1, 1, 6, d), lambda b, i: (b, jnp.where(i < ctx_tiles, 0, 1), 0, 0)),
                  pl.BlockSpec((1, d), full)]
                 + [pl.BlockSpec(w.shape, full) for w in weights]
                 + [pl.BlockSpec((tm, t.shape[1]), lambda b, i: (i, 0)) for t in tables],
        out_specs=[o[1] for o in outs],
        out_shape=[o[0] for o in outs],
        compiler_params=_params("parallel", "parallel"),
        name="in_projection",
    )(x_all, mod, g, *weights, *tables)


def _out_mlp_kernel(*refs, hidden_tile, final_norm, ctx_tiles, n_branches):
    x_ref = refs[0]
    per_branch = 2 if ctx_tiles else 1
    branch_refs = refs[1:1 + per_branch * n_branches]
    mixd_ref, mod_ref, g2_ref, gf_ref, wout_ref, w1_ref, w2_ref, o_ref = refs[1 + per_branch * n_branches:]
    parts = []
    for j in range(n_branches):
        lat = branch_refs[per_branch * j][0]
        if ctx_tiles:
            lat = jnp.where(pl.program_id(1) < ctx_tiles, branch_refs[per_branch * j + 1][0], lat)
        parts.append(lat)
    parts.append(mixd_ref[0])
    mod = mod_ref[0, 0]
    proj = jnp.zeros(x_ref.shape[1:], F32)
    lo = 0
    for part in parts:
        proj = proj + jnp.dot(part, wout_ref[lo:lo + part.shape[1], :], preferred_element_type=F32)
        lo += part.shape[1]
    x1 = x_ref[0] + mod[2:3] * proj
    hb = (_rms(x1, g2_ref[...]) * (1.0 + mod[4:5]) + mod[3:4]).astype(BF16)
    acc = jnp.zeros_like(x1)
    for j in range(w1_ref.shape[1] // hidden_tile):
        cols = pl.ds(j * hidden_tile, hidden_tile)
        u = jnp.maximum(jnp.dot(hb, w1_ref[:, cols], preferred_element_type=F32), 0.0)
        acc = acc + jnp.dot((u * u).astype(BF16), w2_ref[cols, :], preferred_element_type=F32)
    x2 = x1 + mod[5:6] * acc
    if final_norm:
        x2 = _rms(x2, gf_ref[...])
    o_ref[0] = x2


def out_projection_mlp(x_all, branches, mix_d, mod, g2, g_final, w_out, w_fc1, w_fc2, ctx_tiles, last):
    bsz, n_tok, d = x_all.shape
    tm = TOKEN_TILE
    n_tiles = n_tok // tm
    skip = ctx_tiles if last else 0
    tok = lambda b, i: (b, i + skip, 0)
    full = lambda b, i: (0, 0)
    lat_map = lambda b, i: (b, jnp.maximum(i + skip - ctx_tiles, 0), 0)
    ctx_map = lambda b, i: (b, jnp.minimum(i, ctx_tiles - 1), 0)
    branch_args, branch_specs = [], []
    for lat, ctx_rows in branches:
        branch_args.append(lat)
        branch_specs.append(pl.BlockSpec((1, tm, lat.shape[2]), lat_map))
        if not last:
            branch_args.append(ctx_rows)
            branch_specs.append(pl.BlockSpec((1, tm, ctx_rows.shape[2]), ctx_map))
    kern = functools.partial(_out_mlp_kernel, hidden_tile=1024, final_norm=last,
                             ctx_tiles=0 if last else ctx_tiles, n_branches=len(branches))
    return pl.pallas_call(
        kern,
        grid=(bsz, n_tiles - skip),
        in_specs=[pl.BlockSpec((1, tm, d), tok)] + branch_specs
                 + [pl.BlockSpec((1, tm, mix_d.shape[2]), tok),
                    pl.BlockSpec((1, 1, 6, d), lambda b, i: (b, jnp.where(i + skip < ctx_tiles, 0, 1), 0, 0)),
                    pl.BlockSpec((1, d), full),
                    pl.BlockSpec((1, d), full),
                    pl.BlockSpec(w_out.shape, full, pipeline_mode=pl.Buffered(1)),
                    pl.BlockSpec(w_fc1.shape, full, pipeline_mode=pl.Buffered(1)),
                    pl.BlockSpec(w_fc2.shape, full, pipeline_mode=pl.Buffered(1))],
        out_specs=pl.BlockSpec((1, tm, d), lambda b, i: (b, i, 0)),
        out_shape=jax.ShapeDtypeStruct((bsz, n_tok - skip * tm, d), F32),
        compiler_params=_params("parallel", "parallel"),
        name="out_projection_mlp",
    )(x_all, *branch_args, mix_d, mod, g2, g_final, w_out, w_fc1, w_fc2)


def _key_chunks(n_keys):
    first = min(n_keys, ATTN_Q_TILE)
    chunks = [(0, first)]
    while chunks[-1][0] + chunks[-1][1] < n_keys:
        start = chunks[-1][0] + chunks[-1][1]
        chunks.append((start, min(ATTN_KV_CHUNK, n_keys - start)))
    return chunks


def _online_softmax(q, k_at, v_at, chunks):
    rows = q.shape[0]
    scores = lambda c: _bdot_nt(q, k_at(*c))
    m = jnp.full((rows, 1), NEG_INF, F32)
    acc = jnp.zeros((rows, V_EXT_WIDTH), F32)
    s_next = scores(chunks[0])
    for j, chunk in enumerate(chunks):
        s = s_next
        if j + 1 < len(chunks):
            s_next = scores(chunks[j + 1])
        m_new = jnp.maximum(m, jnp.max(s, axis=-1, keepdims=True))
        p = jnp.exp2(s - m_new)
        acc = acc * jnp.exp2(m - m_new) + jnp.dot(p.astype(BF16), v_at(*chunk), preferred_element_type=F32)
        m = m_new
    return acc


def _attention_kernel(q_ref, k_ref, v_ref, o_ref):
    n_q, tq, dk = q_ref.shape[2], q_ref.shape[3], q_ref.shape[4]
    dv = o_ref.shape[-1] // n_q
    chunks = _key_chunks(k_ref.shape[3])
    if k_ref.shape[2] == 1:
        groups = [(q_ref[0, 0].reshape(n_q * tq, dk), 0)]
    else:
        groups = [(q_ref[0, 0, g], g) for g in range(n_q)]
    outs = []
    for q, g in groups:
        acc = _online_softmax(q, lambda s, n: k_ref[0, 0, g, pl.ds(s, n), :],
                              lambda s, n: v_ref[0, 0, g, pl.ds(s, n), :], chunks)
        o = acc[:, :dv] / acc[:, dv:dv + 1]
        outs.extend(o[i * tq:(i + 1) * tq] for i in range(o.shape[0] // tq))
    o_ref[0] = jnp.concatenate(outs, axis=-1).astype(o_ref.dtype)


def attention(q, k, v, q_tile0, n_q_tiles, n_keys, dv):
    bsz, n_pairs, n_q, _, dk = q.shape
    n_kv = k.shape[2]
    tq = ATTN_Q_TILE
    return pl.pallas_call(
        _attention_kernel,
        grid=(bsz, n_pairs, n_q_tiles),
        in_specs=[pl.BlockSpec((1, 1, n_q, tq, dk), lambda b, h, i: (b, h, 0, i + q_tile0, 0)),
                  pl.BlockSpec((1, 1, n_kv, n_keys, dk), lambda b, h, i: (b, h, 0, 0, 0)),
                  pl.BlockSpec((1, 1, n_kv, n_keys, v.shape[-1]), lambda b, h, i: (b, h, 0, 0, 0))],
        out_specs=pl.BlockSpec((1, tq, n_q * dv), lambda b, h, i: (b, i, h)),
        out_shape=jax.ShapeDtypeStruct((bsz, n_q_tiles * tq, n_pairs * n_q * dv), BF16),
        compiler_params=_params("parallel", "parallel", "parallel"),
        name="attention",
    )(q, k, v)


def _na_kernel(q_ref, k_ref, v_ref, bias_ref, o_ref, *, n_ctx, n_row_tiles):
    i = pl.program_id(2)
    n_win = NA_KEY_ROWS * GRID_W
    max_row0 = n_row_tiles * NA_Q_ROWS - NA_KEY_ROWS
    row0 = jnp.clip(i * NA_Q_ROWS - NA_WIN_ROWS // 2, 0, max_row0)
    start = pl.multiple_of(n_ctx + row0 * GRID_W, GRID_W)
    dv = o_ref.shape[-1] // q_ref.shape[2]
    outs = []
    for g in range(q_ref.shape[2]):
        q = q_ref[0, 0, g]
        s_win = _bdot_nt(q, k_ref[0, 0, g, pl.ds(start, n_win), :]) + bias_ref[0, g, 0]
        s_ctx = _bdot_nt(q, k_ref[0, 0, g, pl.ds(0, n_ctx), :])
        m = jnp.maximum(jnp.max(s_win, axis=-1, keepdims=True), jnp.max(s_ctx, axis=-1, keepdims=True))
        p_win = jnp.exp2(s_win - m).astype(BF16)
        p_ctx = jnp.exp2(s_ctx - m).astype(BF16)
        o = (jnp.dot(p_win, v_ref[0, 0, g, pl.ds(start, n_win), :], preferred_element_type=F32)
             + jnp.dot(p_ctx, v_ref[0, 0, g, pl.ds(0, n_ctx), :], preferred_element_type=F32))
        outs.append(o[:, :dv] / o[:, dv:dv + 1])
    o_ref[0] = jnp.concatenate(outs, axis=-1).astype(o_ref.dtype)


def na_bias_table(rpb, rows):
    n_tiles = rows // NA_Q_ROWS
    n_heads, n_dr, n_dc = rpb.shape
    qc = np.arange(GRID_W)[:, None]
    kc = np.arange(GRID_W)[None, :]
    dc = np.clip(kc - qc + NA_WIN_COLS - 1, 0, n_dc - 1)
    onehot = jnp.asarray(dc.reshape(-1)[None, :] == np.arange(n_dc)[:, None], F32)
    by_col = jnp.dot(rpb.reshape(-1, n_dc).astype(F32), onehot, precision=lax.Precision.HIGHEST)
    pad = NA_KEY_ROWS
    by_col = jnp.pad(by_col.reshape(n_heads, n_dr, GRID_W, GRID_W), ((0, 0), (pad, pad), (0, 0), (0, 0)))
    c_start = np.clip(qc - NA_WIN_COLS // 2, 0, GRID_W - NA_WIN_COLS)
    col_ok = (kc >= c_start) & (kc < c_start + NA_WIN_COLS)
    tables = []
    for tile in (0, 1, n_tiles - 1):
        row0 = int(np.clip(tile * NA_Q_ROWS - NA_WIN_ROWS // 2, 0, rows - NA_KEY_ROWS))
        per_row = []
        for a in range(NA_Q_ROWS):
            qr = tile * NA_Q_ROWS + a
            dr0 = row0 - qr + NA_WIN_ROWS - 1 + pad
            kr = row0 + np.arange(NA_KEY_ROWS)
            r_start = int(np.clip(qr - NA_WIN_ROWS // 2, 0, rows - NA_WIN_ROWS))
            row_ok = (kr >= r_start) & (kr < r_start + NA_WIN_ROWS)
            valid = row_ok[None, :, None] & col_ok[:, None, :]
            sl = jnp.swapaxes(by_col[:, dr0:dr0 + NA_KEY_ROWS], 1, 2)
            per_row.append(jnp.where(valid[None], sl * LOG2E, NEG_INF))
        bias = jnp.stack(per_row, axis=1)
        tables.append(bias.reshape(n_heads, NA_Q_ROWS * GRID_W, NA_KEY_ROWS * GRID_W))
    return jnp.stack(tables, axis=1)


def neighbourhood_attention(q, k, v, bias, n_ctx):
    bsz, n_pairs, n_q, n_tok, d = q.shape
    tq = ATTN_Q_TILE
    n_tiles = (n_tok - n_ctx) // tq
    ctx_tiles = n_ctx // tq
    kern = functools.partial(_na_kernel, n_ctx=n_ctx, n_row_tiles=n_tiles)

    def bias_map(b, h, i):
        return (h, 0, jnp.where(i == 0, 0, jnp.where(i == n_tiles - 1, 2, 1)), 0, 0)

    whole = lambda b, h, i: (b, h, 0, 0, 0)
    return pl.pallas_call(
        kern,
        grid=(bsz, n_pairs, n_tiles),
        in_specs=[pl.BlockSpec((1, 1, n_q, tq, d), lambda b, h, i: (b, h, 0, i + ctx_tiles, 0)),
                  pl.BlockSpec((1, 1, n_q, n_tok, d), whole),
                  pl.BlockSpec((1, 1, n_q, n_tok, v.shape[-1]), whole),
                  pl.BlockSpec((1, n_q, 1) + bias.shape[3:], bias_map)],
        out_specs=pl.BlockSpec((1, tq, n_q * d), lambda b, h, i: (b, i, h)),
        out_shape=jax.ShapeDtypeStruct((bsz, n_tok - n_ctx, n_pairs * n_q * d), BF16),
        compiler_params=_params("parallel", "parallel", "parallel"),
        name="neighbourhood_attention",
    )(q, k, v, bias)


def _softplus(z):
    return jnp.maximum(z, 0.0) + jnp.log(1.0 + jnp.exp(-jnp.abs(z)))


def _sigmoid(z):
    return 1.0 / (1.0 + jnp.exp(-z))


def _rwkv_prep_kernel(u_ref, halo_ref, taps_ref, w0_ref, w2_ref, a0_ref, a2_ref, kk_ref, ka_ref, rk_ref,
                      g2_ref, *out_refs):
    dir_refs = (out_refs[0:4], out_refs[4:8])
    v_out, gt_refs, bonus_ref, gate_ref = out_refs[8], out_refs[9:11], out_refs[11], out_refs[12]
    tm = u_ref.shape[1]
    c = RWKV_CHUNK
    n_chunks = tm // c
    gw = GROUP_WIDTH
    u = u_ref[0]
    halo = halo_ref[0, 0]
    tok = lax.broadcasted_iota(jnp.int32, u.shape, 0)
    prev = jnp.where(tok == 0, halo[0:1], pltpu.roll(u, 1, axis=0))
    nxt = jnp.where(tok == tm - 1, halo[1:2], pltpu.roll(u, tm - 1, axis=0))
    taps = taps_ref[...]
    s = prev * taps[0:1] + u * taps[1:2] + nxt * taps[2:3]

    r, k, v = s[:, 0:gw], s[:, gw:2 * gw], s[:, 2 * gw:3 * gw]
    lo = 3 * gw
    w_low = jnp.tanh(s[:, lo:lo + 2 * RWKV_DECAY_RANK])
    lo += 2 * RWKV_DECAY_RANK
    a_low = s[:, lo:lo + 2 * RWKV_ICLR_RANK]
    lo += 2 * RWKV_ICLR_RANK
    g_low = _sigmoid(s[:, lo:lo + RWKV_GATE_RANK])
    w_log = -_softplus(-(w0_ref[...] + _bdot(w_low, w2_ref[...]))) - 0.5
    log_decay = -jnp.exp(w_log)
    iclr = _sigmoid(a0_ref[...] + _bdot(a_low, a2_ref[...]))
    gate_ref[0] = _bdot(g_low, g2_ref[...])

    row = lax.broadcasted_iota(jnp.int32, (tm, tm), 0)
    col = lax.broadcasted_iota(jnp.int32, (tm, tm), 1)
    shift = c.bit_length() - 1
    same = (row >> shift) == (col >> shift)
    head_shift = HEAD_DIM.bit_length() - 1
    ones_blk = ((row >> head_shift) == (col >> head_shift)).astype(F32)

    v_out[0] = v.astype(v_out.dtype)
    bonus = jnp.zeros_like(r)
    for d in range(2):
        sl = slice(d * gw, (d + 1) * gw)
        lw = log_decay[:, sl]
        kk = k * kk_ref[:, sl]
        kk = kk * lax.rsqrt(jnp.maximum(_select_cols(kk * kk, ones_blk), 1e-24))
        k_d = k * (1.0 + (iclr[:, sl] - 1.0) * ka_ref[:, sl])
        b_vec = kk * iclr[:, sl]
        tri = (same & ((row >= col) if d == 0 else (row <= col))).astype(F32)
        cum = _select_rows(tri, lw)
        ends = [cum[i * c + (c - 1 if d == 0 else 0)][None] for i in range(n_chunks)]
        e_inv = jnp.exp(-cum)
        streams = (r * jnp.exp(cum), -kk * jnp.exp(cum - lw), b_vec * e_inv, k_d * e_inv)
        for ref, val in zip(dir_refs[d], streams):
            ref[0] = val.astype(ref.dtype)
        gt_refs[d][0, 0] = jnp.exp(jnp.concatenate(ends, axis=0))
        bonus = bonus + r * k_d * rk_ref[:, sl]
    bonus_ref[0] = _select_cols(bonus, ones_blk) * v


def rwkv_prepare(rw, halo, taps, w0, w2, a0, a2, k_k, k_a, r_k, g2):
    bsz, n_tok, width = rw.shape
    tm = TOKEN_TILE
    n_tiles = n_tok // tm
    n_chunks = tm // RWKV_CHUNK
    full = lambda b, i: (0, 0)
    consts = (taps, w0, w2, a0, a2, k_k, k_a, r_k, g2)
    assert tm == GROUP_WIDTH
    tokens_spec = pl.BlockSpec((1, tm, GROUP_WIDTH), lambda b, i: (b, i, 0))
    stream = jax.ShapeDtypeStruct((bsz, n_tok, GROUP_WIDTH), BF16)
    stream_spec = tokens_spec
    decay = jax.ShapeDtypeStruct((bsz, n_tiles, n_chunks, GROUP_WIDTH), F32)
    decay_spec = pl.BlockSpec((1, 1, n_chunks, GROUP_WIDTH), lambda b, i: (b, i, 0, 0))
    tokens = jax.ShapeDtypeStruct((bsz, n_tok, GROUP_WIDTH), F32)
    return pl.pallas_call(
        _rwkv_prep_kernel,
        grid=(bsz, n_tiles),
        in_specs=[pl.BlockSpec((1, tm, width), lambda b, i: (b, i, 0)),
                  pl.BlockSpec((1, 1, 2, width), lambda b, i: (b, i, 0, 0))]
                 + [pl.BlockSpec(t.shape, full) for t in consts],
        out_specs=[stream_spec] * 9 + [decay_spec] * 2 + [tokens_spec] * 2,
        out_shape=[stream] * 9 + [decay] * 2 + [tokens] * 2,
        compiler_params=_params("parallel", "parallel"),
        name="rwkv_prepare",
    )(rw, halo, *consts)


def _bmm(a, b):
    return lax.dot_general(a.astype(BF16), b.astype(BF16), (((2,), (1,)), ((0,), (0,))),
                           preferred_element_type=F32)


def _bmm_nt(a, b):
    return lax.dot_general(a.astype(BF16), b.astype(BF16), (((2,), (2,)), ((0,), (0,))),
                           preferred_element_type=F32)


def _bmm_tn(a, b):
    return lax.dot_general(a.astype(BF16), b.astype(BF16), (((1,), (1,)), ((0,), (0,))),
                           preferred_element_type=F32)


def _rwkv_chunk_kernel(rt_ref, at_ref, bt_ref, kt_ref, v_ref, gt_ref, y_ref, s_ref, *, reverse):
    n_rows, tb, width = v_ref.shape
    n = width // 2
    c = RWKV_CHUNK
    n_chunks = tb // c
    g = n_rows * n_chunks

    @pl.when(pl.program_id(2) == 0)
    def _():
        s_ref[...] = jnp.zeros_like(s_ref)

    load = lambda ref: ref[...].reshape(g, c, width)
    rt, at, bt, kt, v = (load(ref) for ref in (rt_ref, at_ref, bt_ref, kt_ref, v_ref))
    left = lax.broadcasted_iota(jnp.int32, (1, 1, width), 2) < n
    cat = lambda a, b: jnp.concatenate([a, b], axis=1)

    def bd(y):
        y = y.astype(BF16)
        zero = jnp.zeros_like(y)
        return cat(jnp.where(left, y, zero), jnp.where(left, zero, y))

    row = lax.broadcasted_iota(jnp.int32, (c, 2 * c), 0)
    col = lax.broadcasted_iota(jnp.int32, (c, 2 * c), 1) & (c - 1)
    if reverse:
        row, col = col, row
    strict = row > col
    incl = row >= col
    ar = cat(at, rt)
    ab_rb = _bmm_nt(ar, bd(bt))
    ak_rk = _bmm_nt(ar, bd(kt))
    a_ab = jnp.where(strict, ab_rb[:, :c], 0.0)
    a_rb = jnp.where(incl, ab_rb[:, c:], 0.0)
    a_ak = jnp.where(strict, ak_rk[:, :c], 0.0)
    a_rk = jnp.where(incl, ak_rk[:, c:], 0.0)
    v_bd = bd(v)
    av = _bmm(a_ak, v_bd)
    n1 = jnp.where((row >> 3) == (col >> 3), a_ab, 0.0)
    n2 = _bmm(n1, bd(n1))
    n2_bd = bd(n2)
    n4 = _bmm(n2, n2_bd)
    p = (row == col).astype(F32) + n1 + n2 + _bmm(n1, n2_bd)
    inv = p + _bmm(p, bd(n4))
    for level in range(3, c.bit_length() - 1):
        rb, cb = row >> level, col >> level
        off = jnp.where((rb == cb + 1) & ((rb & 1) == 1), a_ab, 0.0)
        inv = inv + _bmm(_bmm(inv, bd(off)), bd(inv))
    w = _bmm(inv, bd(at))
    u0 = _bmm(inv, bd(av))
    q_eff = rt.astype(F32) + _bmm(a_rb, bd(w))
    y0 = _bmm(jnp.concatenate([a_rb, a_rk], axis=2), cat(bd(u0), v_bd))

    per_chunk = lambda t: t.reshape((n_rows, n_chunks) + t.shape[1:])
    qw = per_chunk(cat(q_eff, w))
    u0, y0, v = per_chunk(u0), per_chunk(y0), per_chunk(v)
    g_tot = gt_ref[:, 0][:, :, None, :]
    bk_end = per_chunk(cat(bt, kt)) * g_tot
    state = s_ref[...]
    for i in (range(n_chunks - 1, -1, -1) if reverse else range(n_chunks)):
        yu = _bmm_nt(qw[:, i], bd(state))
        y_ref[:, i * c:(i + 1) * c, :] = yu[:, :c] + y0[:, i]
        full = _bmm_tn(cat(yu[:, c:] + u0[:, i], v[:, i]), bk_end[:, i])
        state = g_tot[:, i] * state + jnp.where(left, full[:, :n], full[:, n:])
    s_ref[...] = state


def rwkv_chunk_scan(streams, v, g_tot, ctx_blocks, reverse):
    bsz, n_tok, width = v.shape
    tb = TOKEN_TILE
    n_blocks = n_tok // tb
    pair = 2 * HEAD_DIM

    def block(i):
        if not reverse:
            return i
        return jnp.where(i < ctx_blocks, ctx_blocks - 1 - i, n_blocks - 1 + ctx_blocks - i)

    rows = math.gcd(RWKV_ROWS_PER_STEP, bsz)
    spec = pl.BlockSpec((rows, tb, pair), lambda b, p, i: (b, block(i), p))
    return pl.pallas_call(
        functools.partial(_rwkv_chunk_kernel, reverse=reverse),
        grid=(bsz // rows, width // pair, n_blocks),
        in_specs=[spec] * 5 + [pl.BlockSpec((rows, 1, g_tot.shape[2], pair), lambda b, p, i: (b, block(i), 0, p))],
        out_specs=spec,
        out_shape=jax.ShapeDtypeStruct((bsz, n_tok, width), F32),
        scratch_shapes=[pltpu.VMEM((rows, HEAD_DIM, pair), F32)],
        compiler_params=_params("parallel", "parallel", "arbitrary"),
        name="rwkv_chunk_scan_bwd" if reverse else "rwkv_chunk_scan_fwd",
    )(*streams, v, g_tot)


def _rwkv_post_kernel(yf_ref, yb_ref, bonus_ref, gate_ref, lw_ref, lb_ref, o_ref):
    y = yf_ref[0] + yb_ref[0]
    gw = y.shape[1]
    row = lax.broadcasted_iota(jnp.int32, (gw, gw), 0)
    col = lax.broadcasted_iota(jnp.int32, (gw, gw), 1)
    shift = HEAD_DIM.bit_length() - 1
    head_sum = ((row >> shift) == (col >> shift)).astype(F32)
    mu = _select_cols(y, head_sum) * (1.0 / HEAD_DIM)
    var = _select_cols(jnp.square(y - mu), head_sum) * (1.0 / HEAD_DIM)
    yn = (y - mu) * lax.rsqrt(var + GN_EPS)
    yn = yn * lw_ref[...] + lb_ref[...] + bonus_ref[0]
    o_ref[0] = (yn * gate_ref[0]).astype(o_ref.dtype)


def rwkv_finish(y_f, y_b, bonus, gate, lnx_w, lnx_b):
    bsz, n_tok, width = y_f.shape
    tm = TOKEN_TILE
    tokens = pl.BlockSpec((1, tm, width), lambda b, i: (b, i, 0))
    vec = pl.BlockSpec((1, width), lambda b, i: (0, 0))
    return pl.pallas_call(
        _rwkv_post_kernel,
        grid=(bsz, n_tok // tm),
        in_specs=[tokens, tokens, tokens, tokens, vec, vec],
        out_specs=tokens,
        out_shape=jax.ShapeDtypeStruct((bsz, n_tok, width), BF16),
        compiler_params=_params("parallel", "parallel"),
        name="rwkv_finish",
    )(y_f, y_b, bonus, gate, lnx_w, lnx_b)


def _rope_tables(n_ctx, n_lat, rot_dim, n_rep):
    t = jnp.arange(n_lat, dtype=jnp.int32)
    rows = (t // GRID_W).astype(F32)
    cols = (t % GRID_W).astype(F32)
    per_axis = rot_dim // 2
    inv_freq = ROPE_THETA ** (-jnp.arange(0, per_axis, 2, dtype=F32) / per_axis)
    ang = jnp.concatenate([rows[:, None] * inv_freq, cols[:, None] * inv_freq], axis=-1)
    ang = jnp.concatenate([jnp.zeros((n_ctx, ang.shape[1]), F32), ang], axis=0)
    sign = jnp.tile(jnp.array([-1.0, 1.0], F32), rot_dim // 2)
    cos = jnp.repeat(jnp.cos(ang), 2, axis=-1)
    sin = jnp.repeat(jnp.sin(ang), 2, axis=-1) * sign
    return jnp.tile(cos, (1, n_rep)), jnp.tile(sin, (1, n_rep))


def _pairs(t):
    return t.reshape((t.shape[0], t.shape[1] // 2, 2) + t.shape[2:])


def _block_diag(m0, m1):
    z01 = jnp.zeros((m0.shape[0], m1.shape[1]), m0.dtype)
    z10 = jnp.zeros((m1.shape[0], m0.shape[1]), m0.dtype)
    return jnp.concatenate([jnp.concatenate([m0, z01], axis=1), jnp.concatenate([z10, m1], axis=1)], axis=0)


def _shift_halo(edge, ctx_tiles):
    zero = jnp.zeros_like(edge[:, :1, 0])
    prev = jnp.concatenate([zero, edge[:, :-1, 1]], axis=1)
    nxt = jnp.concatenate([edge[:, 1:, 0], zero], axis=1)
    tile = jnp.arange(prev.shape[1])[None, :, None]
    prev = jnp.where(tile == ctx_tiles, 0.0, prev)
    nxt = jnp.where(tile == ctx_tiles - 1, 0.0, nxt)
    return jnp.stack([prev, nxt], axis=2)


def _rwkv_mixer(rw, rw_edge, ctx_tiles, shift_taps, w0, w2, a0, a2, k_k, k_a, r_k, g2, lnx_w, lnx_b):
    both = lambda t: t.reshape(1, -1)
    outs = rwkv_prepare(rw, _shift_halo(rw_edge, ctx_tiles), shift_taps, both(w0),
                        _block_diag(w2[0], w2[1]).astype(BF16), both(a0),
                        _block_diag(a2[0], a2[1]).astype(BF16), both(k_k), both(k_a), both(r_k),
                        g2.astype(BF16))
    v, bonus, gate = outs[8], outs[11], outs[12]
    y_f = rwkv_chunk_scan(outs[0:4], v, outs[9], ctx_tiles, False)
    y_b = rwkv_chunk_scan(outs[4:8], v, outs[10], ctx_tiles, True)
    return rwkv_finish(y_f, y_b, bonus, gate, lnx_w[None], lnx_b[None])


def _dense_branch(q, k, v, n_ctx, want_ctx):
    n_tok = q.shape[3]
    ctx_tiles = n_ctx // ATTN_Q_TILE
    o_lat = attention(q, k, v, ctx_tiles, (n_tok - n_ctx) // ATTN_Q_TILE, n_tok, HEAD_DIM)
    o_ctx = attention(q, k, v, 0, ctx_tiles, n_ctx, HEAD_DIM) if want_ctx else None
    return o_lat, o_ctx


def kernel(x, c, ctx, c_ctx, w_mod, b_mod, norm1_g, norm2_g, w_in, rwkv_shift, na_rpb, gqa_q_norm, gqa_k_norm, mla_q_norm, mla_kv_norm, mla_w_uq, mla_w_ukv, rwkv_w0, rwkv_w2, rwkv_a0, rwkv_a2, rwkv_k_k, rwkv_k_a, rwkv_r_k, rwkv_g2, rwkv_lnx_w, rwkv_lnx_b, w_out, w_fc1, w_fc2, final_norm_g):
    bsz, n_lat, d_model = x.shape
    n_ctx = ctx.shape[1]
    depth = w_mod.shape[0]
    rows = n_lat // GRID_W
    ctx_tiles = n_ctx // TOKEN_TILE
    assert n_ctx % TOKEN_TILE == 0 and n_lat % TOKEN_TILE == 0 and rows % NA_Q_ROWS == 0

    tables = (_rope_tables(n_ctx, n_lat, HEAD_DIM, N_HEADS) + _rope_tables(n_ctx, n_lat, MLA_ROPE_DIM, N_HEADS))

    cond = jnp.concatenate([jax.nn.silu(c), jax.nn.silu(c_ctx)[None]], axis=0)
    n_cond = -(-cond.shape[0] // 8) * 8
    cond = jnp.pad(cond, ((0, n_cond - cond.shape[0]), (0, 0)))
    mod_all = modulation(cond, w_mod, b_mod[:, None, :])

    x_all = jnp.concatenate([ctx, x], axis=1)
    bounds = np.cumsum([0, NA_COLS, GQA_COLS, MLA_COLS, RWKV_COLS])
    mla_pad = -MLA_COLS % 128

    for layer in range(depth):
        want_ctx = layer < depth - 1
        mod = mod_all[layer].reshape(n_cond, 6, d_model)
        mod = jnp.stack([jnp.broadcast_to(mod[bsz], (bsz, 6, d_model)), mod[:bsz]], axis=1)

        w_na, w_gqa, w_mla, w_rw = [w_in[layer][:, bounds[i]:bounds[i + 1]].astype(BF16) for i in range(4)]
        w_uq = mla_w_uq[layer].reshape(MLA_Q_RANK, N_HEADS, MLA_QK_DIM)
        w_uq = jnp.concatenate([w_uq[:, :, :HEAD_DIM].reshape(MLA_Q_RANK, -1),
                                w_uq[:, :, HEAD_DIM:].reshape(MLA_Q_RANK, -1)], axis=1)
        w_ukv = mla_w_ukv[layer].reshape(MLA_KV_RANK, N_HEADS, 2 * HEAD_DIM)
        w_ukv = jnp.concatenate([w_ukv[:, :, :HEAD_DIM].reshape(MLA_KV_RANK, -1),
                                 w_ukv[:, :, HEAD_DIM:].reshape(MLA_KV_RANK, -1)], axis=1)
        weights = (w_na, w_gqa, jnp.pad(w_mla, ((0, 0), (0, mla_pad))), w_rw,
                   jnp.tile(gqa_q_norm[layer], N_HEADS)[None], jnp.tile(gqa_k_norm[layer], GQA_KV_HEADS)[None],
                   mla_q_norm[layer][None], mla_kv_norm[layer][None], w_uq.astype(BF16), w_ukv.astype(BF16))
        (a_q, a_k, a_v, b_q, b_k, b_v, c_q, c_k, c_v, rw, rw_edge) = in_projection(
            x_all, mod, norm1_g[layer][None], weights, tables, ctx_tiles)

        a_q, a_k, a_v = _pairs(a_q), _pairs(a_k), _pairs(a_v)
        bias = na_bias_table(na_rpb[layer], rows)
        out_a = neighbourhood_attention(a_q, a_k, a_v, bias.reshape((N_HEADS // 2, 2) + bias.shape[1:]), n_ctx)
        out_ac = attention(a_q, a_k, a_v, 0, ctx_tiles, n_ctx, HEAD_DIM) if want_ctx else None
        mix_b = _dense_branch(_pairs(b_q), b_k[:, :, None], b_v[:, :, None], n_ctx, want_ctx)
        mix_c = _dense_branch(_pairs(c_q), _pairs(c_k), _pairs(c_v), n_ctx, want_ctx)
        mix_d = _rwkv_mixer(rw, rw_edge, ctx_tiles, rwkv_shift[layer], rwkv_w0[layer], rwkv_w2[layer], rwkv_a0[layer],
                            rwkv_a2[layer], rwkv_k_k[layer], rwkv_k_a[layer], rwkv_r_k[layer],
                            rwkv_g2[layer], rwkv_lnx_w[layer], rwkv_lnx_b[layer])

        x_all = out_projection_mlp(x_all, [(out_a, out_ac), mix_b, mix_c], mix_d, mod, norm2_g[layer][None],
                                   final_norm_g[None], w_out[layer].astype(BF16), w_fc1[layer].astype(BF16),
                                   w_fc2[layer].astype(BF16), ctx_tiles, not want_ctx)
    return x_all
```

```python
import functools
import math

import numpy as np
import jax
import jax.numpy as jnp
from jax import lax
from jax.experimental import pallas as pl
from jax.experimental.pallas import tpu as pltpu

F32 = jnp.float32
BF16 = jnp.bfloat16

GRID_W = 64
HEAD_DIM = 64
N_HEADS = 4
GROUP_WIDTH = N_HEADS * HEAD_DIM
GQA_KV_HEADS = 2
NA_WIN_ROWS = 8
NA_WIN_COLS = 16
MLA_Q_RANK = 256
MLA_KV_RANK = 128
MLA_ROPE_DIM = 32
MLA_QK_DIM = HEAD_DIM + MLA_ROPE_DIM
RWKV_DECAY_RANK = 64
RWKV_ICLR_RANK = 64
RWKV_GATE_RANK = 160
NA_COLS = 3 * GROUP_WIDTH
GQA_COLS = (N_HEADS + 2 * GQA_KV_HEADS) * HEAD_DIM
MLA_COLS = MLA_Q_RANK + MLA_KV_RANK + MLA_ROPE_DIM
RWKV_COLS = 3 * GROUP_WIDTH + 2 * RWKV_DECAY_RANK + 2 * RWKV_ICLR_RANK + RWKV_GATE_RANK
ROPE_THETA = 10000.0
RMS_EPS = 1e-6
GN_EPS = 64e-5
NEG_INF = -1e30

TOKEN_TILE = 256
ATTN_Q_TILE = 256
ATTN_KV_CHUNK = 1024
V_EXT_WIDTH = 128
LOG2E = float(np.log2(np.e))
NA_Q_ROWS = ATTN_Q_TILE // GRID_W
NA_KEY_ROWS = NA_Q_ROWS + NA_WIN_ROWS - 1
RWKV_CHUNK = 64
RWKV_ROWS_PER_STEP = 8
VMEM_LIMIT = 56 * 1024 * 1024


def _params(*sem):
    return pltpu.CompilerParams(dimension_semantics=sem, vmem_limit_bytes=VMEM_LIMIT)


def _bdot(a, b):
    return jnp.dot(a.astype(BF16), b.astype(BF16), preferred_element_type=F32)


def _bdot_nt(a, b):
    return lax.dot_general(a.astype(BF16), b.astype(BF16), (((1,), (1,)), ((), ())),
                           preferred_element_type=F32)


def _mod_kernel(a_ref, w_ref, b_ref, o_ref):
    o_ref[0] = _bdot(a_ref[...], w_ref[0]) + b_ref[0]


def modulation(cond, w_mod, b_mod, tn=1536):
    depth, d, n = w_mod.shape
    r = cond.shape[0]
    return pl.pallas_call(
        _mod_kernel,
        grid=(depth, n // tn),
        in_specs=[pl.BlockSpec((r, d), lambda l, j: (0, 0)),
                  pl.BlockSpec((1, d, tn), lambda l, j: (l, 0, j)),
                  pl.BlockSpec((1, 1, tn), lambda l, j: (l, 0, j))],
        out_specs=pl.BlockSpec((1, r, tn), lambda l, j: (l, 0, j)),
        out_shape=jax.ShapeDtypeStruct((depth, r, n), F32),
        compiler_params=_params("parallel", "parallel"),
        name="modulation",
    )(cond, w_mod, b_mod)


def _rms(x, g):
    return x * lax.rsqrt(jnp.mean(x * x, axis=-1, keepdims=True) + RMS_EPS) * g


def _split3(x):
    hi = x.astype(BF16)
    rest = x - hi.astype(F32)
    mid = rest.astype(BF16)
    return hi, mid, (rest - mid.astype(F32)).astype(BF16)


def _select_rows(sel, x):
    sel = sel.astype(BF16)
    hi, mid, lo = _split3(x)
    dot = lambda part: jnp.dot(sel, part, preferred_element_type=F32)
    return dot(lo) + dot(mid) + dot(hi)


def _select_cols(x, sel):
    sel = sel.astype(BF16)
    hi, mid, lo = _split3(x)
    dot = lambda part: jnp.dot(part, sel, preferred_element_type=F32)
    return dot(lo) + dot(mid) + dot(hi)


def _pair_swap(x):
    axis = x.ndim - 1
    lane = lax.broadcasted_iota(jnp.int32, x.shape, axis)
    return jnp.where((lane & 1) == 0, pltpu.roll(x, x.shape[axis] - 1, axis=axis), pltpu.roll(x, 1, axis=axis))


def _rotate(x, cos, sin):
    return x * cos + _pair_swap(x) * sin


def _in_proj_kernel(x_ref, mod_ref, g_ref, wna_ref, wgqa_ref, wmla_ref, wrw_ref, gq_ref, gk_ref,
                    mq_ref, mkv_ref, wuq_ref, wukv_ref, cosh_ref, sinh_ref, cosm_ref, sinm_ref,
                    naq_ref, nak_ref, nav_ref, bq_ref, bk_ref, bv_ref, cq_ref, ck_ref, cv_ref, rw_ref, edge_ref):
    hd, gw = HEAD_DIM, GROUP_WIDTH
    mod = mod_ref[0, 0]
    hb = (_rms(x_ref[0], g_ref[...]) * (1.0 + mod[1:2]) + mod[0:1]).astype(BF16)
    tm = hb.shape[0]
    ones_col = (lax.broadcasted_iota(jnp.int32, (tm, V_EXT_WIDTH - hd), 1) == 0).astype(F32)
    qk_scale = LOG2E * hd ** -0.5

    def put_heads(ref, val, n_heads, width):
        for h in range(n_heads):
            ref[0, h] = val[:, h * width:(h + 1) * width].astype(ref.dtype)

    def put_values(ref, val, n_heads):
        for h in range(n_heads):
            ref[0, h] = jnp.concatenate([val[:, h * hd:(h + 1) * hd], ones_col], axis=-1).astype(ref.dtype)

    na = jnp.dot(hb, wna_ref[...], preferred_element_type=F32)
    put_heads(naq_ref, na[:, :gw] * qk_scale, N_HEADS, hd)
    put_heads(nak_ref, na[:, gw:2 * gw], N_HEADS, hd)
    put_values(nav_ref, na[:, 2 * gw:], N_HEADS)

    gqa = jnp.dot(hb, wgqa_ref[...], preferred_element_type=F32)
    row = lax.broadcasted_iota(jnp.int32, (gw, gw), 0)
    col = lax.broadcasted_iota(jnp.int32, (gw, gw), 1)
    shift = hd.bit_length() - 1
    head_sum = ((row >> shift) == (col >> shift)).astype(F32)
    cos_h, sin_h = cosh_ref[...], sinh_ref[...]
    kw = GQA_KV_HEADS * hd
    q = gqa[:, :gw]
    q = q * lax.rsqrt(_select_cols(q * q, head_sum) * (1.0 / hd) + RMS_EPS) * gq_ref[...]
    put_heads(bq_ref, _rotate(q, cos_h, sin_h) * qk_scale, N_HEADS, hd)
    k = gqa[:, gw:gw + kw]
    k = k * lax.rsqrt(_select_cols(k * k, head_sum[:kw, :kw]) * (1.0 / hd) + RMS_EPS) * gk_ref[...]
    put_heads(bk_ref, _rotate(k, cos_h[:, :kw], sin_h[:, :kw]), GQA_KV_HEADS, hd)
    put_values(bv_ref, gqa[:, gw + kw:], GQA_KV_HEADS)

    mla = jnp.dot(hb, wmla_ref[...], preferred_element_type=F32)
    cos_m, sin_m = cosm_ref[...], sinm_ref[...]
    mla_scale = LOG2E * MLA_QK_DIM ** -0.5
    uq = jnp.dot(_rms(mla[:, :MLA_Q_RANK], mq_ref[...]).astype(BF16), wuq_ref[...], preferred_element_type=F32)
    q_nope = uq[:, :gw] * mla_scale
    q_rope = _rotate(uq[:, gw:], cos_m, sin_m) * mla_scale
    lo = MLA_Q_RANK + MLA_KV_RANK
    ukv = jnp.dot(_rms(mla[:, MLA_Q_RANK:lo], mkv_ref[...]).astype(BF16), wukv_ref[...],
                  preferred_element_type=F32)
    k_rope = _rotate(mla[:, lo:], cos_m, sin_m)[:, :MLA_ROPE_DIM]
    rd = MLA_ROPE_DIM
    for h in range(N_HEADS):
        cq_ref[0, h] = jnp.concatenate([q_nope[:, h * hd:(h + 1) * hd], q_rope[:, h * rd:(h + 1) * rd]],
                                       axis=-1).astype(cq_ref.dtype)
        ck_ref[0, h] = jnp.concatenate([ukv[:, h * hd:(h + 1) * hd], k_rope], axis=-1).astype(ck_ref.dtype)
    put_values(cv_ref, ukv[:, gw:], N_HEADS)

    rw = jnp.dot(hb, wrw_ref[...], preferred_element_type=F32)
    rw_ref[0] = rw
    edge_ref[0, 0] = jnp.concatenate([rw[0:1], rw[tm - 1:tm]], axis=0)


def in_projection(x_all, mod, g, weights, tables, ctx_tiles):
    bsz, n_tok, d = x_all.shape
    tm = TOKEN_TILE
    tok = lambda b, i: (b, i, 0)
    full = lambda b, i: (0, 0)
    heads = lambda n, w: (jax.ShapeDtypeStruct((bsz, n, n_tok, w), BF16),
                          pl.BlockSpec((1, n, tm, w), lambda b, i: (b, 0, i, 0)))
    outs = [heads(N_HEADS, HEAD_DIM), heads(N_HEADS, HEAD_DIM), heads(N_HEADS, V_EXT_WIDTH),
            heads(N_HEADS, HEAD_DIM), heads(GQA_KV_HEADS, HEAD_DIM), heads(GQA_KV_HEADS, V_EXT_WIDTH),
            heads(N_HEADS, MLA_QK_DIM), heads(N_HEADS, MLA_QK_DIM), heads(N_HEADS, V_EXT_WIDTH),
            (jax.ShapeDtypeStruct((bsz, n_tok, RWKV_COLS), F32), pl.BlockSpec((1, tm, RWKV_COLS), tok)),
            (jax.ShapeDtypeStruct((bsz, n_tok // tm, 2, RWKV_COLS), F32),
             pl.BlockSpec((1, 1, 2, RWKV_COLS), lambda b, i: (b, i, 0, 0)))]
    return pl.pallas_call(
        _in_proj_kernel,
        grid=(bsz, n_tok // tm),
        in_specs=[pl.BlockSpec((1, tm, d), tok),
                  pl.BlockSpec((1, 1, 6, d), lambda b, i: (b, jnp.where(i < ctx_tiles, 0, 1), 0, 0)),
                  pl.BlockSpec((1, d), full)]
                 + [pl.BlockSpec(w.shape, full) for w in weights]
                 + [pl.BlockSpec((tm, t.shape[1]), lambda b, i: (i, 0)) for t in tables],
        out_specs=[o[1] for o in outs],
        out_shape=[o[0] for o in outs],
        compiler_params=_params("parallel", "parallel"),
        name="in_projection",
    )(x_all, mod, g, *weights, *tables)


def _out_mlp_kernel(*refs, hidden_tile, final_norm, ctx_tiles, n_branches):
    x_ref = refs[0]
    per_branch = 2 if ctx_tiles else 1
    branch_refs = refs[1:1 + per_branch * n_branches]
    mixd_ref, mod_ref, g2_ref, gf_ref, wout_ref, w1_ref, w2_ref, o_ref = refs[1 + per_branch * n_branches:]
    parts = []
    for j in range(n_branches):
        lat = branch_refs[per_branch * j][0]
        if ctx_tiles:
            lat = jnp.where(pl.program_id(1) < ctx_tiles, branch_refs[per_branch * j + 1][0], lat)
        parts.append(lat)
    parts.append(mixd_ref[0])
    mod = mod_ref[0, 0]
    proj = jnp.zeros(x_ref.shape[1:], F32)
    lo = 0
    for part in parts:
        proj = proj + jnp.dot(part, wout_ref[lo:lo + part.shape[1], :], preferred_element_type=F32)
        lo += part.shape[1]
    x1 = x_ref[0] + mod[2:3] * proj
    hb = (_rms(x1, g2_ref[...]) * (1.0 + mod[4:5]) + mod[3:4]).astype(BF16)
    acc = jnp.zeros_like(x1)
    for j in range(w1_ref.shape[1] // hidden_tile):
        cols = pl.ds(j * hidden_tile, hidden_tile)
        u = jnp.maximum(jnp.dot(hb, w1_ref[:, cols], preferred_element_type=F32), 0.0)
        acc = acc + jnp.dot((u * u).astype(BF16), w2_ref[cols, :], preferred_element_type=F32)
    x2 = x1 + mod[5:6] * acc
    if final_norm:
        x2 = _rms(x2, gf_ref[...])
    o_ref[0] = x2


def out_projection_mlp(x_all, branches, mix_d, mod, g2, g_final, w_out, w_fc1, w_fc2, ctx_tiles, last):
    bsz, n_tok, d = x_all.shape
    tm = TOKEN_TILE
    n_tiles = n_tok // tm
    skip = ctx_tiles if last else 0
    tok = lambda b, i: (b, i + skip, 0)
    full = lambda b, i: (0, 0)
    lat_map = lambda b, i: (b, jnp.maximum(i + skip - ctx_tiles, 0), 0)
    ctx_map = lambda b, i: (b, jnp.minimum(i, ctx_tiles - 1), 0)
    branch_args, branch_specs = [], []
    for lat, ctx_rows in branches:
        branch_args.append(lat)
        branch_specs.append(pl.BlockSpec((1, tm, lat.shape[2]), lat_map))
        if not last:
            branch_args.append(ctx_rows)
            branch_specs.append(pl.BlockSpec((1, tm, ctx_rows.shape[2]), ctx_map))
    kern = functools.partial(_out_mlp_kernel, hidden_tile=1024, final_norm=last,
                             ctx_tiles=0 if last else ctx_tiles, n_branches=len(branches))
    return pl.pallas_call(
        kern,
        grid=(bsz, n_tiles - skip),
        in_specs=[pl.BlockSpec((1, tm, d), tok)] + branch_specs
                 + [pl.BlockSpec((1, tm, mix_d.shape[2]), tok),
                    pl.BlockSpec((1, 1, 6, d), lambda b, i: (b, jnp.where(i + skip < ctx_tiles, 0, 1), 0, 0)),
                    pl.BlockSpec((1, d), full),
                    pl.BlockSpec((1, d), full),
                    pl.BlockSpec(w_out.shape, full, pipeline_mode=pl.Buffered(1)),
                    pl.BlockSpec(w_fc1.shape, full, pipeline_mode=pl.Buffered(1)),
                    pl.BlockSpec(w_fc2.shape, full, pipeline_mode=pl.Buffered(1))],
        out_specs=pl.BlockSpec((1, tm, d), lambda b, i: (b, i, 0)),
        out_shape=jax.ShapeDtypeStruct((bsz, n_tok - skip * tm, d), F32),
        compiler_params=_params("parallel", "parallel"),
        name="out_projection_mlp",
    )(x_all, *branch_args, mix_d, mod, g2, g_final, w_out, w_fc1, w_fc2)


def _key_chunks(n_keys):
    first = min(n_keys, ATTN_Q_TILE)
    chunks = [(0, first)]
    while chunks[-1][0] + chunks[-1][1] < n_keys:
        start = chunks[-1][0] + chunks[-1][1]
        chunks.append((start, min(ATTN_KV_CHUNK, n_keys - start)))
    return chunks


def _online_softmax(q, k_at, v_at, chunks):
    rows = q.shape[0]
    scores = lambda c: _bdot_nt(q, k_at(*c))
    m = jnp.full((rows, 1), NEG_INF, F32)
    acc = jnp.zeros((rows, V_EXT_WIDTH), F32)
    s_next = scores(chunks[0])
    for j, chunk in enumerate(chunks):
        s = s_next
        if j + 1 < len(chunks):
            s_next = scores(chunks[j + 1])
        m_new = jnp.maximum(m, jnp.max(s, axis=-1, keepdims=True))
        p = jnp.exp2(s - m_new)
        acc = acc * jnp.exp2(m - m_new) + jnp.dot(p.astype(BF16), v_at(*chunk), preferred_element_type=F32)
        m = m_new
    return acc


def _attention_kernel(q_ref, k_ref, v_ref, o_ref):
    n_q, tq, dk = q_ref.shape[2], q_ref.shape[3], q_ref.shape[4]
    dv = o_ref.shape[-1] // n_q
    chunks = _key_chunks(k_ref.shape[3])
    if k_ref.shape[2] == 1:
        groups = [(q_ref[0, 0].reshape(n_q * tq, dk), 0)]
    else:
        groups = [(q_ref[0, 0, g], g) for g in range(n_q)]
    outs = []
    for q, g in groups:
        acc = _online_softmax(q, lambda s, n: k_ref[0, 0, g, pl.ds(s, n), :],
                              lambda s, n: v_ref[0, 0, g, pl.ds(s, n), :], chunks)
        o = acc[:, :dv] / acc[:, dv:dv + 1]
        outs.extend(o[i * tq:(i + 1) * tq] for i in range(o.shape[0] // tq))
    o_ref[0] = jnp.concatenate(outs, axis=-1).astype(o_ref.dtype)


def attention(q, k, v, q_tile0, n_q_tiles, n_keys, dv):
    bsz, n_pairs, n_q, _, dk = q.shape
    n_kv = k.shape[2]
    tq = ATTN_Q_TILE
    return pl.pallas_call(
        _attention_kernel,
        grid=(bsz, n_pairs, n_q_tiles),
        in_specs=[pl.BlockSpec((1, 1, n_q, tq, dk), lambda b, h, i: (b, h, 0, i + q_tile0, 0)),
                  pl.BlockSpec((1, 1, n_kv, n_keys, dk), lambda b, h, i: (b, h, 0, 0, 0)),
                  pl.BlockSpec((1, 1, n_kv, n_keys, v.shape[-1]), lambda b, h, i: (b, h, 0, 0, 0))],
        out_specs=pl.BlockSpec((1, tq, n_q * dv), lambda b, h, i: (b, i, h)),
        out_shape=jax.ShapeDtypeStruct((bsz, n_q_tiles * tq, n_pairs * n_q * dv), BF16),
        compiler_params=_params("parallel", "parallel", "parallel"),
        name="attention",
    )(q, k, v)


def _na_kernel(q_ref, k_ref, v_ref, bias_ref, o_ref, *, n_ctx, n_row_tiles):
    i = pl.program_id(2)
    n_win = NA_KEY_ROWS * GRID_W
    max_row0 = n_row_tiles * NA_Q_ROWS - NA_KEY_ROWS
    row0 = jnp.clip(i * NA_Q_ROWS - NA_WIN_ROWS // 2, 0, max_row0)
    start = pl.multiple_of(n_ctx + row0 * GRID_W, GRID_W)
    dv = o_ref.shape[-1] // q_ref.shape[2]
    outs = []
    for g in range(q_ref.shape[2]):
        q = q_ref[0, 0, g]
        s_win = _bdot_nt(q, k_ref[0, 0, g, pl.ds(start, n_win), :]) + bias_ref[0, g, 0]
        s_ctx = _bdot_nt(q, k_ref[0, 0, g, pl.ds(0, n_ctx), :])
        m = jnp.maximum(jnp.max(s_win, axis=-1, keepdims=True), jnp.max(s_ctx, axis=-1, keepdims=True))
        p_win = jnp.exp2(s_win - m).astype(BF16)
        p_ctx = jnp.exp2(s_ctx - m).astype(BF16)
        o = (jnp.dot(p_win, v_ref[0, 0, g, pl.ds(start, n_win), :], preferred_element_type=F32)
             + jnp.dot(p_ctx, v_ref[0, 0, g, pl.ds(0, n_ctx), :], preferred_element_type=F32))
        outs.append(o[:, :dv] / o[:, dv:dv + 1])
    o_ref[0] = jnp.concatenate(outs, axis=-1).astype(o_ref.dtype)


def na_bias_table(rpb, rows):
    n_tiles = rows // NA_Q_ROWS
    n_heads, n_dr, n_dc = rpb.shape
    qc = np.arange(GRID_W)[:, None]
    kc = np.arange(GRID_W)[None, :]
    dc = np.clip(kc - qc + NA_WIN_COLS - 1, 0, n_dc - 1)
    onehot = jnp.asarray(dc.reshape(-1)[None, :] == np.arange(n_dc)[:, None], F32)
    by_col = jnp.dot(rpb.reshape(-1, n_dc).astype(F32), onehot, precision=lax.Precision.HIGHEST)
    pad = NA_KEY_ROWS
    by_col = jnp.pad(by_col.reshape(n_heads, n_dr, GRID_W, GRID_W), ((0, 0), (pad, pad), (0, 0), (0, 0)))
    c_start = np.clip(qc - NA_WIN_COLS // 2, 0, GRID_W - NA_WIN_COLS)
    col_ok = (kc >= c_start) & (kc < c_start + NA_WIN_COLS)
    tables = []
    for tile in (0, 1, n_tiles - 1):
        row0 = int(np.clip(tile * NA_Q_ROWS - NA_WIN_ROWS // 2, 0, rows - NA_KEY_ROWS))
        per_row = []
        for a in range(NA_Q_ROWS):
            qr = tile * NA_Q_ROWS + a
            dr0 = row0 - qr + NA_WIN_ROWS - 1 + pad
            kr = row0 + np.arange(NA_KEY_ROWS)
            r_start = int(np.clip(qr - NA_WIN_ROWS // 2, 0, rows - NA_WIN_ROWS))
            row_ok = (kr >= r_start) & (kr < r_start + NA_WIN_ROWS)
            valid = row_ok[None, :, None] & col_ok[:, None, :]
            sl = jnp.swapaxes(by_col[:, dr0:dr0 + NA_KEY_ROWS], 1, 2)
            per_row.append(jnp.where(valid[None], sl * LOG2E, NEG_INF))
        bias = jnp.stack(per_row, axis=1)
        tables.append(bias.reshape(n_heads, NA_Q_ROWS * GRID_W, NA_KEY_ROWS * GRID_W))
    return jnp.stack(tables, axis=1)


def neighbourhood_attention(q, k, v, bias, n_ctx):
    bsz, n_pairs, n_q, n_tok, d = q.shape
    tq = ATTN_Q_TILE
    n_tiles = (n_tok - n_ctx) // tq
    ctx_tiles = n_ctx // tq
    kern = functools.partial(_na_kernel, n_ctx=n_ctx, n_row_tiles=n_tiles)

    def bias_map(b, h, i):
        return (h, 0, jnp.where(i == 0, 0, jnp.where(i == n_tiles - 1, 2, 1)), 0, 0)

    whole = lambda b, h, i: (b, h, 0, 0, 0)
    return pl.pallas_call(
        kern,
        grid=(bsz, n_pairs, n_tiles),
        in_specs=[pl.BlockSpec((1, 1, n_q, tq, d), lambda b, h, i: (b, h, 0, i + ctx_tiles, 0)),
                  pl.BlockSpec((1, 1, n_q, n_tok, d), whole),
                  pl.BlockSpec((1, 1, n_q, n_tok, v.shape[-1]), whole),
                  pl.BlockSpec((1, n_q, 1) + bias.shape[3:], bias_map)],
        out_specs=pl.BlockSpec((1, tq, n_q * d), lambda b, h, i: (b, i, h)),
        out_shape=jax.ShapeDtypeStruct((bsz, n_tok - n_ctx, n_pairs * n_q * d), BF16),
        compiler_params=_params("parallel", "parallel", "parallel"),
        name="neighbourhood_attention",
    )(q, k, v, bias)


def _softplus(z):
    return jnp.maximum(z, 0.0) + jnp.log(1.0 + jnp.exp(-jnp.abs(z)))


def _sigmoid(z):
    return 1.0 / (1.0 + jnp.exp(-z))


def _rwkv_prep_kernel(u_ref, halo_ref, taps_ref, w0_ref, w2_ref, a0_ref, a2_ref, kk_ref, ka_ref, rk_ref,
                      g2_ref, *out_refs):
    dir_refs = (out_refs[0:4], out_refs[4:8])
    v_out, gt_refs, bonus_ref, gate_ref = out_refs[8], out_refs[9:11], out_refs[11], out_refs[12]
    tm = u_ref.shape[1]
    c = RWKV_CHUNK
    n_chunks = tm // c
    gw = GROUP_WIDTH
    u = u_ref[0]
    halo = halo_ref[0, 0]
    tok = lax.broadcasted_iota(jnp.int32, u.shape, 0)
    prev = jnp.where(tok == 0, halo[0:1], pltpu.roll(u, 1, axis=0))
    nxt = jnp.where(tok == tm - 1, halo[1:2], pltpu.roll(u, tm - 1, axis=0))
    taps = taps_ref[...]
    s = prev * taps[0:1] + u * taps[1:2] + nxt * taps[2:3]

    r, k, v = s[:, 0:gw], s[:, gw:2 * gw], s[:, 2 * gw:3 * gw]
    lo = 3 * gw
    w_low = jnp.tanh(s[:, lo:lo + 2 * RWKV_DECAY_RANK])
    lo += 2 * RWKV_DECAY_RANK
    a_low = s[:, lo:lo + 2 * RWKV_ICLR_RANK]
    lo += 2 * RWKV_ICLR_RANK
    g_low = _sigmoid(s[:, lo:lo + RWKV_GATE_RANK])
    w_log = -_softplus(-(w0_ref[...] + _bdot(w_low, w2_ref[...]))) - 0.5
    log_decay = -jnp.exp(w_log)
    iclr = _sigmoid(a0_ref[...] + _bdot(a_low, a2_ref[...]))
    gate_ref[0] = _bdot(g_low, g2_ref[...])

    row = lax.broadcasted_iota(jnp.int32, (tm, tm), 0)
    col = lax.broadcasted_iota(jnp.int32, (tm, tm), 1)
    shift = c.bit_length() - 1
    same = (row >> shift) == (col >> shift)
    head_shift = HEAD_DIM.bit_length() - 1
    ones_blk = ((row >> head_shift) == (col >> head_shift)).astype(F32)

    v_out[0] = v.astype(v_out.dtype)
    bonus = jnp.zeros_like(r)
    for d in range(2):
        sl = slice(d * gw, (d + 1) * gw)
        lw = log_decay[:, sl]
        kk = k * kk_ref[:, sl]
        kk = kk * lax.rsqrt(jnp.maximum(_select_cols(kk * kk, ones_blk), 1e-24))
        k_d = k * (1.0 + (iclr[:, sl] - 1.0) * ka_ref[:, sl])
        b_vec = kk * iclr[:, sl]
        tri = (same & ((row >= col) if d == 0 else (row <= col))).astype(F32)
        cum = _select_rows(tri, lw)
        ends = [cum[i * c + (c - 1 if d == 0 else 0)][None] for i in range(n_chunks)]
        e_inv = jnp.exp(-cum)
        streams = (r * jnp.exp(cum), -kk * jnp.exp(cum - lw), b_vec * e_inv, k_d * e_inv)
        for ref, val in zip(dir_refs[d], streams):
            ref[0] = val.astype(ref.dtype)
        gt_refs[d][0, 0] = jnp.exp(jnp.concatenate(ends, axis=0))
        bonus = bonus + r * k_d * rk_ref[:, sl]
    bonus_ref[0] = _select_cols(bonus, ones_blk) * v


def rwkv_prepare(rw, halo, taps, w0, w2, a0, a2, k_k, k_a, r_k, g2):
    bsz, n_tok, width = rw.shape
    tm = TOKEN_TILE
    n_tiles = n_tok // tm
    n_chunks = tm // RWKV_CHUNK
    full = lambda b, i: (0, 0)
    consts = (taps, w0, w2, a0, a2, k_k, k_a, r_k, g2)
    assert tm == GROUP_WIDTH
    tokens_spec = pl.BlockSpec((1, tm, GROUP_WIDTH), lambda b, i: (b, i, 0))
    stream = jax.ShapeDtypeStruct((bsz, n_tok, GROUP_WIDTH), BF16)
    stream_spec = tokens_spec
    decay = jax.ShapeDtypeStruct((bsz, n_tiles, n_chunks, GROUP_WIDTH), F32)
    decay_spec = pl.BlockSpec((1, 1, n_chunks, GROUP_WIDTH), lambda b, i: (b, i, 0, 0))
    tokens = jax.ShapeDtypeStruct((bsz, n_tok, GROUP_WIDTH), F32)
    return pl.pallas_call(
        _rwkv_prep_kernel,
        grid=(bsz, n_tiles),
        in_specs=[pl.BlockSpec((1, tm, width), lambda b, i: (b, i, 0)),
                  pl.BlockSpec((1, 1, 2, width), lambda b, i: (b, i, 0, 0))]
                 + [pl.BlockSpec(t.shape, full) for t in consts],
        out_specs=[stream_spec] * 9 + [decay_spec] * 2 + [tokens_spec] * 2,
        out_shape=[stream] * 9 + [decay] * 2 + [tokens] * 2,
        compiler_params=_params("parallel", "parallel"),
        name="rwkv_prepare",
    )(rw, halo, *consts)


def _bmm(a, b):
    return lax.dot_general(a.astype(BF16), b.astype(BF16), (((2,), (1,)), ((0,), (0,))),
                           preferred_element_type=F32)


def _bmm_nt(a, b):
    return lax.dot_general(a.astype(BF16), b.astype(BF16), (((2,), (2,)), ((0,), (0,))),
                           preferred_element_type=F32)


def _bmm_tn(a, b):
    return lax.dot_general(a.astype(BF16), b.astype(BF16), (((1,), (1,)), ((0,), (0,))),
                           preferred_element_type=F32)


def _rwkv_chunk_kernel(rt_ref, at_ref, bt_ref, kt_ref, v_ref, gt_ref, *rest, reverse):
    y_ref, s_ref = rest[-2:]
    other_ref, bonus_ref, gate_ref, lw_ref, lb_ref = rest[:5] if len(rest) > 2 else (None,) * 5
    n_rows, tb, width = v_ref.shape
    n = width // 2
    c = RWKV_CHUNK
    n_chunks = tb // c
    g = n_rows * n_chunks

    @pl.when(pl.program_id(2) == 0)
    def _():
        s_ref[...] = jnp.zeros_like(s_ref)

    load = lambda ref: ref[...].reshape(g, c, width)
    rt, at, bt, kt, v = (load(ref) for ref in (rt_ref, at_ref, bt_ref, kt_ref, v_ref))
    left = lax.broadcasted_iota(jnp.int32, (1, 1, width), 2) < n
    cat = lambda a, b: jnp.concatenate([a, b], axis=1)

    def bd(y):
        y = y.astype(BF16)
        zero = jnp.zeros_like(y)
        return cat(jnp.where(left, y, zero), jnp.where(left, zero, y))

    row = lax.broadcasted_iota(jnp.int32, (c, 2 * c), 0)
    col = lax.broadcasted_iota(jnp.int32, (c, 2 * c), 1) & (c - 1)
    if reverse:
        row, col = col, row
    strict = row > col
    incl = row >= col
    ar = cat(at, rt)
    ab_rb = _bmm_nt(ar, bd(bt))
    ak_rk = _bmm_nt(ar, bd(kt))
    a_ab = jnp.where(strict, ab_rb[:, :c], 0.0)
    a_rb = jnp.where(incl, ab_rb[:, c:], 0.0)
    a_ak = jnp.where(strict, ak_rk[:, :c], 0.0)
    a_rk = jnp.where(incl, ak_rk[:, c:], 0.0)
    v_bd = bd(v)
    av = _bmm(a_ak, v_bd)
    n1 = jnp.where((row >> 3) == (col >> 3), a_ab, 0.0)
    n2 = _bmm(n1, bd(n1))
    n2_bd = bd(n2)
    n4 = _bmm(n2, n2_bd)
    p = (row == col).astype(F32) + n1 + n2 + _bmm(n1, n2_bd)
    inv = p + _bmm(p, bd(n4))
    for level in range(3, c.bit_length() - 1):
        rb, cb = row >> level, col >> level
        off = jnp.where((rb == cb + 1) & ((rb & 1) == 1), a_ab, 0.0)
        inv = inv + _bmm(_bmm(inv, bd(off)), bd(inv))
    w = _bmm(inv, bd(at))
    u0 = _bmm(inv, bd(av))
    q_eff = rt.astype(F32) + _bmm(a_rb, bd(w))
    y0 = _bmm(jnp.concatenate([a_rb, a_rk], axis=2), cat(bd(u0), v_bd))

    per_chunk = lambda t: t.reshape((n_rows, n_chunks) + t.shape[1:])
    qw = per_chunk(cat(q_eff, w))
    u0, y0, v = per_chunk(u0), per_chunk(y0), per_chunk(v)
    g_tot = gt_ref[:, 0][:, :, None, :]
    bk_end = per_chunk(cat(bt, kt)) * g_tot
    state = s_ref[...]
    for i in (range(n_chunks - 1, -1, -1) if reverse else range(n_chunks)):
        tokens = slice(i * c, (i + 1) * c)
        yu = _bmm_nt(qw[:, i], bd(state))
        y = yu[:, :c] + y0[:, i]
        if other_ref is not None:
            y = _rwkv_output(y + other_ref[:, tokens, :], bonus_ref[:, tokens, :], gate_ref[:, tokens, :],
                             lw_ref[...], lb_ref[...])
        y_ref[:, tokens, :] = y.astype(y_ref.dtype)
        full = _bmm_tn(cat(yu[:, c:] + u0[:, i], v[:, i]), bk_end[:, i])
        state = g_tot[:, i] * state + jnp.where(left, full[:, :n], full[:, n:])
    s_ref[...] = state


def _rwkv_output(y, bonus, gate, lnx_w, lnx_b):
    shape = y.shape
    y = y.reshape(-1, shape[-1])
    row = lax.broadcasted_iota(jnp.int32, (shape[-1], shape[-1]), 0)
    col = lax.broadcasted_iota(jnp.int32, (shape[-1], shape[-1]), 1)
    shift = HEAD_DIM.bit_length() - 1
    head_sum = ((row >> shift) == (col >> shift)).astype(F32)
    mu = _select_cols(y, head_sum) * (1.0 / HEAD_DIM)
    var = _select_cols(jnp.square(y - mu), head_sum) * (1.0 / HEAD_DIM)
    yn = (y - mu) * lax.rsqrt(var + GN_EPS) * lnx_w + lnx_b
    return (yn.reshape(shape) + bonus) * gate


def rwkv_chunk_scan(streams, v, g_tot, ctx_blocks, reverse, finish=None):
    bsz, n_tok, width = v.shape
    tb = TOKEN_TILE
    n_blocks = n_tok // tb
    pair = 2 * HEAD_DIM

    def block(i):
        if not reverse:
            return i
        return jnp.where(i < ctx_blocks, ctx_blocks - 1 - i, n_blocks - 1 + ctx_blocks - i)

    rows = math.gcd(RWKV_ROWS_PER_STEP, bsz)
    spec = pl.BlockSpec((rows, tb, pair), lambda b, p, i: (b, block(i), p))
    in_specs = [spec] * 5 + [pl.BlockSpec((rows, 1, g_tot.shape[2], pair), lambda b, p, i: (b, block(i), 0, p))]
    args = (*streams, v, g_tot)
    if finish is not None:
        vec = pl.BlockSpec((1, pair), lambda b, p, i: (0, p))
        in_specs += [spec, spec, spec, vec, vec]
        args += tuple(finish)
    return pl.pallas_call(
        functools.partial(_rwkv_chunk_kernel, reverse=reverse),
        grid=(bsz // rows, width // pair, n_blocks),
        in_specs=in_specs,
        out_specs=spec,
        out_shape=jax.ShapeDtypeStruct((bsz, n_tok, width), F32 if finish is None else BF16),
        scratch_shapes=[pltpu.VMEM((rows, HEAD_DIM, pair), F32)],
        compiler_params=_params("parallel", "parallel", "arbitrary"),
        name="rwkv_chunk_scan_bwd" if reverse else "rwkv_chunk_scan_fwd",
    )(*args)


def _rope_tables(n_ctx, n_lat, rot_dim, n_rep):
    t = jnp.arange(n_lat, dtype=jnp.int32)
    rows = (t // GRID_W).astype(F32)
    cols = (t % GRID_W).astype(F32)
    per_axis = rot_dim // 2
    inv_freq = ROPE_THETA ** (-jnp.arange(0, per_axis, 2, dtype=F32) / per_axis)
    ang = jnp.concatenate([rows[:, None] * inv_freq, cols[:, None] * inv_freq], axis=-1)
    ang = jnp.concatenate([jnp.zeros((n_ctx, ang.shape[1]), F32), ang], axis=0)
    sign = jnp.tile(jnp.array([-1.0, 1.0], F32), rot_dim // 2)
    cos = jnp.repeat(jnp.cos(ang), 2, axis=-1)
    sin = jnp.repeat(jnp.sin(ang), 2, axis=-1) * sign
    return jnp.tile(cos, (1, n_rep)), jnp.tile(sin, (1, n_rep))


def _pairs(t):
    return t.reshape((t.shape[0], t.shape[1] // 2, 2) + t.shape[2:])


def _block_diag(m0, m1):
    z01 = jnp.zeros((m0.shape[0], m1.shape[1]), m0.dtype)
    z10 = jnp.zeros((m1.shape[0], m0.shape[1]), m0.dtype)
    return jnp.concatenate([jnp.concatenate([m0, z01], axis=1), jnp.concatenate([z10, m1], axis=1)], axis=0)


def _shift_halo(edge, ctx_tiles):
    zero = jnp.zeros_like(edge[:, :1, 0])
    prev = jnp.concatenate([zero, edge[:, :-1, 1]], axis=1)
    nxt = jnp.concatenate([edge[:, 1:, 0], zero], axis=1)
    tile = jnp.arange(prev.shape[1])[None, :, None]
    prev = jnp.where(tile == ctx_tiles, 0.0, prev)
    nxt = jnp.where(tile == ctx_tiles - 1, 0.0, nxt)
    return jnp.stack([prev, nxt], axis=2)


def _rwkv_mixer(rw, rw_edge, ctx_tiles, shift_taps, w0, w2, a0, a2, k_k, k_a, r_k, g2, lnx_w, lnx_b):
    both = lambda t: t.reshape(1, -1)
    outs = rwkv_prepare(rw, _shift_halo(rw_edge, ctx_tiles), shift_taps, both(w0),
                        _block_diag(w2[0], w2[1]).astype(BF16), both(a0),
                        _block_diag(a2[0], a2[1]).astype(BF16), both(k_k), both(k_a), both(r_k),
                        g2.astype(BF16))
    v, bonus, gate = outs[8], outs[11], outs[12]
    y_f = rwkv_chunk_scan(outs[0:4], v, outs[9], ctx_tiles, False)
    return rwkv_chunk_scan(outs[4:8], v, outs[10], ctx_tiles, True,
                           finish=(y_f, bonus, gate, lnx_w[None], lnx_b[None]))


def _dense_branch(q, k, v, n_ctx, want_ctx):
    n_tok = q.shape[3]
    ctx_tiles = n_ctx // ATTN_Q_TILE
    o_lat = attention(q, k, v, ctx_tiles, (n_tok - n_ctx) // ATTN_Q_TILE, n_tok, HEAD_DIM)
    o_ctx = attention(q, k, v, 0, ctx_tiles, n_ctx, HEAD_DIM) if want_ctx else None
    return o_lat, o_ctx


def kernel(x, c, ctx, c_ctx, w_mod, b_mod, norm1_g, norm2_g, w_in, rwkv_shift, na_rpb, gqa_q_norm, gqa_k_norm, mla_q_norm, mla_kv_norm, mla_w_uq, mla_w_ukv, rwkv_w0, rwkv_w2, rwkv_a0, rwkv_a2, rwkv_k_k, rwkv_k_a, rwkv_r_k, rwkv_g2, rwkv_lnx_w, rwkv_lnx_b, w_out, w_fc1, w_fc2, final_norm_g):
    bsz, n_lat, d_model = x.shape
    n_ctx = ctx.shape[1]
    depth = w_mod.shape[0]
    rows = n_lat // GRID_W
    ctx_tiles = n_ctx // TOKEN_TILE
    assert n_ctx % TOKEN_TILE == 0 and n_lat % TOKEN_TILE == 0 and rows % NA_Q_ROWS == 0

    tables = (_rope_tables(n_ctx, n_lat, HEAD_DIM, N_HEADS) + _rope_tables(n_ctx, n_lat, MLA_ROPE_DIM, N_HEADS))

    cond = jnp.concatenate([jax.nn.silu(c), jax.nn.silu(c_ctx)[None]], axis=0)
    n_cond = -(-cond.shape[0] // 8) * 8
    cond = jnp.pad(cond, ((0, n_cond - cond.shape[0]), (0, 0)))
    mod_all = modulation(cond, w_mod, b_mod[:, None, :])

    x_all = jnp.concatenate([ctx, x], axis=1)
    bounds = np.cumsum([0, NA_COLS, GQA_COLS, MLA_COLS, RWKV_COLS])
    mla_pad = -MLA_COLS % 128

    for layer in range(depth):
        want_ctx = layer < depth - 1
        mod = mod_all[layer].reshape(n_cond, 6, d_model)
        mod = jnp.stack([jnp.broadcast_to(mod[bsz], (bsz, 6, d_model)), mod[:bsz]], axis=1)

        w_na, w_gqa, w_mla, w_rw = [w_in[layer][:, bounds[i]:bounds[i + 1]].astype(BF16) for i in range(4)]
        w_uq = mla_w_uq[layer].reshape(MLA_Q_RANK, N_HEADS, MLA_QK_DIM)
        w_uq = jnp.concatenate([w_uq[:, :, :HEAD_DIM].reshape(MLA_Q_RANK, -1),
                                w_uq[:, :, HEAD_DIM:].reshape(MLA_Q_RANK, -1)], axis=1)
        w_ukv = mla_w_ukv[layer].reshape(MLA_KV_RANK, N_HEADS, 2 * HEAD_DIM)
        w_ukv = jnp.concatenate([w_ukv[:, :, :HEAD_DIM].reshape(MLA_KV_RANK, -1),
                                 w_ukv[:, :, HEAD_DIM:].reshape(MLA_KV_RANK, -1)], axis=1)
        weights = (w_na, w_gqa, jnp.pad(w_mla, ((0, 0), (0, mla_pad))), w_rw,
                   jnp.tile(gqa_q_norm[layer], N_HEADS)[None], jnp.tile(gqa_k_norm[layer], GQA_KV_HEADS)[None],
                   mla_q_norm[layer][None], mla_kv_norm[layer][None], w_uq.astype(BF16), w_ukv.astype(BF16))
        (a_q, a_k, a_v, b_q, b_k, b_v, c_q, c_k, c_v, rw, rw_edge) = in_projection(
            x_all, mod, norm1_g[layer][None], weights, tables, ctx_tiles)

        a_q, a_k, a_v = _pairs(a_q), _pairs(a_k), _pairs(a_v)
        bias = na_bias_table(na_rpb[layer], rows)
        out_a = neighbourhood_attention(a_q, a_k, a_v, bias.reshape((N_HEADS // 2, 2) + bias.shape[1:]), n_ctx)
        out_ac = attention(a_q, a_k, a_v, 0, ctx_tiles, n_ctx, HEAD_DIM) if want_ctx else None
        mix_b = _dense_branch(_pairs(b_q), b_k[:, :, None], b_v[:, :, None], n_ctx, want_ctx)
        mix_c = _dense_branch(_pairs(c_q), _pairs(c_k), _pairs(c_v), n_ctx, want_ctx)
        mix_d = _rwkv_mixer(rw, rw_edge, ctx_tiles, rwkv_shift[layer], rwkv_w0[layer], rwkv_w2[layer], rwkv_a0[layer],
                            rwkv_a2[layer], rwkv_k_k[layer], rwkv_k_a[layer], rwkv_r_k[layer],
                            rwkv_g2[layer], rwkv_lnx_w[layer], rwkv_lnx_b[layer])

        x_all = out_projection_mlp(x_all, [(out_a, out_ac), mix_b, mix_c], mix_d, mod, norm2_g[layer][None],
                                   final_norm_g[None], w_out[layer].astype(BF16), w_fc1[layer].astype(BF16),
                                   w_fc2[layer].astype(BF16), ctx_tiles, not want_ctx)
    return x_all
```

```python
import functools
import math

import numpy as np
import jax
import jax.numpy as jnp
from jax import lax
from jax.experimental import pallas as pl
from jax.experimental.pallas import tpu as pltpu

F32 = jnp.float32
BF16 = jnp.bfloat16

GRID_W = 64
HEAD_DIM = 64
N_HEADS = 4
GROUP_WIDTH = N_HEADS * HEAD_DIM
GQA_KV_HEADS = 2
NA_WIN_ROWS = 8
NA_WIN_COLS = 16
MLA_Q_RANK = 256
MLA_KV_RANK = 128
MLA_ROPE_DIM = 32
MLA_QK_DIM = HEAD_DIM + MLA_ROPE_DIM
RWKV_DECAY_RANK = 64
RWKV_ICLR_RANK = 64
RWKV_GATE_RANK = 160
NA_COLS = 3 * GROUP_WIDTH
GQA_COLS = (N_HEADS + 2 * GQA_KV_HEADS) * HEAD_DIM
MLA_COLS = MLA_Q_RANK + MLA_KV_RANK + MLA_ROPE_DIM
RWKV_COLS = 3 * GROUP_WIDTH + 2 * RWKV_DECAY_RANK + 2 * RWKV_ICLR_RANK + RWKV_GATE_RANK
ROPE_THETA = 10000.0
RMS_EPS = 1e-6
GN_EPS = 64e-5
NEG_INF = -1e30

TOKEN_TILE = 256
ATTN_Q_TILE = 256
ATTN_LAT_Q_TILE = 1024
ATTN_KV_CHUNK = 1024
V_EXT_WIDTH = 128
LOG2E = float(np.log2(np.e))
NA_Q_ROWS = ATTN_Q_TILE // GRID_W
NA_KEY_ROWS = NA_Q_ROWS + NA_WIN_ROWS - 1
RWKV_CHUNK = 64
RWKV_ROWS_PER_STEP = 8
VMEM_LIMIT = 56 * 1024 * 1024


def _params(*sem):
    return pltpu.CompilerParams(dimension_semantics=sem, vmem_limit_bytes=VMEM_LIMIT)


def _bdot(a, b):
    return jnp.dot(a.astype(BF16), b.astype(BF16), preferred_element_type=F32)


def _bdot_nt(a, b):
    return lax.dot_general(a.astype(BF16), b.astype(BF16), (((1,), (1,)), ((), ())),
                           preferred_element_type=F32)


def _mod_kernel(a_ref, w_ref, b_ref, o_ref):
    o_ref[0] = _bdot(a_ref[...], w_ref[0]) + b_ref[0]


def modulation(cond, w_mod, b_mod, tn=1536):
    depth, d, n = w_mod.shape
    r = cond.shape[0]
    return pl.pallas_call(
        _mod_kernel,
        grid=(depth, n // tn),
        in_specs=[pl.BlockSpec((r, d), lambda l, j: (0, 0)),
                  pl.BlockSpec((1, d, tn), lambda l, j: (l, 0, j)),
                  pl.BlockSpec((1, 1, tn), lambda l, j: (l, 0, j))],
        out_specs=pl.BlockSpec((1, r, tn), lambda l, j: (l, 0, j)),
        out_shape=jax.ShapeDtypeStruct((depth, r, n), F32),
        compiler_params=_params("parallel", "parallel"),
        name="modulation",
    )(cond, w_mod, b_mod)


def _rms(x, g):
    return x * lax.rsqrt(jnp.mean(x * x, axis=-1, keepdims=True) + RMS_EPS) * g


def _split3(x):
    hi = x.astype(BF16)
    rest = x - hi.astype(F32)
    mid = rest.astype(BF16)
    return hi, mid, (rest - mid.astype(F32)).astype(BF16)


def _select_rows(sel, x):
    sel = sel.astype(BF16)
    hi, mid, lo = _split3(x)
    dot = lambda part: jnp.dot(sel, part, preferred_element_type=F32)
    return dot(lo) + dot(mid) + dot(hi)


def _select_cols(x, sel):
    sel = sel.astype(BF16)
    hi, mid, lo = _split3(x)
    dot = lambda part: jnp.dot(part, sel, preferred_element_type=F32)
    return dot(lo) + dot(mid) + dot(hi)


def _pair_swap(x):
    axis = x.ndim - 1
    lane = lax.broadcasted_iota(jnp.int32, x.shape, axis)
    return jnp.where((lane & 1) == 0, pltpu.roll(x, x.shape[axis] - 1, axis=axis), pltpu.roll(x, 1, axis=axis))


def _rotate(x, cos, sin):
    return x * cos + _pair_swap(x) * sin


def _in_proj_kernel(x_ref, mod_ref, g_ref, wna_ref, wgqa_ref, wmla_ref, wrw_ref, gq_ref, gk_ref,
                    mq_ref, mkv_ref, wuq_ref, wukv_ref, cosh_ref, sinh_ref, cosm_ref, sinm_ref,
                    naq_ref, nak_ref, nav_ref, bq_ref, bk_ref, bv_ref, cq_ref, ck_ref, cv_ref, rw_ref, edge_ref):
    hd, gw = HEAD_DIM, GROUP_WIDTH
    mod = mod_ref[0, 0]
    hb = (_rms(x_ref[0], g_ref[...]) * (1.0 + mod[1:2]) + mod[0:1]).astype(BF16)
    tm = hb.shape[0]
    ones_col = (lax.broadcasted_iota(jnp.int32, (tm, V_EXT_WIDTH - hd), 1) == 0).astype(F32)
    qk_scale = LOG2E * hd ** -0.5

    def put_heads(ref, val, n_heads, width):
        for h in range(n_heads):
            ref[0, h] = val[:, h * width:(h + 1) * width].astype(ref.dtype)

    def put_values(ref, val, n_heads):
        for h in range(n_heads):
            ref[0, h] = jnp.concatenate([val[:, h * hd:(h + 1) * hd], ones_col], axis=-1).astype(ref.dtype)

    na = jnp.dot(hb, wna_ref[...], preferred_element_type=F32)
    put_heads(naq_ref, na[:, :gw] * qk_scale, N_HEADS, hd)
    put_heads(nak_ref, na[:, gw:2 * gw], N_HEADS, hd)
    put_values(nav_ref, na[:, 2 * gw:], N_HEADS)

    gqa = jnp.dot(hb, wgqa_ref[...], preferred_element_type=F32)
    row = lax.broadcasted_iota(jnp.int32, (gw, gw), 0)
    col = lax.broadcasted_iota(jnp.int32, (gw, gw), 1)
    shift = hd.bit_length() - 1
    head_sum = ((row >> shift) == (col >> shift)).astype(F32)
    cos_h, sin_h = cosh_ref[...], sinh_ref[...]
    kw = GQA_KV_HEADS * hd
    q = gqa[:, :gw]
    q = q * lax.rsqrt(_select_cols(q * q, head_sum) * (1.0 / hd) + RMS_EPS) * gq_ref[...]
    put_heads(bq_ref, _rotate(q, cos_h, sin_h) * qk_scale, N_HEADS, hd)
    k = gqa[:, gw:gw + kw]
    k = k * lax.rsqrt(_select_cols(k * k, head_sum[:kw, :kw]) * (1.0 / hd) + RMS_EPS) * gk_ref[...]
    put_heads(bk_ref, _rotate(k, cos_h[:, :kw], sin_h[:, :kw]), GQA_KV_HEADS, hd)
    put_values(bv_ref, gqa[:, gw + kw:], GQA_KV_HEADS)

    mla = jnp.dot(hb, wmla_ref[...], preferred_element_type=F32)
    cos_m, sin_m = cosm_ref[...], sinm_ref[...]
    mla_scale = LOG2E * MLA_QK_DIM ** -0.5
    uq = jnp.dot(_rms(mla[:, :MLA_Q_RANK], mq_ref[...]).astype(BF16), wuq_ref[...], preferred_element_type=F32)
    q_nope = uq[:, :gw] * mla_scale
    q_rope = _rotate(uq[:, gw:], cos_m, sin_m) * mla_scale
    lo = MLA_Q_RANK + MLA_KV_RANK
    ukv = jnp.dot(_rms(mla[:, MLA_Q_RANK:lo], mkv_ref[...]).astype(BF16), wukv_ref[...],
                  preferred_element_type=F32)
    k_rope = _rotate(mla[:, lo:], cos_m, sin_m)[:, :MLA_ROPE_DIM]
    rd = MLA_ROPE_DIM
    for h in range(N_HEADS):
        cq_ref[0, h] = jnp.concatenate([q_nope[:, h * hd:(h + 1) * hd], q_rope[:, h * rd:(h + 1) * rd]],
                                       axis=-1).astype(cq_ref.dtype)
        ck_ref[0, h] = jnp.concatenate([ukv[:, h * hd:(h + 1) * hd], k_rope], axis=-1).astype(ck_ref.dtype)
    put_values(cv_ref, ukv[:, gw:], N_HEADS)

    rw = jnp.dot(hb, wrw_ref[...], preferred_element_type=F32)
    rw_ref[0] = rw
    edge_ref[0, 0] = jnp.concatenate([rw[0:1], rw[tm - 1:tm]], axis=0)


def in_projection(x_all, mod, g, weights, tables, ctx_tiles):
    bsz, n_tok, d = x_all.shape
    tm = TOKEN_TILE
    tok = lambda b, i: (b, i, 0)
    full = lambda b, i: (0, 0)
    heads = lambda n, w: (jax.ShapeDtypeStruct((bsz, n, n_tok, w), BF16),
                          pl.BlockSpec((1, n, tm, w), lambda b, i: (b, 0, i, 0)))
    outs = [heads(N_HEADS, HEAD_DIM), heads(N_HEADS, HEAD_DIM), heads(N_HEADS, V_EXT_WIDTH),
            heads(N_HEADS, HEAD_DIM), heads(GQA_KV_HEADS, HEAD_DIM), heads(GQA_KV_HEADS, V_EXT_WIDTH),
            heads(N_HEADS, MLA_QK_DIM), heads(N_HEADS, MLA_QK_DIM), heads(N_HEADS, V_EXT_WIDTH),
            (jax.ShapeDtypeStruct((bsz, n_tok, RWKV_COLS), F32), pl.BlockSpec((1, tm, RWKV_COLS), tok)),
            (jax.ShapeDtypeStruct((bsz, n_tok // tm, 2, RWKV_COLS), F32),
             pl.BlockSpec((1, 1, 2, RWKV_COLS), lambda b, i: (b, i, 0, 0)))]
    return pl.pallas_call(
        _in_proj_kernel,
        grid=(bsz, n_tok // tm),
        in_specs=[pl.BlockSpec((1, tm, d), tok),
                  pl.BlockSpec((1, 1, 6, d), lambda b, i: (b, jnp.where(i < ctx_tiles, 0, 1), 0, 0)),
                  pl.BlockSpec((1, d), full)]
                 + [pl.BlockSpec(w.shape, full) for w in weights]
                 + [pl.BlockSpec((tm, t.shape[1]), lambda b, i: (i, 0)) for t in tables],
        out_specs=[o[1] for o in outs],
        out_shape=[o[0] for o in outs],
        compiler_params=_params("parallel", "parallel"),
        name="in_projection",
    )(x_all, mod, g, *weights, *tables)


def _out_mlp_kernel(*refs, hidden_tile, final_norm, ctx_tiles, n_branches):
    x_ref = refs[0]
    per_branch = 2 if ctx_tiles else 1
    branch_refs = refs[1:1 + per_branch * n_branches]
    mixd_ref, mod_ref, g2_ref, gf_ref, wout_ref, w1_ref, w2_ref, o_ref = refs[1 + per_branch * n_branches:]
    parts = []
    for j in range(n_branches):
        lat = branch_refs[per_branch * j][0]
        if ctx_tiles:
            lat = jnp.where(pl.program_id(1) < ctx_tiles, branch_refs[per_branch * j + 1][0], lat)
        parts.append(lat)
    parts.append(mixd_ref[0])
    mod = mod_ref[0, 0]
    proj = jnp.zeros(x_ref.shape[1:], F32)
    lo = 0
    for part in parts:
        proj = proj + jnp.dot(part, wout_ref[lo:lo + part.shape[1], :], preferred_element_type=F32)
        lo += part.shape[1]
    x1 = x_ref[0] + mod[2:3] * proj
    hb = (_rms(x1, g2_ref[...]) * (1.0 + mod[4:5]) + mod[3:4]).astype(BF16)
    acc = jnp.zeros_like(x1)
    for j in range(w1_ref.shape[1] // hidden_tile):
        cols = pl.ds(j * hidden_tile, hidden_tile)
        u = jnp.maximum(jnp.dot(hb, w1_ref[:, cols], preferred_element_type=F32), 0.0)
        acc = acc + jnp.dot((u * u).astype(BF16), w2_ref[cols, :], preferred_element_type=F32)
    x2 = x1 + mod[5:6] * acc
    if final_norm:
        x2 = _rms(x2, gf_ref[...])
    o_ref[0] = x2


def out_projection_mlp(x_all, branches, mix_d, mod, g2, g_final, w_out, w_fc1, w_fc2, ctx_tiles, last):
    bsz, n_tok, d = x_all.shape
    tm = TOKEN_TILE
    n_tiles = n_tok // tm
    skip = ctx_tiles if last else 0
    tok = lambda b, i: (b, i + skip, 0)
    full = lambda b, i: (0, 0)
    lat_map = lambda b, i: (b, jnp.maximum(i + skip - ctx_tiles, 0), 0)
    ctx_map = lambda b, i: (b, jnp.minimum(i, ctx_tiles - 1), 0)
    branch_args, branch_specs = [], []
    for lat, ctx_rows in branches:
        branch_args.append(lat)
        branch_specs.append(pl.BlockSpec((1, tm, lat.shape[2]), lat_map))
        if not last:
            branch_args.append(ctx_rows)
            branch_specs.append(pl.BlockSpec((1, tm, ctx_rows.shape[2]), ctx_map))
    kern = functools.partial(_out_mlp_kernel, hidden_tile=1024, final_norm=last,
                             ctx_tiles=0 if last else ctx_tiles, n_branches=len(branches))
    return pl.pallas_call(
        kern,
        grid=(bsz, n_tiles - skip),
        in_specs=[pl.BlockSpec((1, tm, d), tok)] + branch_specs
                 + [pl.BlockSpec((1, tm, mix_d.shape[2]), tok),
                    pl.BlockSpec((1, 1, 6, d), lambda b, i: (b, jnp.where(i + skip < ctx_tiles, 0, 1), 0, 0)),
                    pl.BlockSpec((1, d), full),
                    pl.BlockSpec((1, d), full),
                    pl.BlockSpec(w_out.shape, full, pipeline_mode=pl.Buffered(1)),
                    pl.BlockSpec(w_fc1.shape, full, pipeline_mode=pl.Buffered(1)),
                    pl.BlockSpec(w_fc2.shape, full, pipeline_mode=pl.Buffered(1))],
        out_specs=pl.BlockSpec((1, tm, d), lambda b, i: (b, i, 0)),
        out_shape=jax.ShapeDtypeStruct((bsz, n_tok - skip * tm, d), F32),
        compiler_params=_params("parallel", "parallel"),
        name="out_projection_mlp",
    )(x_all, *branch_args, mix_d, mod, g2, g_final, w_out, w_fc1, w_fc2)


def _key_chunks(n_keys):
    first = min(n_keys, ATTN_Q_TILE)
    chunks = [(0, first)]
    while chunks[-1][0] + chunks[-1][1] < n_keys:
        start = chunks[-1][0] + chunks[-1][1]
        chunks.append((start, min(ATTN_KV_CHUNK, n_keys - start)))
    return chunks


def _online_softmax(q, k_at, v_at, chunks):
    rows = q.shape[0]
    scores = lambda c: _bdot_nt(q, k_at(*c))
    m = jnp.full((rows, 1), NEG_INF, F32)
    acc = jnp.zeros((rows, V_EXT_WIDTH), F32)
    s_next = scores(chunks[0])
    for j, chunk in enumerate(chunks):
        s = s_next
        if j + 1 < len(chunks):
            s_next = scores(chunks[j + 1])
        m_new = jnp.maximum(m, jnp.max(s, axis=-1, keepdims=True))
        p = jnp.exp2(s - m_new)
        acc = acc * jnp.exp2(m - m_new) + jnp.dot(p.astype(BF16), v_at(*chunk), preferred_element_type=F32)
        m = m_new
    return acc


def _attention_kernel(q_ref, k_ref, v_ref, o_ref):
    n_q, tq, dk = q_ref.shape[2], q_ref.shape[3], q_ref.shape[4]
    dv = o_ref.shape[-1] // n_q
    chunks = _key_chunks(k_ref.shape[3])
    if k_ref.shape[2] == 1:
        groups = [(q_ref[0, 0].reshape(n_q * tq, dk), 0)]
    else:
        groups = [(q_ref[0, 0, g], g) for g in range(n_q)]
    outs = []
    for q, g in groups:
        acc = _online_softmax(q, lambda s, n: k_ref[0, 0, g, pl.ds(s, n), :],
                              lambda s, n: v_ref[0, 0, g, pl.ds(s, n), :], chunks)
        o = acc[:, :dv] / acc[:, dv:dv + 1]
        outs.extend(o[i * tq:(i + 1) * tq] for i in range(o.shape[0] // tq))
    o_ref[0] = jnp.concatenate(outs, axis=-1).astype(o_ref.dtype)


def attention(q, k, v, n_keys, dv, tq):
    bsz, n_pairs, n_q, n_rows, dk = q.shape
    n_kv = k.shape[2]
    n_q_tiles = n_rows // tq
    return pl.pallas_call(
        _attention_kernel,
        grid=(bsz, n_pairs, n_q_tiles),
        in_specs=[pl.BlockSpec((1, 1, n_q, tq, dk), lambda b, h, i: (b, h, 0, i, 0)),
                  pl.BlockSpec((1, 1, n_kv, n_keys, dk), lambda b, h, i: (b, h, 0, 0, 0)),
                  pl.BlockSpec((1, 1, n_kv, n_keys, v.shape[-1]), lambda b, h, i: (b, h, 0, 0, 0))],
        out_specs=pl.BlockSpec((1, tq, n_q * dv), lambda b, h, i: (b, i, h)),
        out_shape=jax.ShapeDtypeStruct((bsz, n_q_tiles * tq, n_pairs * n_q * dv), BF16),
        compiler_params=_params("parallel", "parallel", "parallel"),
        name="attention",
    )(q, k, v)


def _na_kernel(q_ref, k_ref, v_ref, bias_ref, o_ref, *, n_ctx, n_row_tiles):
    i = pl.program_id(2)
    n_win = NA_KEY_ROWS * GRID_W
    max_row0 = n_row_tiles * NA_Q_ROWS - NA_KEY_ROWS
    row0 = jnp.clip(i * NA_Q_ROWS - NA_WIN_ROWS // 2, 0, max_row0)
    start = pl.multiple_of(n_ctx + row0 * GRID_W, GRID_W)
    dv = o_ref.shape[-1] // q_ref.shape[2]
    outs = []
    for g in range(q_ref.shape[2]):
        q = q_ref[0, 0, g]
        s_win = _bdot_nt(q, k_ref[0, 0, g, pl.ds(start, n_win), :]) + bias_ref[0, g, 0]
        s_ctx = _bdot_nt(q, k_ref[0, 0, g, pl.ds(0, n_ctx), :])
        m = jnp.maximum(jnp.max(s_win, axis=-1, keepdims=True), jnp.max(s_ctx, axis=-1, keepdims=True))
        p_win = jnp.exp2(s_win - m).astype(BF16)
        p_ctx = jnp.exp2(s_ctx - m).astype(BF16)
        o = (jnp.dot(p_win, v_ref[0, 0, g, pl.ds(start, n_win), :], preferred_element_type=F32)
             + jnp.dot(p_ctx, v_ref[0, 0, g, pl.ds(0, n_ctx), :], preferred_element_type=F32))
        outs.append(o[:, :dv] / o[:, dv:dv + 1])
    o_ref[0] = jnp.concatenate(outs, axis=-1).astype(o_ref.dtype)


def na_bias_table(rpb, rows):
    n_tiles = rows // NA_Q_ROWS
    n_heads, n_dr, n_dc = rpb.shape
    qc = np.arange(GRID_W)[:, None]
    kc = np.arange(GRID_W)[None, :]
    dc = np.clip(kc - qc + NA_WIN_COLS - 1, 0, n_dc - 1)
    onehot = jnp.asarray(dc.reshape(-1)[None, :] == np.arange(n_dc)[:, None], F32)
    by_col = jnp.dot(rpb.reshape(-1, n_dc).astype(F32), onehot, precision=lax.Precision.HIGHEST)
    pad = NA_KEY_ROWS
    by_col = jnp.pad(by_col.reshape(n_heads, n_dr, GRID_W, GRID_W), ((0, 0), (pad, pad), (0, 0), (0, 0)))
    c_start = np.clip(qc - NA_WIN_COLS // 2, 0, GRID_W - NA_WIN_COLS)
    col_ok = (kc >= c_start) & (kc < c_start + NA_WIN_COLS)
    tables = []
    for tile in (0, 1, n_tiles - 1):
        row0 = int(np.clip(tile * NA_Q_ROWS - NA_WIN_ROWS // 2, 0, rows - NA_KEY_ROWS))
        per_row = []
        for a in range(NA_Q_ROWS):
            qr = tile * NA_Q_ROWS + a
            dr0 = row0 - qr + NA_WIN_ROWS - 1 + pad
            kr = row0 + np.arange(NA_KEY_ROWS)
            r_start = int(np.clip(qr - NA_WIN_ROWS // 2, 0, rows - NA_WIN_ROWS))
            row_ok = (kr >= r_start) & (kr < r_start + NA_WIN_ROWS)
            valid = row_ok[None, :, None] & col_ok[:, None, :]
            sl = jnp.swapaxes(by_col[:, dr0:dr0 + NA_KEY_ROWS], 1, 2)
            per_row.append(jnp.where(valid[None], sl * LOG2E, NEG_INF))
        bias = jnp.stack(per_row, axis=1)
        tables.append(bias.reshape(n_heads, NA_Q_ROWS * GRID_W, NA_KEY_ROWS * GRID_W))
    return jnp.stack(tables, axis=1)


def neighbourhood_attention(q, k, v, bias, n_ctx):
    bsz, n_pairs, n_q, n_tok, d = q.shape
    tq = ATTN_Q_TILE
    n_tiles = (n_tok - n_ctx) // tq
    ctx_tiles = n_ctx // tq
    kern = functools.partial(_na_kernel, n_ctx=n_ctx, n_row_tiles=n_tiles)

    def bias_map(b, h, i):
        return (h, 0, jnp.where(i == 0, 0, jnp.where(i == n_tiles - 1, 2, 1)), 0, 0)

    whole = lambda b, h, i: (b, h, 0, 0, 0)
    return pl.pallas_call(
        kern,
        grid=(bsz, n_pairs, n_tiles),
        in_specs=[pl.BlockSpec((1, 1, n_q, tq, d), lambda b, h, i: (b, h, 0, i + ctx_tiles, 0)),
                  pl.BlockSpec((1, 1, n_q, n_tok, d), whole),
                  pl.BlockSpec((1, 1, n_q, n_tok, v.shape[-1]), whole),
                  pl.BlockSpec((1, n_q, 1) + bias.shape[3:], bias_map)],
        out_specs=pl.BlockSpec((1, tq, n_q * d), lambda b, h, i: (b, i, h)),
        out_shape=jax.ShapeDtypeStruct((bsz, n_tok - n_ctx, n_pairs * n_q * d), BF16),
        compiler_params=_params("parallel", "parallel", "parallel"),
        name="neighbourhood_attention",
    )(q, k, v, bias)


def _softplus(z):
    return jnp.maximum(z, 0.0) + jnp.log(1.0 + jnp.exp(-jnp.abs(z)))


def _sigmoid(z):
    return 1.0 / (1.0 + jnp.exp(-z))


def _rwkv_prep_kernel(u_ref, halo_ref, taps_ref, w0_ref, w2_ref, a0_ref, a2_ref, kk_ref, ka_ref, rk_ref,
                      g2_ref, *out_refs):
    dir_refs = (out_refs[0:4], out_refs[4:8])
    v_out, gt_refs, bonus_ref, gate_ref = out_refs[8], out_refs[9:11], out_refs[11], out_refs[12]
    tm = u_ref.shape[1]
    c = RWKV_CHUNK
    n_chunks = tm // c
    gw = GROUP_WIDTH
    u = u_ref[0]
    halo = halo_ref[0, 0]
    tok = lax.broadcasted_iota(jnp.int32, u.shape, 0)
    prev = jnp.where(tok == 0, halo[0:1], pltpu.roll(u, 1, axis=0))
    nxt = jnp.where(tok == tm - 1, halo[1:2], pltpu.roll(u, tm - 1, axis=0))
    taps = taps_ref[...]
    s = prev * taps[0:1] + u * taps[1:2] + nxt * taps[2:3]

    r, k, v = s[:, 0:gw], s[:, gw:2 * gw], s[:, 2 * gw:3 * gw]
    lo = 3 * gw
    w_low = jnp.tanh(s[:, lo:lo + 2 * RWKV_DECAY_RANK])
    lo += 2 * RWKV_DECAY_RANK
    a_low = s[:, lo:lo + 2 * RWKV_ICLR_RANK]
    lo += 2 * RWKV_ICLR_RANK
    g_low = _sigmoid(s[:, lo:lo + RWKV_GATE_RANK])
    w_log = -_softplus(-(w0_ref[...] + _bdot(w_low, w2_ref[...]))) - 0.5
    log_decay = -jnp.exp(w_log)
    iclr = _sigmoid(a0_ref[...] + _bdot(a_low, a2_ref[...]))
    gate_ref[0] = _bdot(g_low, g2_ref[...])

    row = lax.broadcasted_iota(jnp.int32, (tm, tm), 0)
    col = lax.broadcasted_iota(jnp.int32, (tm, tm), 1)
    shift = c.bit_length() - 1
    same = (row >> shift) == (col >> shift)
    head_shift = HEAD_DIM.bit_length() - 1
    ones_blk = ((row >> head_shift) == (col >> head_shift)).astype(F32)

    v_out[0] = v.astype(v_out.dtype)
    bonus = jnp.zeros_like(r)
    for d in range(2):
        sl = slice(d * gw, (d + 1) * gw)
        lw = log_decay[:, sl]
        kk = k * kk_ref[:, sl]
        kk = kk * lax.rsqrt(jnp.maximum(_select_cols(kk * kk, ones_blk), 1e-24))
        k_d = k * (1.0 + (iclr[:, sl] - 1.0) * ka_ref[:, sl])
        b_vec = kk * iclr[:, sl]
        tri = (same & ((row >= col) if d == 0 else (row <= col))).astype(F32)
        cum = _select_rows(tri, lw)
        ends = [cum[i * c + (c - 1 if d == 0 else 0)][None] for i in range(n_chunks)]
        e_inv = jnp.exp(-cum)
        streams = (r * jnp.exp(cum), -kk * jnp.exp(cum - lw), b_vec * e_inv, k_d * e_inv)
        for ref, val in zip(dir_refs[d], streams):
            ref[0] = val.astype(ref.dtype)
        gt_refs[d][0, 0] = jnp.exp(jnp.concatenate(ends, axis=0))
        bonus = bonus + r * k_d * rk_ref[:, sl]
    bonus_ref[0] = _select_cols(bonus, ones_blk) * v


def rwkv_prepare(rw, halo, taps, w0, w2, a0, a2, k_k, k_a, r_k, g2):
    bsz, n_tok, width = rw.shape
    tm = TOKEN_TILE
    n_tiles = n_tok // tm
    n_chunks = tm // RWKV_CHUNK
    full = lambda b, i: (0, 0)
    consts = (taps, w0, w2, a0, a2, k_k, k_a, r_k, g2)
    assert tm == GROUP_WIDTH
    tokens_spec = pl.BlockSpec((1, tm, GROUP_WIDTH), lambda b, i: (b, i, 0))
    stream = jax.ShapeDtypeStruct((bsz, n_tok, GROUP_WIDTH), BF16)
    stream_spec = tokens_spec
    decay = jax.ShapeDtypeStruct((bsz, n_tiles, n_chunks, GROUP_WIDTH), F32)
    decay_spec = pl.BlockSpec((1, 1, n_chunks, GROUP_WIDTH), lambda b, i: (b, i, 0, 0))
    tokens = jax.ShapeDtypeStruct((bsz, n_tok, GROUP_WIDTH), F32)
    return pl.pallas_call(
        _rwkv_prep_kernel,
        grid=(bsz, n_tiles),
        in_specs=[pl.BlockSpec((1, tm, width), lambda b, i: (b, i, 0)),
                  pl.BlockSpec((1, 1, 2, width), lambda b, i: (b, i, 0, 0))]
                 + [pl.BlockSpec(t.shape, full) for t in consts],
        out_specs=[stream_spec] * 9 + [decay_spec] * 2 + [tokens_spec] * 2,
        out_shape=[stream] * 9 + [decay] * 2 + [tokens] * 2,
        compiler_params=_params("parallel", "parallel"),
        name="rwkv_prepare",
    )(rw, halo, *consts)


def _bmm(a, b):
    return lax.dot_general(a.astype(BF16), b.astype(BF16), (((2,), (1,)), ((0,), (0,))),
                           preferred_element_type=F32)


def _bmm_nt(a, b):
    return lax.dot_general(a.astype(BF16), b.astype(BF16), (((2,), (2,)), ((0,), (0,))),
                           preferred_element_type=F32)


def _bmm_tn(a, b):
    return lax.dot_general(a.astype(BF16), b.astype(BF16), (((1,), (1,)), ((0,), (0,))),
                           preferred_element_type=F32)


def _rwkv_chunk_kernel(rt_ref, at_ref, bt_ref, kt_ref, v_ref, gt_ref, *rest, reverse):
    y_ref, s_ref = rest[-2:]
    other_ref, bonus_ref, gate_ref, lw_ref, lb_ref = rest[:5] if len(rest) > 2 else (None,) * 5
    n_rows, tb, width = v_ref.shape
    n = width // 2
    c = RWKV_CHUNK
    n_chunks = tb // c
    g = n_rows * n_chunks

    @pl.when(pl.program_id(2) == 0)
    def _():
        s_ref[...] = jnp.zeros_like(s_ref)

    load = lambda ref: ref[...].reshape(g, c, width)
    rt, at, bt, kt, v = (load(ref) for ref in (rt_ref, at_ref, bt_ref, kt_ref, v_ref))
    left = lax.broadcasted_iota(jnp.int32, (1, 1, width), 2) < n
    cat = lambda a, b: jnp.concatenate([a, b], axis=1)

    def bd(y):
        y = y.astype(BF16)
        zero = jnp.zeros_like(y)
        return cat(jnp.where(left, y, zero), jnp.where(left, zero, y))

    row = lax.broadcasted_iota(jnp.int32, (c, 2 * c), 0)
    col = lax.broadcasted_iota(jnp.int32, (c, 2 * c), 1) & (c - 1)
    if reverse:
        row, col = col, row
    strict = row > col
    incl = row >= col
    ar = cat(at, rt)
    ab_rb = _bmm_nt(ar, bd(bt))
    ak_rk = _bmm_nt(ar, bd(kt))
    a_ab = jnp.where(strict, ab_rb[:, :c], 0.0)
    a_rb = jnp.where(incl, ab_rb[:, c:], 0.0)
    a_ak = jnp.where(strict, ak_rk[:, :c], 0.0)
    a_rk = jnp.where(incl, ak_rk[:, c:], 0.0)
    v_bd = bd(v)
    av = _bmm(a_ak, v_bd)
    n1 = jnp.where((row >> 3) == (col >> 3), a_ab, 0.0)
    n2 = _bmm(n1, bd(n1))
    n2_bd = bd(n2)
    n4 = _bmm(n2, n2_bd)
    p = (row == col).astype(F32) + n1 + n2 + _bmm(n1, n2_bd)
    inv = p + _bmm(p, bd(n4))
    for level in range(3, c.bit_length() - 1):
        rb, cb = row >> level, col >> level
        off = jnp.where((rb == cb + 1) & ((rb & 1) == 1), a_ab, 0.0)
        inv = inv + _bmm(_bmm(inv, bd(off)), bd(inv))
    w = _bmm(inv, bd(at))
    u0 = _bmm(inv, bd(av))
    q_eff = rt.astype(F32) + _bmm(a_rb, bd(w))
    y0 = _bmm(jnp.concatenate([a_rb, a_rk], axis=2), cat(bd(u0), v_bd))

    per_chunk = lambda t: t.reshape((n_rows, n_chunks) + t.shape[1:])
    qw = per_chunk(cat(q_eff, w))
    u0, y0, v = per_chunk(u0), per_chunk(y0), per_chunk(v)
    g_tot = gt_ref[:, 0][:, :, None, :]
    bk_end = per_chunk(cat(bt, kt)) * g_tot
    state = s_ref[...]
    for i in (range(n_chunks - 1, -1, -1) if reverse else range(n_chunks)):
        tokens = slice(i * c, (i + 1) * c)
        yu = _bmm_nt(qw[:, i], bd(state))
        y = yu[:, :c] + y0[:, i]
        if other_ref is not None:
            y = _rwkv_output(y + other_ref[:, tokens, :], bonus_ref[:, tokens, :], gate_ref[:, tokens, :],
                             lw_ref[...], lb_ref[...])
        y_ref[:, tokens, :] = y.astype(y_ref.dtype)
        full = _bmm_tn(cat(yu[:, c:] + u0[:, i], v[:, i]), bk_end[:, i])
        state = g_tot[:, i] * state + jnp.where(left, full[:, :n], full[:, n:])
    s_ref[...] = state


def _rwkv_output(y, bonus, gate, lnx_w, lnx_b):
    shape = y.shape
    y = y.reshape(-1, shape[-1])
    row = lax.broadcasted_iota(jnp.int32, (shape[-1], shape[-1]), 0)
    col = lax.broadcasted_iota(jnp.int32, (shape[-1], shape[-1]), 1)
    shift = HEAD_DIM.bit_length() - 1
    head_sum = ((row >> shift) == (col >> shift)).astype(F32)
    mu = _select_cols(y, head_sum) * (1.0 / HEAD_DIM)
    var = _select_cols(jnp.square(y - mu), head_sum) * (1.0 / HEAD_DIM)
    yn = (y - mu) * lax.rsqrt(var + GN_EPS) * lnx_w + lnx_b
    return (yn.reshape(shape) + bonus) * gate


def rwkv_chunk_scan(streams, v, g_tot, ctx_blocks, reverse, finish=None):
    bsz, n_tok, width = v.shape
    tb = TOKEN_TILE
    n_blocks = n_tok // tb
    pair = 2 * HEAD_DIM

    def block(i):
        if not reverse:
            return i
        return jnp.where(i < ctx_blocks, ctx_blocks - 1 - i, n_blocks - 1 + ctx_blocks - i)

    rows = math.gcd(RWKV_ROWS_PER_STEP, bsz)
    spec = pl.BlockSpec((rows, tb, pair), lambda b, p, i: (b, block(i), p))
    in_specs = [spec] * 5 + [pl.BlockSpec((rows, 1, g_tot.shape[2], pair), lambda b, p, i: (b, block(i), 0, p))]
    args = (*streams, v, g_tot)
    if finish is not None:
        vec = pl.BlockSpec((1, pair), lambda b, p, i: (0, p))
        in_specs += [spec, spec, spec, vec, vec]
        args += tuple(finish)
    return pl.pallas_call(
        functools.partial(_rwkv_chunk_kernel, reverse=reverse),
        grid=(bsz // rows, width // pair, n_blocks),
        in_specs=in_specs,
        out_specs=spec,
        out_shape=jax.ShapeDtypeStruct((bsz, n_tok, width), F32 if finish is None else BF16),
        scratch_shapes=[pltpu.VMEM((rows, HEAD_DIM, pair), F32)],
        compiler_params=_params("parallel", "parallel", "arbitrary"),
        name="rwkv_chunk_scan_bwd" if reverse else "rwkv_chunk_scan_fwd",
    )(*args)


def _rope_tables(n_ctx, n_lat, rot_dim, n_rep):
    t = jnp.arange(n_lat, dtype=jnp.int32)
    rows = (t // GRID_W).astype(F32)
    cols = (t % GRID_W).astype(F32)
    per_axis = rot_dim // 2
    inv_freq = ROPE_THETA ** (-jnp.arange(0, per_axis, 2, dtype=F32) / per_axis)
    ang = jnp.concatenate([rows[:, None] * inv_freq, cols[:, None] * inv_freq], axis=-1)
    ang = jnp.concatenate([jnp.zeros((n_ctx, ang.shape[1]), F32), ang], axis=0)
    sign = jnp.tile(jnp.array([-1.0, 1.0], F32), rot_dim // 2)
    cos = jnp.repeat(jnp.cos(ang), 2, axis=-1)
    sin = jnp.repeat(jnp.sin(ang), 2, axis=-1) * sign
    return jnp.tile(cos, (1, n_rep)), jnp.tile(sin, (1, n_rep))


def _pairs(t):
    return t.reshape((t.shape[0], t.shape[1] // 2, 2) + t.shape[2:])


def _block_diag(m0, m1):
    z01 = jnp.zeros((m0.shape[0], m1.shape[1]), m0.dtype)
    z10 = jnp.zeros((m1.shape[0], m0.shape[1]), m0.dtype)
    return jnp.concatenate([jnp.concatenate([m0, z01], axis=1), jnp.concatenate([z10, m1], axis=1)], axis=0)


def _shift_halo(edge, ctx_tiles):
    zero = jnp.zeros_like(edge[:, :1, 0])
    prev = jnp.concatenate([zero, edge[:, :-1, 1]], axis=1)
    nxt = jnp.concatenate([edge[:, 1:, 0], zero], axis=1)
    tile = jnp.arange(prev.shape[1])[None, :, None]
    prev = jnp.where(tile == ctx_tiles, 0.0, prev)
    nxt = jnp.where(tile == ctx_tiles - 1, 0.0, nxt)
    return jnp.stack([prev, nxt], axis=2)


def _rwkv_mixer(rw, rw_edge, ctx_tiles, shift_taps, w0, w2, a0, a2, k_k, k_a, r_k, g2, lnx_w, lnx_b):
    both = lambda t: t.reshape(1, -1)
    outs = rwkv_prepare(rw, _shift_halo(rw_edge, ctx_tiles), shift_taps, both(w0),
                        _block_diag(w2[0], w2[1]).astype(BF16), both(a0),
                        _block_diag(a2[0], a2[1]).astype(BF16), both(k_k), both(k_a), both(r_k),
                        g2.astype(BF16))
    v, bonus, gate = outs[8], outs[11], outs[12]
    y_f = rwkv_chunk_scan(outs[0:4], v, outs[9], ctx_tiles, False)
    return rwkv_chunk_scan(outs[4:8], v, outs[10], ctx_tiles, True,
                           finish=(y_f, bonus, gate, lnx_w[None], lnx_b[None]))


def _dense_branch(q, k, v, n_ctx, want_ctx):
    n_tok = q.shape[3]
    o_lat = attention(q[:, :, :, n_ctx:], k, v, n_tok, HEAD_DIM, ATTN_LAT_Q_TILE)
    o_ctx = attention(q[:, :, :, :n_ctx], k, v, n_ctx, HEAD_DIM, ATTN_Q_TILE) if want_ctx else None
    return o_lat, o_ctx


def kernel(x, c, ctx, c_ctx, w_mod, b_mod, norm1_g, norm2_g, w_in, rwkv_shift, na_rpb, gqa_q_norm, gqa_k_norm, mla_q_norm, mla_kv_norm, mla_w_uq, mla_w_ukv, rwkv_w0, rwkv_w2, rwkv_a0, rwkv_a2, rwkv_k_k, rwkv_k_a, rwkv_r_k, rwkv_g2, rwkv_lnx_w, rwkv_lnx_b, w_out, w_fc1, w_fc2, final_norm_g):
    bsz, n_lat, d_model = x.shape
    n_ctx = ctx.shape[1]
    depth = w_mod.shape[0]
    rows = n_lat // GRID_W
    ctx_tiles = n_ctx // TOKEN_TILE
    assert n_ctx % TOKEN_TILE == 0 and n_lat % TOKEN_TILE == 0 and rows % NA_Q_ROWS == 0

    tables = (_rope_tables(n_ctx, n_lat, HEAD_DIM, N_HEADS) + _rope_tables(n_ctx, n_lat, MLA_ROPE_DIM, N_HEADS))

    cond = jnp.concatenate([jax.nn.silu(c), jax.nn.silu(c_ctx)[None]], axis=0)
    n_cond = -(-cond.shape[0] // 8) * 8
    cond = jnp.pad(cond, ((0, n_cond - cond.shape[0]), (0, 0)))
    mod_all = modulation(cond, w_mod, b_mod[:, None, :])

    x_all = jnp.concatenate([ctx, x], axis=1)
    bounds = np.cumsum([0, NA_COLS, GQA_COLS, MLA_COLS, RWKV_COLS])
    mla_pad = -MLA_COLS % 128

    for layer in range(depth):
        want_ctx = layer < depth - 1
        mod = mod_all[layer].reshape(n_cond, 6, d_model)
        mod = jnp.stack([jnp.broadcast_to(mod[bsz], (bsz, 6, d_model)), mod[:bsz]], axis=1)

        w_na, w_gqa, w_mla, w_rw = [w_in[layer][:, bounds[i]:bounds[i + 1]].astype(BF16) for i in range(4)]
        w_uq = mla_w_uq[layer].reshape(MLA_Q_RANK, N_HEADS, MLA_QK_DIM)
        w_uq = jnp.concatenate([w_uq[:, :, :HEAD_DIM].reshape(MLA_Q_RANK, -1),
                                w_uq[:, :, HEAD_DIM:].reshape(MLA_Q_RANK, -1)], axis=1)
        w_ukv = mla_w_ukv[layer].reshape(MLA_KV_RANK, N_HEADS, 2 * HEAD_DIM)
        w_ukv = jnp.concatenate([w_ukv[:, :, :HEAD_DIM].reshape(MLA_KV_RANK, -1),
                                 w_ukv[:, :, HEAD_DIM:].reshape(MLA_KV_RANK, -1)], axis=1)
        weights = (w_na, w_gqa, jnp.pad(w_mla, ((0, 0), (0, mla_pad))), w_rw,
                   jnp.tile(gqa_q_norm[layer], N_HEADS)[None], jnp.tile(gqa_k_norm[layer], GQA_KV_HEADS)[None],
                   mla_q_norm[layer][None], mla_kv_norm[layer][None], w_uq.astype(BF16), w_ukv.astype(BF16))
        (a_q, a_k, a_v, b_q, b_k, b_v, c_q, c_k, c_v, rw, rw_edge) = in_projection(
            x_all, mod, norm1_g[layer][None], weights, tables, ctx_tiles)

        a_q, a_k, a_v = _pairs(a_q), _pairs(a_k), _pairs(a_v)
        bias = na_bias_table(na_rpb[layer], rows)
        out_a = neighbourhood_attention(a_q, a_k, a_v, bias.reshape((N_HEADS // 2, 2) + bias.shape[1:]), n_ctx)
        out_ac = attention(a_q[:, :, :, :n_ctx], a_k, a_v, n_ctx, HEAD_DIM, ATTN_Q_TILE) if want_ctx else None
        mix_b = _dense_branch(_pairs(b_q), b_k[:, :, None], b_v[:, :, None], n_ctx, want_ctx)
        mix_c = _dense_branch(_pairs(c_q), _pairs(c_k), _pairs(c_v), n_ctx, want_ctx)
        mix_d = _rwkv_mixer(rw, rw_edge, ctx_tiles, rwkv_shift[layer], rwkv_w0[layer], rwkv_w2[layer], rwkv_a0[layer],
                            rwkv_a2[layer], rwkv_k_k[layer], rwkv_k_a[layer], rwkv_r_k[layer],
                            rwkv_g2[layer], rwkv_lnx_w[layer], rwkv_lnx_b[layer])

        x_all = out_projection_mlp(x_all, [(out_a, out_ac), mix_b, mix_c], mix_d, mod, norm2_g[layer][None],
                                   final_norm_g[None], w_out[layer].astype(BF16), w_fc1[layer].astype(BF16),
                                   w_fc2[layer].astype(BF16), ctx_tiles, not want_ctx)
    return x_all
```

```python
import functools
import math

import numpy as np
import jax
import jax.numpy as jnp
from jax import lax
from jax.experimental import pallas as pl
from jax.experimental.pallas import tpu as pltpu

F32 = jnp.float32
BF16 = jnp.bfloat16

GRID_W = 64
HEAD_DIM = 64
N_HEADS = 4
GROUP_WIDTH = N_HEADS * HEAD_DIM
GQA_KV_HEADS = 2
NA_WIN_ROWS = 8
NA_WIN_COLS = 16
MLA_Q_RANK = 256
MLA_KV_RANK = 128
MLA_ROPE_DIM = 32
MLA_QK_DIM = HEAD_DIM + MLA_ROPE_DIM
RWKV_DECAY_RANK = 64
RWKV_ICLR_RANK = 64
RWKV_GATE_RANK = 160
NA_COLS = 3 * GROUP_WIDTH
GQA_COLS = (N_HEADS + 2 * GQA_KV_HEADS) * HEAD_DIM
MLA_COLS = MLA_Q_RANK + MLA_KV_RANK + MLA_ROPE_DIM
RWKV_COLS = 3 * GROUP_WIDTH + 2 * RWKV_DECAY_RANK + 2 * RWKV_ICLR_RANK + RWKV_GATE_RANK
ROPE_THETA = 10000.0
RMS_EPS = 1e-6
GN_EPS = 64e-5
NEG_INF = -1e30

TOKEN_TILE = 256
ATTN_Q_TILE = 256
ATTN_LAT_Q_TILE = 1024
ATTN_KV_CHUNK = 1024
V_EXT_WIDTH = 128
LOG2E = float(np.log2(np.e))
NA_Q_ROWS = ATTN_Q_TILE // GRID_W
NA_KEY_ROWS = NA_Q_ROWS + NA_WIN_ROWS - 1
RWKV_CHUNK = 64
RWKV_ROWS_PER_STEP = 8
VMEM_LIMIT = 56 * 1024 * 1024


def _params(*sem):
    return pltpu.CompilerParams(dimension_semantics=sem, vmem_limit_bytes=VMEM_LIMIT)


def _bdot(a, b):
    return jnp.dot(a.astype(BF16), b.astype(BF16), preferred_element_type=F32)


def _bdot_nt(a, b):
    return lax.dot_general(a.astype(BF16), b.astype(BF16), (((1,), (1,)), ((), ())),
                           preferred_element_type=F32)


def _mod_kernel(a_ref, w_ref, b_ref, o_ref):
    o_ref[0] = _bdot(a_ref[...], w_ref[0]) + b_ref[0]


def modulation(cond, w_mod, b_mod, tn=1536):
    depth, d, n = w_mod.shape
    r = cond.shape[0]
    return pl.pallas_call(
        _mod_kernel,
        grid=(depth, n // tn),
        in_specs=[pl.BlockSpec((r, d), lambda l, j: (0, 0)),
                  pl.BlockSpec((1, d, tn), lambda l, j: (l, 0, j)),
                  pl.BlockSpec((1, 1, tn), lambda l, j: (l, 0, j))],
        out_specs=pl.BlockSpec((1, r, tn), lambda l, j: (l, 0, j)),
        out_shape=jax.ShapeDtypeStruct((depth, r, n), F32),
        compiler_params=_params("parallel", "parallel"),
        name="modulation",
    )(cond, w_mod, b_mod)


def _rms(x, g):
    return x * lax.rsqrt(jnp.mean(x * x, axis=-1, keepdims=True) + RMS_EPS) * g


def _split3(x):
    hi = x.astype(BF16)
    rest = x - hi.astype(F32)
    mid = rest.astype(BF16)
    return hi, mid, (rest - mid.astype(F32)).astype(BF16)


def _select_rows(sel, x):
    sel = sel.astype(BF16)
    hi, mid, lo = _split3(x)
    dot = lambda part: jnp.dot(sel, part, preferred_element_type=F32)
    return dot(lo) + dot(mid) + dot(hi)


def _select_cols(x, sel):
    sel = sel.astype(BF16)
    hi, mid, lo = _split3(x)
    dot = lambda part: jnp.dot(part, sel, preferred_element_type=F32)
    return dot(lo) + dot(mid) + dot(hi)


def _pair_swap(x):
    axis = x.ndim - 1
    lane = lax.broadcasted_iota(jnp.int32, x.shape, axis)
    return jnp.where((lane & 1) == 0, pltpu.roll(x, x.shape[axis] - 1, axis=axis), pltpu.roll(x, 1, axis=axis))


def _rotate(x, cos, sin):
    return x * cos + _pair_swap(x) * sin


def _in_proj_kernel(x_ref, mod_ref, g_ref, wna_ref, wgqa_ref, wmla_ref, wrw_ref, gq_ref, gk_ref,
                    mq_ref, mkv_ref, wuq_ref, wukv_ref, cosh_ref, sinh_ref, cosm_ref, sinm_ref,
                    naq_ref, nak_ref, nav_ref, bq_ref, bk_ref, bv_ref, cq_ref, ck_ref, cv_ref, rw_ref, edge_ref):
    hd, gw = HEAD_DIM, GROUP_WIDTH
    mod = mod_ref[0, 0]
    hb = (_rms(x_ref[0], g_ref[...]) * (1.0 + mod[1:2]) + mod[0:1]).astype(BF16)
    tm = hb.shape[0]
    ones_col = (lax.broadcasted_iota(jnp.int32, (tm, V_EXT_WIDTH - hd), 1) == 0).astype(F32)
    qk_scale = LOG2E * hd ** -0.5

    def put_heads(ref, val, n_heads, width):
        for h in range(n_heads):
            ref[0, h] = val[:, h * width:(h + 1) * width].astype(ref.dtype)

    def put_values(ref, val, n_heads):
        for h in range(n_heads):
            ref[0, h] = jnp.concatenate([val[:, h * hd:(h + 1) * hd], ones_col], axis=-1).astype(ref.dtype)

    na = jnp.dot(hb, wna_ref[...], preferred_element_type=F32)
    put_heads(naq_ref, na[:, :gw] * qk_scale, N_HEADS, hd)
    put_heads(nak_ref, na[:, gw:2 * gw], N_HEADS, hd)
    put_values(nav_ref, na[:, 2 * gw:], N_HEADS)

    gqa = jnp.dot(hb, wgqa_ref[...], preferred_element_type=F32)
    row = lax.broadcasted_iota(jnp.int32, (gw, gw), 0)
    col = lax.broadcasted_iota(jnp.int32, (gw, gw), 1)
    shift = hd.bit_length() - 1
    head_sum = ((row >> shift) == (col >> shift)).astype(F32)
    cos_h, sin_h = cosh_ref[...], sinh_ref[...]
    kw = GQA_KV_HEADS * hd
    q = gqa[:, :gw]
    q = q * lax.rsqrt(_select_cols(q * q, head_sum) * (1.0 / hd) + RMS_EPS) * gq_ref[...]
    put_heads(bq_ref, _rotate(q, cos_h, sin_h) * qk_scale, N_HEADS, hd)
    k = gqa[:, gw:gw + kw]
    k = k * lax.rsqrt(_select_cols(k * k, head_sum[:kw, :kw]) * (1.0 / hd) + RMS_EPS) * gk_ref[...]
    put_heads(bk_ref, _rotate(k, cos_h[:, :kw], sin_h[:, :kw]), GQA_KV_HEADS, hd)
    put_values(bv_ref, gqa[:, gw + kw:], GQA_KV_HEADS)

    mla = jnp.dot(hb, wmla_ref[...], preferred_element_type=F32)
    cos_m, sin_m = cosm_ref[...], sinm_ref[...]
    mla_scale = LOG2E * MLA_QK_DIM ** -0.5
    uq = jnp.dot(_rms(mla[:, :MLA_Q_RANK], mq_ref[...]).astype(BF16), wuq_ref[...], preferred_element_type=F32)
    q_nope = uq[:, :gw] * mla_scale
    q_rope = _rotate(uq[:, gw:], cos_m, sin_m) * mla_scale
    lo = MLA_Q_RANK + MLA_KV_RANK
    ukv = jnp.dot(_rms(mla[:, MLA_Q_RANK:lo], mkv_ref[...]).astype(BF16), wukv_ref[...],
                  preferred_element_type=F32)
    k_rope = _rotate(mla[:, lo:], cos_m, sin_m)[:, :MLA_ROPE_DIM]
    rd = MLA_ROPE_DIM
    for h in range(N_HEADS):
        cq_ref[0, h] = jnp.concatenate([q_nope[:, h * hd:(h + 1) * hd], q_rope[:, h * rd:(h + 1) * rd]],
                                       axis=-1).astype(cq_ref.dtype)
        ck_ref[0, h] = jnp.concatenate([ukv[:, h * hd:(h + 1) * hd], k_rope], axis=-1).astype(ck_ref.dtype)
    put_values(cv_ref, ukv[:, gw:], N_HEADS)

    rw = jnp.dot(hb, wrw_ref[...], preferred_element_type=F32)
    rw_ref[0] = rw
    edge_ref[0, 0] = jnp.concatenate([rw[0:1], rw[tm - 1:tm]], axis=0)


def in_projection(x_all, mod, g, weights, tables, ctx_tiles):
    bsz, n_tok, d = x_all.shape
    tm = TOKEN_TILE
    tok = lambda b, i: (b, i, 0)
    full = lambda b, i: (0, 0)
    heads = lambda n, w: (jax.ShapeDtypeStruct((bsz, n, n_tok, w), BF16),
                          pl.BlockSpec((1, n, tm, w), lambda b, i: (b, 0, i, 0)))
    outs = [heads(N_HEADS, HEAD_DIM), heads(N_HEADS, HEAD_DIM), heads(N_HEADS, V_EXT_WIDTH),
            heads(N_HEADS, HEAD_DIM), heads(GQA_KV_HEADS, HEAD_DIM), heads(GQA_KV_HEADS, V_EXT_WIDTH),
            heads(N_HEADS, MLA_QK_DIM), heads(N_HEADS, MLA_QK_DIM), heads(N_HEADS, V_EXT_WIDTH),
            (jax.ShapeDtypeStruct((bsz, n_tok, RWKV_COLS), F32), pl.BlockSpec((1, tm, RWKV_COLS), tok)),
            (jax.ShapeDtypeStruct((bsz, n_tok // tm, 2, RWKV_COLS), F32),
             pl.BlockSpec((1, 1, 2, RWKV_COLS), lambda b, i: (b, i, 0, 0)))]
    return pl.pallas_call(
        _in_proj_kernel,
        grid=(bsz, n_tok // tm),
        in_specs=[pl.BlockSpec((1, tm, d), tok),
                  pl.BlockSpec((1, 1, 6, d), lambda b, i: (b, jnp.where(i < ctx_tiles, 0, 1), 0, 0)),
                  pl.BlockSpec((1, d), full)]
                 + [pl.BlockSpec(w.shape, full) for w in weights]
                 + [pl.BlockSpec((tm, t.shape[1]), lambda b, i: (i, 0)) for t in tables],
        out_specs=[o[1] for o in outs],
        out_shape=[o[0] for o in outs],
        compiler_params=_params("parallel", "parallel"),
        name="in_projection",
    )(x_all, mod, g, *weights, *tables)


def _out_mlp_kernel(*refs, hidden_tile, final_norm, ctx_tiles, n_branches):
    x_ref = refs[0]
    per_branch = 2 if ctx_tiles else 1
    branch_refs = refs[1:1 + per_branch * n_branches]
    mixd_ref, mod_ref, g2_ref, gf_ref, wout_ref, w1_ref, w2_ref, o_ref = refs[1 + per_branch * n_branches:]
    parts = []
    for j in range(n_branches):
        lat = branch_refs[per_branch * j][0]
        if ctx_tiles:
            lat = jnp.where(pl.program_id(1) < ctx_tiles, branch_refs[per_branch * j + 1][0], lat)
        parts.append(lat)
    parts.append(mixd_ref[0])
    mod = mod_ref[0, 0]
    proj = jnp.zeros(x_ref.shape[1:], F32)
    lo = 0
    for part in parts:
        proj = proj + jnp.dot(part, wout_ref[lo:lo + part.shape[1], :], preferred_element_type=F32)
        lo += part.shape[1]
    x1 = x_ref[0] + mod[2:3] * proj
    hb = (_rms(x1, g2_ref[...]) * (1.0 + mod[4:5]) + mod[3:4]).astype(BF16)
    acc = jnp.zeros_like(x1)
    for j in range(w1_ref.shape[1] // hidden_tile):
        cols = pl.ds(j * hidden_tile, hidden_tile)
        u = jnp.maximum(jnp.dot(hb, w1_ref[:, cols], preferred_element_type=F32), 0.0)
        acc = acc + jnp.dot((u * u).astype(BF16), w2_ref[cols, :], preferred_element_type=F32)
    x2 = x1 + mod[5:6] * acc
    if final_norm:
        x2 = _rms(x2, gf_ref[...])
    o_ref[0] = x2


def out_projection_mlp(x_all, branches, mix_d, mod, g2, g_final, w_out, w_fc1, w_fc2, ctx_tiles, last):
    bsz, n_tok, d = x_all.shape
    tm = TOKEN_TILE
    n_tiles = n_tok // tm
    skip = ctx_tiles if last else 0
    tok = lambda b, i: (b, i + skip, 0)
    full = lambda b, i: (0, 0)
    lat_map = lambda b, i: (b, jnp.maximum(i + skip - ctx_tiles, 0), 0)
    ctx_map = lambda b, i: (b, jnp.minimum(i, ctx_tiles - 1), 0)
    branch_args, branch_specs = [], []
    for lat, ctx_rows in branches:
        branch_args.append(lat)
        branch_specs.append(pl.BlockSpec((1, tm, lat.shape[2]), lat_map))
        if not last:
            branch_args.append(ctx_rows)
            branch_specs.append(pl.BlockSpec((1, tm, ctx_rows.shape[2]), ctx_map))
    kern = functools.partial(_out_mlp_kernel, hidden_tile=1024, final_norm=last,
                             ctx_tiles=0 if last else ctx_tiles, n_branches=len(branches))
    return pl.pallas_call(
        kern,
        grid=(bsz, n_tiles - skip),
        in_specs=[pl.BlockSpec((1, tm, d), tok)] + branch_specs
                 + [pl.BlockSpec((1, tm, mix_d.shape[2]), tok),
                    pl.BlockSpec((1, 1, 6, d), lambda b, i: (b, jnp.where(i + skip < ctx_tiles, 0, 1), 0, 0)),
                    pl.BlockSpec((1, d), full),
                    pl.BlockSpec((1, d), full),
                    pl.BlockSpec(w_out.shape, full, pipeline_mode=pl.Buffered(1)),
                    pl.BlockSpec(w_fc1.shape, full, pipeline_mode=pl.Buffered(1)),
                    pl.BlockSpec(w_fc2.shape, full, pipeline_mode=pl.Buffered(1))],
        out_specs=pl.BlockSpec((1, tm, d), lambda b, i: (b, i, 0)),
        out_shape=jax.ShapeDtypeStruct((bsz, n_tok - skip * tm, d), F32),
        compiler_params=_params("parallel", "parallel"),
        name="out_projection_mlp",
    )(x_all, *branch_args, mix_d, mod, g2, g_final, w_out, w_fc1, w_fc2)


def _key_chunks(n_keys):
    first = min(n_keys, ATTN_Q_TILE)
    chunks = [(0, first)]
    while chunks[-1][0] + chunks[-1][1] < n_keys:
        start = chunks[-1][0] + chunks[-1][1]
        chunks.append((start, min(ATTN_KV_CHUNK, n_keys - start)))
    return chunks


def _online_softmax(q, k_at, v_at, chunks):
    rows = q.shape[0]
    scores = lambda c: _bdot_nt(q, k_at(*c))
    m = jnp.full((rows, 1), NEG_INF, F32)
    acc = jnp.zeros((rows, V_EXT_WIDTH), F32)
    s_next = scores(chunks[0])
    for j, chunk in enumerate(chunks):
        s = s_next
        if j + 1 < len(chunks):
            s_next = scores(chunks[j + 1])
        m_new = jnp.maximum(m, jnp.max(s, axis=-1, keepdims=True))
        p = jnp.exp2(s - m_new)
        acc = acc * jnp.exp2(m - m_new) + jnp.dot(p.astype(BF16), v_at(*chunk), preferred_element_type=F32)
        m = m_new
    return acc


def _attention_kernel(q_ref, k_ref, v_ref, o_ref, *, q_row0):
    n_q, dk = q_ref.shape[2], q_ref.shape[4]
    tq = o_ref.shape[1]
    dv = o_ref.shape[-1] // n_q
    chunks = _key_chunks(k_ref.shape[3])
    rows = pl.ds(pl.multiple_of(q_row0 + pl.program_id(2) * tq, ATTN_Q_TILE), tq)
    if k_ref.shape[2] == 1:
        groups = [(q_ref[0, 0, :, rows, :].reshape(n_q * tq, dk), 0)]
    else:
        groups = [(q_ref[0, 0, g, rows, :], g) for g in range(n_q)]
    outs = []
    for q, g in groups:
        acc = _online_softmax(q, lambda s, n: k_ref[0, 0, g, pl.ds(s, n), :],
                              lambda s, n: v_ref[0, 0, g, pl.ds(s, n), :], chunks)
        o = acc[:, :dv] / acc[:, dv:dv + 1]
        outs.extend(o[i * tq:(i + 1) * tq] for i in range(o.shape[0] // tq))
    o_ref[0] = jnp.concatenate(outs, axis=-1).astype(o_ref.dtype)


def attention(q, k, v, q_row0, n_rows, n_keys, dv, tq):
    bsz, n_pairs, n_q, n_tok, dk = q.shape
    n_kv = k.shape[2]
    n_q_tiles = n_rows // tq
    whole = lambda b, h, i: (b, h, 0, 0, 0)
    return pl.pallas_call(
        functools.partial(_attention_kernel, q_row0=q_row0),
        grid=(bsz, n_pairs, n_q_tiles),
        in_specs=[pl.BlockSpec((1, 1, n_q, n_tok, dk), whole),
                  pl.BlockSpec((1, 1, n_kv, n_keys, dk), lambda b, h, i: (b, h, 0, 0, 0)),
                  pl.BlockSpec((1, 1, n_kv, n_keys, v.shape[-1]), lambda b, h, i: (b, h, 0, 0, 0))],
        out_specs=pl.BlockSpec((1, tq, n_q * dv), lambda b, h, i: (b, i, h)),
        out_shape=jax.ShapeDtypeStruct((bsz, n_q_tiles * tq, n_pairs * n_q * dv), BF16),
        compiler_params=_params("parallel", "parallel", "parallel"),
        name="attention",
    )(q, k, v)


def _na_kernel(q_ref, k_ref, v_ref, bias_ref, o_ref, *, n_ctx, n_row_tiles):
    i = pl.program_id(2)
    n_win = NA_KEY_ROWS * GRID_W
    max_row0 = n_row_tiles * NA_Q_ROWS - NA_KEY_ROWS
    row0 = jnp.clip(i * NA_Q_ROWS - NA_WIN_ROWS // 2, 0, max_row0)
    start = pl.multiple_of(n_ctx + row0 * GRID_W, GRID_W)
    dv = o_ref.shape[-1] // q_ref.shape[2]
    outs = []
    for g in range(q_ref.shape[2]):
        q = q_ref[0, 0, g]
        keys = jnp.concatenate([k_ref[0, 0, g, pl.ds(0, n_ctx), :], k_ref[0, 0, g, pl.ds(start, n_win), :]], axis=0)
        vals = jnp.concatenate([v_ref[0, 0, g, pl.ds(0, n_ctx), :], v_ref[0, 0, g, pl.ds(start, n_win), :]], axis=0)
        s = _bdot_nt(q, keys) + bias_ref[0, g, 0]
        p = jnp.exp2(s - jnp.max(s, axis=-1, keepdims=True)).astype(BF16)
        o = jnp.dot(p, vals, preferred_element_type=F32)
        outs.append(o[:, :dv] / o[:, dv:dv + 1])
    o_ref[0] = jnp.concatenate(outs, axis=-1).astype(o_ref.dtype)


def na_bias_table(rpb, rows, n_ctx):
    n_tiles = rows // NA_Q_ROWS
    n_heads, n_dr, n_dc = rpb.shape
    qc = np.arange(GRID_W)[:, None]
    kc = np.arange(GRID_W)[None, :]
    dc = np.clip(kc - qc + NA_WIN_COLS - 1, 0, n_dc - 1)
    onehot = jnp.asarray(dc.reshape(-1)[None, :] == np.arange(n_dc)[:, None], F32)
    by_col = jnp.dot(rpb.reshape(-1, n_dc).astype(F32), onehot, precision=lax.Precision.HIGHEST)
    pad = NA_KEY_ROWS
    by_col = jnp.pad(by_col.reshape(n_heads, n_dr, GRID_W, GRID_W), ((0, 0), (pad, pad), (0, 0), (0, 0)))
    c_start = np.clip(qc - NA_WIN_COLS // 2, 0, GRID_W - NA_WIN_COLS)
    col_ok = (kc >= c_start) & (kc < c_start + NA_WIN_COLS)
    tables = []
    for tile in (0, 1, n_tiles - 1):
        row0 = int(np.clip(tile * NA_Q_ROWS - NA_WIN_ROWS // 2, 0, rows - NA_KEY_ROWS))
        per_row = []
        for a in range(NA_Q_ROWS):
            qr = tile * NA_Q_ROWS + a
            dr0 = row0 - qr + NA_WIN_ROWS - 1 + pad
            kr = row0 + np.arange(NA_KEY_ROWS)
            r_start = int(np.clip(qr - NA_WIN_ROWS // 2, 0, rows - NA_WIN_ROWS))
            row_ok = (kr >= r_start) & (kr < r_start + NA_WIN_ROWS)
            valid = row_ok[None, :, None] & col_ok[:, None, :]
            sl = jnp.swapaxes(by_col[:, dr0:dr0 + NA_KEY_ROWS], 1, 2)
            per_row.append(jnp.where(valid[None], sl * LOG2E, NEG_INF))
        bias = jnp.stack(per_row, axis=1)
        tables.append(bias.reshape(n_heads, NA_Q_ROWS * GRID_W, NA_KEY_ROWS * GRID_W))
    table = jnp.stack(tables, axis=1)
    return jnp.pad(table, ((0, 0), (0, 0), (0, 0), (n_ctx, 0)))


def neighbourhood_attention(q, k, v, bias, n_ctx):
    bsz, n_pairs, n_q, n_tok, d = q.shape
    tq = ATTN_Q_TILE
    n_tiles = (n_tok - n_ctx) // tq
    ctx_tiles = n_ctx // tq
    kern = functools.partial(_na_kernel, n_ctx=n_ctx, n_row_tiles=n_tiles)

    def bias_map(b, h, i):
        return (h, 0, jnp.where(i == 0, 0, jnp.where(i == n_tiles - 1, 2, 1)), 0, 0)

    whole = lambda b, h, i: (b, h, 0, 0, 0)
    return pl.pallas_call(
        kern,
        grid=(bsz, n_pairs, n_tiles),
        in_specs=[pl.BlockSpec((1, 1, n_q, tq, d), lambda b, h, i: (b, h, 0, i + ctx_tiles, 0)),
                  pl.BlockSpec((1, 1, n_q, n_tok, d), whole),
                  pl.BlockSpec((1, 1, n_q, n_tok, v.shape[-1]), whole),
                  pl.BlockSpec((1, n_q, 1) + bias.shape[3:], bias_map)],
        out_specs=pl.BlockSpec((1, tq, n_q * d), lambda b, h, i: (b, i, h)),
        out_shape=jax.ShapeDtypeStruct((bsz, n_tok - n_ctx, n_pairs * n_q * d), BF16),
        compiler_params=_params("parallel", "parallel", "parallel"),
        name="neighbourhood_attention",
    )(q, k, v, bias)


def _softplus(z):
    return jnp.maximum(z, 0.0) + jnp.log(1.0 + jnp.exp(-jnp.abs(z)))


def _sigmoid(z):
    return 1.0 / (1.0 + jnp.exp(-z))


def _rwkv_prep_kernel(u_ref, halo_ref, taps_ref, w0_ref, w2_ref, a0_ref, a2_ref, kk_ref, ka_ref, rk_ref,
                      g2_ref, *out_refs):
    dir_refs = (out_refs[0:4], out_refs[4:8])
    v_out, gt_refs, bonus_ref, gate_ref = out_refs[8], out_refs[9:11], out_refs[11], out_refs[12]
    tm = u_ref.shape[1]
    c = RWKV_CHUNK
    n_chunks = tm // c
    gw = GROUP_WIDTH
    u = u_ref[0]
    halo = halo_ref[0, 0]
    tok = lax.broadcasted_iota(jnp.int32, u.shape, 0)
    prev = jnp.where(tok == 0, halo[0:1], pltpu.roll(u, 1, axis=0))
    nxt = jnp.where(tok == tm - 1, halo[1:2], pltpu.roll(u, tm - 1, axis=0))
    taps = taps_ref[...]
    s = prev * taps[0:1] + u * taps[1:2] + nxt * taps[2:3]

    r, k, v = s[:, 0:gw], s[:, gw:2 * gw], s[:, 2 * gw:3 * gw]
    lo = 3 * gw
    w_low = jnp.tanh(s[:, lo:lo + 2 * RWKV_DECAY_RANK])
    lo += 2 * RWKV_DECAY_RANK
    a_low = s[:, lo:lo + 2 * RWKV_ICLR_RANK]
    lo += 2 * RWKV_ICLR_RANK
    g_low = _sigmoid(s[:, lo:lo + RWKV_GATE_RANK])
    w_log = -_softplus(-(w0_ref[...] + _bdot(w_low, w2_ref[...]))) - 0.5
    log_decay = -jnp.exp(w_log)
    iclr = _sigmoid(a0_ref[...] + _bdot(a_low, a2_ref[...]))
    gate_ref[0] = _bdot(g_low, g2_ref[...])

    row = lax.broadcasted_iota(jnp.int32, (tm, tm), 0)
    col = lax.broadcasted_iota(jnp.int32, (tm, tm), 1)
    shift = c.bit_length() - 1
    same = (row >> shift) == (col >> shift)
    head_shift = HEAD_DIM.bit_length() - 1
    ones_blk = ((row >> head_shift) == (col >> head_shift)).astype(F32)

    v_out[0] = v.astype(v_out.dtype)
    bonus = jnp.zeros_like(r)
    for d in range(2):
        sl = slice(d * gw, (d + 1) * gw)
        lw = log_decay[:, sl]
        kk = k * kk_ref[:, sl]
        kk = kk * lax.rsqrt(jnp.maximum(_select_cols(kk * kk, ones_blk), 1e-24))
        k_d = k * (1.0 + (iclr[:, sl] - 1.0) * ka_ref[:, sl])
        b_vec = kk * iclr[:, sl]
        tri = (same & ((row >= col) if d == 0 else (row <= col))).astype(F32)
        cum = _select_rows(tri, lw)
        ends = [cum[i * c + (c - 1 if d == 0 else 0)][None] for i in range(n_chunks)]
        e_inv = jnp.exp(-cum)
        streams = (r * jnp.exp(cum), -kk * jnp.exp(cum - lw), b_vec * e_inv, k_d * e_inv)
        for ref, val in zip(dir_refs[d], streams):
            ref[0] = val.astype(ref.dtype)
        gt_refs[d][0, 0] = jnp.exp(jnp.concatenate(ends, axis=0))
        bonus = bonus + r * k_d * rk_ref[:, sl]
    bonus_ref[0] = _select_cols(bonus, ones_blk) * v


def rwkv_prepare(rw, halo, taps, w0, w2, a0, a2, k_k, k_a, r_k, g2):
    bsz, n_tok, width = rw.shape
    tm = TOKEN_TILE
    n_tiles = n_tok // tm
    n_chunks = tm // RWKV_CHUNK
    full = lambda b, i: (0, 0)
    consts = (taps, w0, w2, a0, a2, k_k, k_a, r_k, g2)
    assert tm == GROUP_WIDTH
    tokens_spec = pl.BlockSpec((1, tm, GROUP_WIDTH), lambda b, i: (b, i, 0))
    stream = jax.ShapeDtypeStruct((bsz, n_tok, GROUP_WIDTH), BF16)
    stream_spec = tokens_spec
    decay = jax.ShapeDtypeStruct((bsz, n_tiles, n_chunks, GROUP_WIDTH), F32)
    decay_spec = pl.BlockSpec((1, 1, n_chunks, GROUP_WIDTH), lambda b, i: (b, i, 0, 0))
    tokens = jax.ShapeDtypeStruct((bsz, n_tok, GROUP_WIDTH), F32)
    return pl.pallas_call(
        _rwkv_prep_kernel,
        grid=(bsz, n_tiles),
        in_specs=[pl.BlockSpec((1, tm, width), lambda b, i: (b, i, 0)),
                  pl.BlockSpec((1, 1, 2, width), lambda b, i: (b, i, 0, 0))]
                 + [pl.BlockSpec(t.shape, full) for t in consts],
        out_specs=[stream_spec] * 9 + [decay_spec] * 2 + [tokens_spec] * 2,
        out_shape=[stream] * 9 + [decay] * 2 + [tokens] * 2,
        compiler_params=_params("parallel", "parallel"),
        name="rwkv_prepare",
    )(rw, halo, *consts)


def _bmm(a, b):
    return lax.dot_general(a.astype(BF16), b.astype(BF16), (((2,), (1,)), ((0,), (0,))),
                           preferred_element_type=F32)


def _bmm_nt(a, b):
    return lax.dot_general(a.astype(BF16), b.astype(BF16), (((2,), (2,)), ((0,), (0,))),
                           preferred_element_type=F32)


def _bmm_tn(a, b):
    return lax.dot_general(a.astype(BF16), b.astype(BF16), (((1,), (1,)), ((0,), (0,))),
                           preferred_element_type=F32)


def _rwkv_chunk_kernel(rt_ref, at_ref, bt_ref, kt_ref, v_ref, gt_ref, *rest, reverse):
    y_ref, s_ref = rest[-2:]
    other_ref, bonus_ref, gate_ref, lw_ref, lb_ref = rest[:5] if len(rest) > 2 else (None,) * 5
    n_rows, tb, width = v_ref.shape
    n = width // 2
    c = RWKV_CHUNK
    n_chunks = tb // c
    g = n_rows * n_chunks

    @pl.when(pl.program_id(2) == 0)
    def _():
        s_ref[...] = jnp.zeros_like(s_ref)

    load = lambda ref: ref[...].reshape(g, c, width)
    rt, at, bt, kt, v = (load(ref) for ref in (rt_ref, at_ref, bt_ref, kt_ref, v_ref))
    left = lax.broadcasted_iota(jnp.int32, (1, 1, width), 2) < n
    cat = lambda a, b: jnp.concatenate([a, b], axis=1)

    def bd(y):
        y = y.astype(BF16)
        zero = jnp.zeros_like(y)
        return cat(jnp.where(left, y, zero), jnp.where(left, zero, y))

    row = lax.broadcasted_iota(jnp.int32, (c, 2 * c), 0)
    col = lax.broadcasted_iota(jnp.int32, (c, 2 * c), 1) & (c - 1)
    if reverse:
        row, col = col, row
    strict = row > col
    incl = row >= col
    ar = cat(at, rt)
    ab_rb = _bmm_nt(ar, bd(bt))
    ak_rk = _bmm_nt(ar, bd(kt))
    a_ab = jnp.where(strict, ab_rb[:, :c], 0.0)
    a_rb = jnp.where(incl, ab_rb[:, c:], 0.0)
    a_ak = jnp.where(strict, ak_rk[:, :c], 0.0)
    a_rk = jnp.where(incl, ak_rk[:, c:], 0.0)
    v_bd = bd(v)
    av = _bmm(a_ak, v_bd)
    n1 = jnp.where((row >> 3) == (col >> 3), a_ab, 0.0)
    n2 = _bmm(n1, bd(n1))
    n2_bd = bd(n2)
    n4 = _bmm(n2, n2_bd)
    p = (row == col).astype(F32) + n1 + n2 + _bmm(n1, n2_bd)
    inv = p + _bmm(p, bd(n4))
    for level in range(3, c.bit_length() - 1):
        rb, cb = row >> level, col >> level
        off = jnp.where((rb == cb + 1) & ((rb & 1) == 1), a_ab, 0.0)
        inv = inv + _bmm(_bmm(inv, bd(off)), bd(inv))
    w = _bmm(inv, bd(at))
    u0 = _bmm(inv, bd(av))
    q_eff = rt.astype(F32) + _bmm(a_rb, bd(w))
    y0 = _bmm(jnp.concatenate([a_rb, a_rk], axis=2), cat(bd(u0), v_bd))

    per_chunk = lambda t: t.reshape((n_rows, n_chunks) + t.shape[1:])
    qw = per_chunk(cat(q_eff, w))
    u0, y0, v = per_chunk(u0), per_chunk(y0), per_chunk(v)
    g_tot = gt_ref[:, 0][:, :, None, :]
    bk_end = per_chunk(cat(bt, kt)) * g_tot
    state = s_ref[...]
    for i in (range(n_chunks - 1, -1, -1) if reverse else range(n_chunks)):
        tokens = slice(i * c, (i + 1) * c)
        yu = _bmm_nt(qw[:, i], bd(state))
        y = yu[:, :c] + y0[:, i]
        if other_ref is not None:
            y = _rwkv_output(y + other_ref[:, tokens, :], bonus_ref[:, tokens, :], gate_ref[:, tokens, :],
                             lw_ref[...], lb_ref[...])
        y_ref[:, tokens, :] = y.astype(y_ref.dtype)
        full = _bmm_tn(cat(yu[:, c:] + u0[:, i], v[:, i]), bk_end[:, i])
        state = g_tot[:, i] * state + jnp.where(left, full[:, :n], full[:, n:])
    s_ref[...] = state


def _rwkv_output(y, bonus, gate, lnx_w, lnx_b):
    shape = y.shape
    y = y.reshape(-1, shape[-1])
    row = lax.broadcasted_iota(jnp.int32, (shape[-1], shape[-1]), 0)
    col = lax.broadcasted_iota(jnp.int32, (shape[-1], shape[-1]), 1)
    shift = HEAD_DIM.bit_length() - 1
    head_sum = ((row >> shift) == (col >> shift)).astype(F32)
    mu = _select_cols(y, head_sum) * (1.0 / HEAD_DIM)
    var = _select_cols(jnp.square(y - mu), head_sum) * (1.0 / HEAD_DIM)
    yn = (y - mu) * lax.rsqrt(var + GN_EPS) * lnx_w + lnx_b
    return (yn.reshape(shape) + bonus) * gate


def rwkv_chunk_scan(streams, v, g_tot, ctx_blocks, reverse, finish=None):
    bsz, n_tok, width = v.shape
    tb = TOKEN_TILE
    n_blocks = n_tok // tb
    pair = 2 * HEAD_DIM

    def block(i):
        if not reverse:
            return i
        return jnp.where(i < ctx_blocks, ctx_blocks - 1 - i, n_blocks - 1 + ctx_blocks - i)

    rows = math.gcd(RWKV_ROWS_PER_STEP, bsz)
    spec = pl.BlockSpec((rows, tb, pair), lambda b, p, i: (b, block(i), p))
    in_specs = [spec] * 5 + [pl.BlockSpec((rows, 1, g_tot.shape[2], pair), lambda b, p, i: (b, block(i), 0, p))]
    args = (*streams, v, g_tot)
    if finish is not None:
        vec = pl.BlockSpec((1, pair), lambda b, p, i: (0, p))
        in_specs += [spec, spec, spec, vec, vec]
        args += tuple(finish)
    return pl.pallas_call(
        functools.partial(_rwkv_chunk_kernel, reverse=reverse),
        grid=(bsz // rows, width // pair, n_blocks),
        in_specs=in_specs,
        out_specs=spec,
        out_shape=jax.ShapeDtypeStruct((bsz, n_tok, width), F32 if finish is None else BF16),
        scratch_shapes=[pltpu.VMEM((rows, HEAD_DIM, pair), F32)],
        compiler_params=_params("parallel", "parallel", "arbitrary"),
        name="rwkv_chunk_scan_bwd" if reverse else "rwkv_chunk_scan_fwd",
    )(*args)


def _rope_tables(n_ctx, n_lat, rot_dim, n_rep):
    t = jnp.arange(n_lat, dtype=jnp.int32)
    rows = (t // GRID_W).astype(F32)
    cols = (t % GRID_W).astype(F32)
    per_axis = rot_dim // 2
    inv_freq = ROPE_THETA ** (-jnp.arange(0, per_axis, 2, dtype=F32) / per_axis)
    ang = jnp.concatenate([rows[:, None] * inv_freq, cols[:, None] * inv_freq], axis=-1)
    ang = jnp.concatenate([jnp.zeros((n_ctx, ang.shape[1]), F32), ang], axis=0)
    sign = jnp.tile(jnp.array([-1.0, 1.0], F32), rot_dim // 2)
    cos = jnp.repeat(jnp.cos(ang), 2, axis=-1)
    sin = jnp.repeat(jnp.sin(ang), 2, axis=-1) * sign
    return jnp.tile(cos, (1, n_rep)), jnp.tile(sin, (1, n_rep))


def _pairs(t):
    return t.reshape((t.shape[0], t.shape[1] // 2, 2) + t.shape[2:])


def _block_diag(m0, m1):
    z01 = jnp.zeros((m0.shape[0], m1.shape[1]), m0.dtype)
    z10 = jnp.zeros((m1.shape[0], m0.shape[1]), m0.dtype)
    return jnp.concatenate([jnp.concatenate([m0, z01], axis=1), jnp.concatenate([z10, m1], axis=1)], axis=0)


def _shift_halo(edge, ctx_tiles):
    zero = jnp.zeros_like(edge[:, :1, 0])
    prev = jnp.concatenate([zero, edge[:, :-1, 1]], axis=1)
    nxt = jnp.concatenate([edge[:, 1:, 0], zero], axis=1)
    tile = jnp.arange(prev.shape[1])[None, :, None]
    prev = jnp.where(tile == ctx_tiles, 0.0, prev)
    nxt = jnp.where(tile == ctx_tiles - 1, 0.0, nxt)
    return jnp.stack([prev, nxt], axis=2)


def _rwkv_mixer(rw, rw_edge, ctx_tiles, shift_taps, w0, w2, a0, a2, k_k, k_a, r_k, g2, lnx_w, lnx_b):
    both = lambda t: t.reshape(1, -1)
    outs = rwkv_prepare(rw, _shift_halo(rw_edge, ctx_tiles), shift_taps, both(w0),
                        _block_diag(w2[0], w2[1]).astype(BF16), both(a0),
                        _block_diag(a2[0], a2[1]).astype(BF16), both(k_k), both(k_a), both(r_k),
                        g2.astype(BF16))
    v, bonus, gate = outs[8], outs[11], outs[12]
    y_f = rwkv_chunk_scan(outs[0:4], v, outs[9], ctx_tiles, False)
    return rwkv_chunk_scan(outs[4:8], v, outs[10], ctx_tiles, True,
                           finish=(y_f, bonus, gate, lnx_w[None], lnx_b[None]))


def _dense_branch(q, k, v, n_ctx, want_ctx):
    n_tok = q.shape[3]
    o_lat = attention(q, k, v, n_ctx, n_tok - n_ctx, n_tok, HEAD_DIM, ATTN_LAT_Q_TILE)
    o_ctx = attention(q, k, v, 0, n_ctx, n_ctx, HEAD_DIM, ATTN_Q_TILE) if want_ctx else None
    return o_lat, o_ctx


def kernel(x, c, ctx, c_ctx, w_mod, b_mod, norm1_g, norm2_g, w_in, rwkv_shift, na_rpb, gqa_q_norm, gqa_k_norm, mla_q_norm, mla_kv_norm, mla_w_uq, mla_w_ukv, rwkv_w0, rwkv_w2, rwkv_a0, rwkv_a2, rwkv_k_k, rwkv_k_a, rwkv_r_k, rwkv_g2, rwkv_lnx_w, rwkv_lnx_b, w_out, w_fc1, w_fc2, final_norm_g):
    bsz, n_lat, d_model = x.shape
    n_ctx = ctx.shape[1]
    depth = w_mod.shape[0]
    rows = n_lat // GRID_W
    ctx_tiles = n_ctx // TOKEN_TILE
    assert n_ctx % TOKEN_TILE == 0 and n_lat % TOKEN_TILE == 0 and rows % NA_Q_ROWS == 0

    tables = (_rope_tables(n_ctx, n_lat, HEAD_DIM, N_HEADS) + _rope_tables(n_ctx, n_lat, MLA_ROPE_DIM, N_HEADS))

    cond = jnp.concatenate([jax.nn.silu(c), jax.nn.silu(c_ctx)[None]], axis=0)
    n_cond = -(-cond.shape[0] // 8) * 8
    cond = jnp.pad(cond, ((0, n_cond - cond.shape[0]), (0, 0)))
    mod_all = modulation(cond, w_mod, b_mod[:, None, :])

    x_all = jnp.concatenate([ctx, x], axis=1)
    bounds = np.cumsum([0, NA_COLS, GQA_COLS, MLA_COLS, RWKV_COLS])
    mla_pad = -MLA_COLS % 128

    for layer in range(depth):
        want_ctx = layer < depth - 1
        mod = mod_all[layer].reshape(n_cond, 6, d_model)
        mod = jnp.stack([jnp.broadcast_to(mod[bsz], (bsz, 6, d_model)), mod[:bsz]], axis=1)

        w_na, w_gqa, w_mla, w_rw = [w_in[layer][:, bounds[i]:bounds[i + 1]].astype(BF16) for i in range(4)]
        w_uq = mla_w_uq[layer].reshape(MLA_Q_RANK, N_HEADS, MLA_QK_DIM)
        w_uq = jnp.concatenate([w_uq[:, :, :HEAD_DIM].reshape(MLA_Q_RANK, -1),
                                w_uq[:, :, HEAD_DIM:].reshape(MLA_Q_RANK, -1)], axis=1)
        w_ukv = mla_w_ukv[layer].reshape(MLA_KV_RANK, N_HEADS, 2 * HEAD_DIM)
        w_ukv = jnp.concatenate([w_ukv[:, :, :HEAD_DIM].reshape(MLA_KV_RANK, -1),
                                 w_ukv[:, :, HEAD_DIM:].reshape(MLA_KV_RANK, -1)], axis=1)
        weights = (w_na, w_gqa, jnp.pad(w_mla, ((0, 0), (0, mla_pad))), w_rw,
                   jnp.tile(gqa_q_norm[layer], N_HEADS)[None], jnp.tile(gqa_k_norm[layer], GQA_KV_HEADS)[None],
                   mla_q_norm[layer][None], mla_kv_norm[layer][None], w_uq.astype(BF16), w_ukv.astype(BF16))
        (a_q, a_k, a_v, b_q, b_k, b_v, c_q, c_k, c_v, rw, rw_edge) = in_projection(
            x_all, mod, norm1_g[layer][None], weights, tables, ctx_tiles)

        a_q, a_k, a_v = _pairs(a_q), _pairs(a_k), _pairs(a_v)
        bias = na_bias_table(na_rpb[layer], rows, n_ctx)
        out_a = neighbourhood_attention(a_q, a_k, a_v, bias.reshape((N_HEADS // 2, 2) + bias.shape[1:]), n_ctx)
        out_ac = attention(a_q, a_k, a_v, 0, n_ctx, n_ctx, HEAD_DIM, ATTN_Q_TILE) if want_ctx else None
        mix_b = _dense_branch(_pairs(b_q), b_k[:, :, None], b_v[:, :, None], n_ctx, want_ctx)
        mix_c = _dense_branch(_pairs(c_q), _pairs(c_k), _pairs(c_v), n_ctx, want_ctx)
        mix_d = _rwkv_mixer(rw, rw_edge, ctx_tiles, rwkv_shift[layer], rwkv_w0[layer], rwkv_w2[layer], rwkv_a0[layer],
                            rwkv_a2[layer], rwkv_k_k[layer], rwkv_k_a[layer], rwkv_r_k[layer],
                            rwkv_g2[layer], rwkv_lnx_w[layer], rwkv_lnx_b[layer])

        x_all = out_projection_mlp(x_all, [(out_a, out_ac), mix_b, mix_c], mix_d, mod, norm2_g[layer][None],
                                   final_norm_g[None], w_out[layer].astype(BF16), w_fc1[layer].astype(BF16),
                                   w_fc2[layer].astype(BF16), ctx_tiles, not want_ctx)
    return x_all
```

```python
import functools
import math

import numpy as np
import jax
import jax.numpy as jnp
from jax import lax
from jax.experimental import pallas as pl
from jax.experimental.pallas import tpu as pltpu

F32 = jnp.float32
BF16 = jnp.bfloat16

GRID_W = 64
HEAD_DIM = 64
N_HEADS = 4
GROUP_WIDTH = N_HEADS * HEAD_DIM
GQA_KV_HEADS = 2
NA_WIN_ROWS = 8
NA_WIN_COLS = 16
MLA_Q_RANK = 256
MLA_KV_RANK = 128
MLA_ROPE_DIM = 32
MLA_QK_DIM = HEAD_DIM + MLA_ROPE_DIM
RWKV_DECAY_RANK = 64
RWKV_ICLR_RANK = 64
RWKV_GATE_RANK = 160
NA_COLS = 3 * GROUP_WIDTH
GQA_COLS = (N_HEADS + 2 * GQA_KV_HEADS) * HEAD_DIM
MLA_COLS = MLA_Q_RANK + MLA_KV_RANK + MLA_ROPE_DIM
RWKV_COLS = 3 * GROUP_WIDTH + 2 * RWKV_DECAY_RANK + 2 * RWKV_ICLR_RANK + RWKV_GATE_RANK
ROPE_THETA = 10000.0
RMS_EPS = 1e-6
GN_EPS = 64e-5
NEG_INF = -1e30

TOKEN_TILE = 256
ATTN_Q_TILE = 256
ATTN_LAT_Q_TILE = 1024
ATTN_KV_CHUNK = 1024
V_EXT_WIDTH = 128
LOG2E = float(np.log2(np.e))
NA_Q_ROWS = ATTN_Q_TILE // GRID_W
NA_KEY_ROWS = NA_Q_ROWS + NA_WIN_ROWS - 1
RWKV_CHUNK = 64
RWKV_ROWS_PER_STEP = 8
VMEM_LIMIT = 56 * 1024 * 1024


def _params(*sem):
    return pltpu.CompilerParams(dimension_semantics=sem, vmem_limit_bytes=VMEM_LIMIT)


def _bdot(a, b):
    return jnp.dot(a.astype(BF16), b.astype(BF16), preferred_element_type=F32)


def _bdot_nt(a, b):
    return lax.dot_general(a.astype(BF16), b.astype(BF16), (((1,), (1,)), ((), ())),
                           preferred_element_type=F32)


def _mod_kernel(a_ref, w_ref, b_ref, o_ref):
    o_ref[0] = _bdot(a_ref[...], w_ref[0]) + b_ref[0]


def modulation(cond, w_mod, b_mod, tn=1536):
    depth, d, n = w_mod.shape
    r = cond.shape[0]
    return pl.pallas_call(
        _mod_kernel,
        grid=(depth, n // tn),
        in_specs=[pl.BlockSpec((r, d), lambda l, j: (0, 0)),
                  pl.BlockSpec((1, d, tn), lambda l, j: (l, 0, j)),
                  pl.BlockSpec((1, 1, tn), lambda l, j: (l, 0, j))],
        out_specs=pl.BlockSpec((1, r, tn), lambda l, j: (l, 0, j)),
        out_shape=jax.ShapeDtypeStruct((depth, r, n), F32),
        compiler_params=_params("parallel", "parallel"),
        name="modulation",
    )(cond, w_mod, b_mod)


def _rms(x, g):
    return x * lax.rsqrt(jnp.mean(x * x, axis=-1, keepdims=True) + RMS_EPS) * g


def _split3(x):
    hi = x.astype(BF16)
    rest = x - hi.astype(F32)
    mid = rest.astype(BF16)
    return hi, mid, (rest - mid.astype(F32)).astype(BF16)


def _select_rows(sel, x):
    sel = sel.astype(BF16)
    hi, mid, lo = _split3(x)
    dot = lambda part: jnp.dot(sel, part, preferred_element_type=F32)
    return dot(lo) + dot(mid) + dot(hi)


def _select_cols(x, sel):
    sel = sel.astype(BF16)
    hi, mid, lo = _split3(x)
    dot = lambda part: jnp.dot(part, sel, preferred_element_type=F32)
    return dot(lo) + dot(mid) + dot(hi)


def _sum_cols(x, sel):
    sel = sel.astype(BF16)
    hi = x.astype(BF16)
    mid = (x - hi.astype(F32)).astype(BF16)
    return jnp.dot(mid, sel, preferred_element_type=F32) + jnp.dot(hi, sel, preferred_element_type=F32)


def _pair_swap(x):
    axis = x.ndim - 1
    lane = lax.broadcasted_iota(jnp.int32, x.shape, axis)
    return jnp.where((lane & 1) == 0, pltpu.roll(x, x.shape[axis] - 1, axis=axis), pltpu.roll(x, 1, axis=axis))


def _rotate(x, cos, sin):
    return x * cos + _pair_swap(x) * sin


def _in_proj_kernel(x_ref, mod_ref, g_ref, wna_ref, wgqa_ref, wmla_ref, wrw_ref, gq_ref, gk_ref,
                    mq_ref, mkv_ref, wuq_ref, wukv_ref, cosh_ref, sinh_ref, cosm_ref, sinm_ref,
                    naq_ref, nak_ref, nav_ref, bq_ref, bk_ref, bv_ref, cq_ref, ck_ref, cv_ref, rw_ref, edge_ref):
    hd, gw = HEAD_DIM, GROUP_WIDTH
    mod = mod_ref[0, 0]
    hb = (_rms(x_ref[0], g_ref[...]) * (1.0 + mod[1:2]) + mod[0:1]).astype(BF16)
    tm = hb.shape[0]
    ones_col = (lax.broadcasted_iota(jnp.int32, (tm, V_EXT_WIDTH - hd), 1) == 0).astype(F32)
    qk_scale = LOG2E * hd ** -0.5

    def put_heads(ref, val, n_heads, width):
        for h in range(n_heads):
            ref[0, h] = val[:, h * width:(h + 1) * width].astype(ref.dtype)

    def put_values(ref, val, n_heads):
        for h in range(n_heads):
            ref[0, h] = jnp.concatenate([val[:, h * hd:(h + 1) * hd], ones_col], axis=-1).astype(ref.dtype)

    na = jnp.dot(hb, wna_ref[...], preferred_element_type=F32)
    put_heads(naq_ref, na[:, :gw] * qk_scale, N_HEADS, hd)
    put_heads(nak_ref, na[:, gw:2 * gw], N_HEADS, hd)
    put_values(nav_ref, na[:, 2 * gw:], N_HEADS)

    gqa = jnp.dot(hb, wgqa_ref[...], preferred_element_type=F32)
    row = lax.broadcasted_iota(jnp.int32, (gw, gw), 0)
    col = lax.broadcasted_iota(jnp.int32, (gw, gw), 1)
    shift = hd.bit_length() - 1
    head_sum = ((row >> shift) == (col >> shift)).astype(F32)
    cos_h, sin_h = cosh_ref[...], sinh_ref[...]
    kw = GQA_KV_HEADS * hd
    q = gqa[:, :gw]
    q = q * lax.rsqrt(_select_cols(q * q, head_sum) * (1.0 / hd) + RMS_EPS) * gq_ref[...]
    put_heads(bq_ref, _rotate(q, cos_h, sin_h) * qk_scale, N_HEADS, hd)
    k = gqa[:, gw:gw + kw]
    k = k * lax.rsqrt(_select_cols(k * k, head_sum[:kw, :kw]) * (1.0 / hd) + RMS_EPS) * gk_ref[...]
    put_heads(bk_ref, _rotate(k, cos_h[:, :kw], sin_h[:, :kw]), GQA_KV_HEADS, hd)
    put_values(bv_ref, gqa[:, gw + kw:], GQA_KV_HEADS)

    mla = jnp.dot(hb, wmla_ref[...], preferred_element_type=F32)
    cos_m, sin_m = cosm_ref[...], sinm_ref[...]
    mla_scale = LOG2E * MLA_QK_DIM ** -0.5
    uq = jnp.dot(_rms(mla[:, :MLA_Q_RANK], mq_ref[...]).astype(BF16), wuq_ref[...], preferred_element_type=F32)
    q_nope = uq[:, :gw] * mla_scale
    q_rope = _rotate(uq[:, gw:], cos_m, sin_m) * mla_scale
    lo = MLA_Q_RANK + MLA_KV_RANK
    ukv = jnp.dot(_rms(mla[:, MLA_Q_RANK:lo], mkv_ref[...]).astype(BF16), wukv_ref[...],
                  preferred_element_type=F32)
    k_rope = _rotate(mla[:, lo:], cos_m, sin_m)[:, :MLA_ROPE_DIM]
    rd = MLA_ROPE_DIM
    for h in range(N_HEADS):
        cq_ref[0, h] = jnp.concatenate([q_nope[:, h * hd:(h + 1) * hd], q_rope[:, h * rd:(h + 1) * rd]],
                                       axis=-1).astype(cq_ref.dtype)
        ck_ref[0, h] = jnp.concatenate([ukv[:, h * hd:(h + 1) * hd], k_rope], axis=-1).astype(ck_ref.dtype)
    put_values(cv_ref, ukv[:, gw:], N_HEADS)

    rw = jnp.dot(hb, wrw_ref[...], preferred_element_type=F32)
    rw_ref[0] = rw
    edge_ref[0, 0] = jnp.concatenate([rw[0:1], rw[tm - 1:tm]], axis=0)


def in_projection(x_all, mod, g, weights, tables, ctx_tiles):
    bsz, n_tok, d = x_all.shape
    tm = TOKEN_TILE
    tok = lambda b, i: (b, i, 0)
    full = lambda b, i: (0, 0)
    heads = lambda n, w: (jax.ShapeDtypeStruct((bsz, n, n_tok, w), BF16),
                          pl.BlockSpec((1, n, tm, w), lambda b, i: (b, 0, i, 0)))
    outs = [heads(N_HEADS, HEAD_DIM), heads(N_HEADS, HEAD_DIM), heads(N_HEADS, V_EXT_WIDTH),
            heads(N_HEADS, HEAD_DIM), heads(GQA_KV_HEADS, HEAD_DIM), heads(GQA_KV_HEADS, V_EXT_WIDTH),
            heads(N_HEADS, MLA_QK_DIM), heads(N_HEADS, MLA_QK_DIM), heads(N_HEADS, V_EXT_WIDTH),
            (jax.ShapeDtypeStruct((bsz, n_tok, RWKV_COLS), F32), pl.BlockSpec((1, tm, RWKV_COLS), tok)),
            (jax.ShapeDtypeStruct((bsz, n_tok // tm, 2, RWKV_COLS), F32),
             pl.BlockSpec((1, 1, 2, RWKV_COLS), lambda b, i: (b, i, 0, 0)))]
    return pl.pallas_call(
        _in_proj_kernel,
        grid=(bsz, n_tok // tm),
        in_specs=[pl.BlockSpec((1, tm, d), tok),
                  pl.BlockSpec((1, 1, 6, d), lambda b, i: (b, jnp.where(i < ctx_tiles, 0, 1), 0, 0)),
                  pl.BlockSpec((1, d), full)]
                 + [pl.BlockSpec(w.shape, full) for w in weights]
                 + [pl.BlockSpec((tm, t.shape[1]), lambda b, i: (i, 0)) for t in tables],
        out_specs=[o[1] for o in outs],
        out_shape=[o[0] for o in outs],
        compiler_params=_params("parallel", "parallel"),
        name="in_projection",
    )(x_all, mod, g, *weights, *tables)


def _out_mlp_kernel(*refs, hidden_tile, final_norm, ctx_tiles, n_branches):
    x_ref = refs[0]
    per_branch = 2 if ctx_tiles else 1
    branch_refs = refs[1:1 + per_branch * n_branches]
    mixd_ref, mod_ref, g2_ref, gf_ref, wout_ref, w1_ref, w2_ref, o_ref = refs[1 + per_branch * n_branches:]
    parts = []
    for j in range(n_branches):
        lat = branch_refs[per_branch * j][0]
        if ctx_tiles:
            lat = jnp.where(pl.program_id(1) < ctx_tiles, branch_refs[per_branch * j + 1][0], lat)
        parts.append(lat)
    parts.append(mixd_ref[0])
    mod = mod_ref[0, 0]
    proj = jnp.zeros(x_ref.shape[1:], F32)
    lo = 0
    for part in parts:
        proj = proj + jnp.dot(part, wout_ref[lo:lo + part.shape[1], :], preferred_element_type=F32)
        lo += part.shape[1]
    x1 = x_ref[0] + mod[2:3] * proj
    hb = (_rms(x1, g2_ref[...]) * (1.0 + mod[4:5]) + mod[3:4]).astype(BF16)
    acc = jnp.zeros_like(x1)
    for j in range(w1_ref.shape[1] // hidden_tile):
        cols = pl.ds(j * hidden_tile, hidden_tile)
        u = jnp.maximum(jnp.dot(hb, w1_ref[:, cols], preferred_element_type=F32), 0.0)
        acc = acc + jnp.dot((u * u).astype(BF16), w2_ref[cols, :], preferred_element_type=F32)
    x2 = x1 + mod[5:6] * acc
    if final_norm:
        x2 = _rms(x2, gf_ref[...])
    o_ref[0] = x2


def out_projection_mlp(x_all, branches, mix_d, mod, g2, g_final, w_out, w_fc1, w_fc2, ctx_tiles, last):
    bsz, n_tok, d = x_all.shape
    tm = TOKEN_TILE
    n_tiles = n_tok // tm
    skip = ctx_tiles if last else 0
    tok = lambda b, i: (b, i + skip, 0)
    full = lambda b, i: (0, 0)
    lat_map = lambda b, i: (b, jnp.maximum(i + skip - ctx_tiles, 0), 0)
    ctx_map = lambda b, i: (b, jnp.minimum(i, ctx_tiles - 1), 0)
    branch_args, branch_specs = [], []
    for lat, ctx_rows in branches:
        branch_args.append(lat)
        branch_specs.append(pl.BlockSpec((1, tm, lat.shape[2]), lat_map))
        if not last:
            branch_args.append(ctx_rows)
            branch_specs.append(pl.BlockSpec((1, tm, ctx_rows.shape[2]), ctx_map))
    kern = functools.partial(_out_mlp_kernel, hidden_tile=1024, final_norm=last,
                             ctx_tiles=0 if last else ctx_tiles, n_branches=len(branches))
    return pl.pallas_call(
        kern,
        grid=(bsz, n_tiles - skip),
        in_specs=[pl.BlockSpec((1, tm, d), tok)] + branch_specs
                 + [pl.BlockSpec((1, tm, mix_d.shape[2]), tok),
                    pl.BlockSpec((1, 1, 6, d), lambda b, i: (b, jnp.where(i + skip < ctx_tiles, 0, 1), 0, 0)),
                    pl.BlockSpec((1, d), full),
                    pl.BlockSpec((1, d), full),
                    pl.BlockSpec(w_out.shape, full, pipeline_mode=pl.Buffered(1)),
                    pl.BlockSpec(w_fc1.shape, full, pipeline_mode=pl.Buffered(1)),
                    pl.BlockSpec(w_fc2.shape, full, pipeline_mode=pl.Buffered(1))],
        out_specs=pl.BlockSpec((1, tm, d), lambda b, i: (b, i, 0)),
        out_shape=jax.ShapeDtypeStruct((bsz, n_tok - skip * tm, d), F32),
        compiler_params=_params("parallel", "parallel"),
        name="out_projection_mlp",
    )(x_all, *branch_args, mix_d, mod, g2, g_final, w_out, w_fc1, w_fc2)


def _key_chunks(n_keys):
    first = min(n_keys, ATTN_Q_TILE)
    chunks = [(0, first)]
    while chunks[-1][0] + chunks[-1][1] < n_keys:
        start = chunks[-1][0] + chunks[-1][1]
        chunks.append((start, min(ATTN_KV_CHUNK, n_keys - start)))
    return chunks


def _online_softmax(q, k_at, v_at, chunks):
    rows = q.shape[0]
    scores = lambda c: _bdot_nt(q, k_at(*c))
    m = jnp.full((rows, 1), NEG_INF, F32)
    acc = jnp.zeros((rows, V_EXT_WIDTH), F32)
    s_next = scores(chunks[0])
    for j, chunk in enumerate(chunks):
        s = s_next
        if j + 1 < len(chunks):
            s_next = scores(chunks[j + 1])
        m_new = jnp.maximum(m, jnp.max(s, axis=-1, keepdims=True))
        p = jnp.exp2(s - m_new)
        acc = acc * jnp.exp2(m - m_new) + jnp.dot(p.astype(BF16), v_at(*chunk), preferred_element_type=F32)
        m = m_new
    return acc


def _attention_kernel(q_ref, k_ref, v_ref, o_ref, *, q_row0):
    n_q, dk = q_ref.shape[2], q_ref.shape[4]
    tq = o_ref.shape[1]
    dv = o_ref.shape[-1] // n_q
    chunks = _key_chunks(k_ref.shape[3])
    rows = pl.ds(pl.multiple_of(q_row0 + pl.program_id(2) * tq, ATTN_Q_TILE), tq)
    if k_ref.shape[2] == 1:
        groups = [(q_ref[0, 0, :, rows, :].reshape(n_q * tq, dk), 0)]
    else:
        groups = [(q_ref[0, 0, g, rows, :], g) for g in range(n_q)]
    outs = []
    for q, g in groups:
        acc = _online_softmax(q, lambda s, n: k_ref[0, 0, g, pl.ds(s, n), :],
                              lambda s, n: v_ref[0, 0, g, pl.ds(s, n), :], chunks)
        o = acc[:, :dv] / acc[:, dv:dv + 1]
        outs.extend(o[i * tq:(i + 1) * tq] for i in range(o.shape[0] // tq))
    o_ref[0] = jnp.concatenate(outs, axis=-1).astype(o_ref.dtype)


def attention(q, k, v, q_row0, n_rows, n_keys, dv, tq):
    bsz, n_pairs, n_q, n_tok, dk = q.shape
    n_kv = k.shape[2]
    n_q_tiles = n_rows // tq
    whole = lambda b, h, i: (b, h, 0, 0, 0)
    return pl.pallas_call(
        functools.partial(_attention_kernel, q_row0=q_row0),
        grid=(bsz, n_pairs, n_q_tiles),
        in_specs=[pl.BlockSpec((1, 1, n_q, n_tok, dk), whole),
                  pl.BlockSpec((1, 1, n_kv, n_keys, dk), lambda b, h, i: (b, h, 0, 0, 0)),
                  pl.BlockSpec((1, 1, n_kv, n_keys, v.shape[-1]), lambda b, h, i: (b, h, 0, 0, 0))],
        out_specs=pl.BlockSpec((1, tq, n_q * dv), lambda b, h, i: (b, i, h)),
        out_shape=jax.ShapeDtypeStruct((bsz, n_q_tiles * tq, n_pairs * n_q * dv), BF16),
        compiler_params=_params("parallel", "parallel", "parallel"),
        name="attention",
    )(q, k, v)


def _na_kernel(q_ref, k_ref, v_ref, bias_ref, o_ref, *, n_ctx, n_row_tiles):
    i = pl.program_id(2)
    n_win = NA_KEY_ROWS * GRID_W
    max_row0 = n_row_tiles * NA_Q_ROWS - NA_KEY_ROWS
    row0 = jnp.clip(i * NA_Q_ROWS - NA_WIN_ROWS // 2, 0, max_row0)
    start = pl.multiple_of(n_ctx + row0 * GRID_W, GRID_W)
    dv = o_ref.shape[-1] // q_ref.shape[2]
    outs = []
    for g in range(q_ref.shape[2]):
        q = q_ref[0, 0, g]
        keys = jnp.concatenate([k_ref[0, 0, g, pl.ds(0, n_ctx), :], k_ref[0, 0, g, pl.ds(start, n_win), :]], axis=0)
        vals = jnp.concatenate([v_ref[0, 0, g, pl.ds(0, n_ctx), :], v_ref[0, 0, g, pl.ds(start, n_win), :]], axis=0)
        s = _bdot_nt(q, keys) + bias_ref[0, g, 0]
        p = jnp.exp2(s - jnp.max(s, axis=-1, keepdims=True)).astype(BF16)
        o = jnp.dot(p, vals, preferred_element_type=F32)
        outs.append(o[:, :dv] / o[:, dv:dv + 1])
    o_ref[0] = jnp.concatenate(outs, axis=-1).astype(o_ref.dtype)


def na_bias_table(rpb, rows, n_ctx):
    n_tiles = rows // NA_Q_ROWS
    n_heads, n_dr, n_dc = rpb.shape
    qc = np.arange(GRID_W)[:, None]
    kc = np.arange(GRID_W)[None, :]
    dc = np.clip(kc - qc + NA_WIN_COLS - 1, 0, n_dc - 1)
    onehot = jnp.asarray(dc.reshape(-1)[None, :] == np.arange(n_dc)[:, None], F32)
    by_col = jnp.dot(rpb.reshape(-1, n_dc).astype(F32), onehot, precision=lax.Precision.HIGHEST)
    pad = NA_KEY_ROWS
    by_col = jnp.pad(by_col.reshape(n_heads, n_dr, GRID_W, GRID_W), ((0, 0), (pad, pad), (0, 0), (0, 0)))
    c_start = np.clip(qc - NA_WIN_COLS // 2, 0, GRID_W - NA_WIN_COLS)
    col_ok = (kc >= c_start) & (kc < c_start + NA_WIN_COLS)
    tables = []
    for tile in (0, 1, n_tiles - 1):
        row0 = int(np.clip(tile * NA_Q_ROWS - NA_WIN_ROWS // 2, 0, rows - NA_KEY_ROWS))
        per_row = []
        for a in range(NA_Q_ROWS):
            qr = tile * NA_Q_ROWS + a
            dr0 = row0 - qr + NA_WIN_ROWS - 1 + pad
            kr = row0 + np.arange(NA_KEY_ROWS)
            r_start = int(np.clip(qr - NA_WIN_ROWS // 2, 0, rows - NA_WIN_ROWS))
            row_ok = (kr >= r_start) & (kr < r_start + NA_WIN_ROWS)
            valid = row_ok[None, :, None] & col_ok[:, None, :]
            sl = jnp.swapaxes(by_col[:, dr0:dr0 + NA_KEY_ROWS], 1, 2)
            per_row.append(jnp.where(valid[None], sl * LOG2E, NEG_INF))
        bias = jnp.stack(per_row, axis=1)
        tables.append(bias.reshape(n_heads, NA_Q_ROWS * GRID_W, NA_KEY_ROWS * GRID_W))
    table = jnp.stack(tables, axis=1)
    return jnp.pad(table, ((0, 0), (0, 0), (0, 0), (n_ctx, 0)))


def neighbourhood_attention(q, k, v, bias, n_ctx):
    bsz, n_pairs, n_q, n_tok, d = q.shape
    tq = ATTN_Q_TILE
    n_tiles = (n_tok - n_ctx) // tq
    ctx_tiles = n_ctx // tq
    kern = functools.partial(_na_kernel, n_ctx=n_ctx, n_row_tiles=n_tiles)

    def bias_map(b, h, i):
        return (h, 0, jnp.where(i == 0, 0, jnp.where(i == n_tiles - 1, 2, 1)), 0, 0)

    whole = lambda b, h, i: (b, h, 0, 0, 0)
    return pl.pallas_call(
        kern,
        grid=(bsz, n_pairs, n_tiles),
        in_specs=[pl.BlockSpec((1, 1, n_q, tq, d), lambda b, h, i: (b, h, 0, i + ctx_tiles, 0)),
                  pl.BlockSpec((1, 1, n_q, n_tok, d), whole),
                  pl.BlockSpec((1, 1, n_q, n_tok, v.shape[-1]), whole),
                  pl.BlockSpec((1, n_q, 1) + bias.shape[3:], bias_map)],
        out_specs=pl.BlockSpec((1, tq, n_q * d), lambda b, h, i: (b, i, h)),
        out_shape=jax.ShapeDtypeStruct((bsz, n_tok - n_ctx, n_pairs * n_q * d), BF16),
        compiler_params=_params("parallel", "parallel", "parallel"),
        name="neighbourhood_attention",
    )(q, k, v, bias)


def _softplus(z):
    return jnp.maximum(z, 0.0) + jnp.log(1.0 + jnp.exp(-jnp.abs(z)))


def _sigmoid(z):
    return 1.0 / (1.0 + jnp.exp(-z))


def _rwkv_prep_kernel(u_ref, halo_ref, taps_ref, w0_ref, w2_ref, a0_ref, a2_ref, kk_ref, ka_ref, rk_ref,
                      g2_ref, *out_refs):
    dir_refs = (out_refs[0:4], out_refs[4:8])
    v_out, gt_refs, bonus_ref, gate_ref = out_refs[8], out_refs[9:11], out_refs[11], out_refs[12]
    tm = u_ref.shape[1]
    c = RWKV_CHUNK
    n_chunks = tm // c
    gw = GROUP_WIDTH
    u = u_ref[0]
    halo = halo_ref[0, 0]
    tok = lax.broadcasted_iota(jnp.int32, u.shape, 0)
    prev = jnp.where(tok == 0, halo[0:1], pltpu.roll(u, 1, axis=0))
    nxt = jnp.where(tok == tm - 1, halo[1:2], pltpu.roll(u, tm - 1, axis=0))
    taps = taps_ref[...]
    s = prev * taps[0:1] + u * taps[1:2] + nxt * taps[2:3]

    r, k, v = s[:, 0:gw], s[:, gw:2 * gw], s[:, 2 * gw:3 * gw]
    lo = 3 * gw
    w_low = jnp.tanh(s[:, lo:lo + 2 * RWKV_DECAY_RANK])
    lo += 2 * RWKV_DECAY_RANK
    a_low = s[:, lo:lo + 2 * RWKV_ICLR_RANK]
    lo += 2 * RWKV_ICLR_RANK
    g_low = _sigmoid(s[:, lo:lo + RWKV_GATE_RANK])
    w_log = -_softplus(-(w0_ref[...] + _bdot(w_low, w2_ref[...]))) - 0.5
    log_decay = -jnp.exp(w_log)
    iclr = _sigmoid(a0_ref[...] + _bdot(a_low, a2_ref[...]))
    gate_ref[0] = _bdot(g_low, g2_ref[...])

    row = lax.broadcasted_iota(jnp.int32, (tm, tm), 0)
    col = lax.broadcasted_iota(jnp.int32, (tm, tm), 1)
    shift = c.bit_length() - 1
    same = (row >> shift) == (col >> shift)
    head_shift = HEAD_DIM.bit_length() - 1
    ones_blk = ((row >> head_shift) == (col >> head_shift)).astype(F32)

    v_out[0] = v.astype(v_out.dtype)
    bonus = jnp.zeros_like(r)
    for d in range(2):
        sl = slice(d * gw, (d + 1) * gw)
        lw = log_decay[:, sl]
        kk = k * kk_ref[:, sl]
        kk = kk * lax.rsqrt(jnp.maximum(_sum_cols(kk * kk, ones_blk), 1e-24))
        k_d = k * (1.0 + (iclr[:, sl] - 1.0) * ka_ref[:, sl])
        b_vec = kk * iclr[:, sl]
        tri = (same & ((row >= col) if d == 0 else (row <= col))).astype(F32)
        cum = _select_rows(tri, lw)
        ends = [cum[i * c + (c - 1 if d == 0 else 0)][None] for i in range(n_chunks)]
        e_inv = jnp.exp(-cum)
        streams = (r * jnp.exp(cum), -kk * jnp.exp(cum - lw), b_vec * e_inv, k_d * e_inv)
        for ref, val in zip(dir_refs[d], streams):
            ref[0] = val.astype(ref.dtype)
        gt_refs[d][0, 0] = jnp.exp(jnp.concatenate(ends, axis=0))
        bonus = bonus + r * k_d * rk_ref[:, sl]
    bonus_ref[0] = _sum_cols(bonus, ones_blk) * v


def rwkv_prepare(rw, halo, taps, w0, w2, a0, a2, k_k, k_a, r_k, g2):
    bsz, n_tok, width = rw.shape
    tm = TOKEN_TILE
    n_tiles = n_tok // tm
    n_chunks = tm // RWKV_CHUNK
    full = lambda b, i: (0, 0)
    consts = (taps, w0, w2, a0, a2, k_k, k_a, r_k, g2)
    assert tm == GROUP_WIDTH
    tokens_spec = pl.BlockSpec((1, tm, GROUP_WIDTH), lambda b, i: (b, i, 0))
    stream = jax.ShapeDtypeStruct((bsz, n_tok, GROUP_WIDTH), BF16)
    stream_spec = tokens_spec
    decay = jax.ShapeDtypeStruct((bsz, n_tiles, n_chunks, GROUP_WIDTH), F32)
    decay_spec = pl.BlockSpec((1, 1, n_chunks, GROUP_WIDTH), lambda b, i: (b, i, 0, 0))
    tokens = jax.ShapeDtypeStruct((bsz, n_tok, GROUP_WIDTH), F32)
    return pl.pallas_call(
        _rwkv_prep_kernel,
        grid=(bsz, n_tiles),
        in_specs=[pl.BlockSpec((1, tm, width), lambda b, i: (b, i, 0)),
                  pl.BlockSpec((1, 1, 2, width), lambda b, i: (b, i, 0, 0))]
                 + [pl.BlockSpec(t.shape, full) for t in consts],
        out_specs=[stream_spec] * 9 + [decay_spec] * 2 + [tokens_spec] * 2,
        out_shape=[stream] * 9 + [decay] * 2 + [tokens] * 2,
        compiler_params=_params("parallel", "parallel"),
        name="rwkv_prepare",
    )(rw, halo, *consts)


def _bmm(a, b):
    return lax.dot_general(a.astype(BF16), b.astype(BF16), (((2,), (1,)), ((0,), (0,))),
                           preferred_element_type=F32)


def _bmm_nt(a, b):
    return lax.dot_general(a.astype(BF16), b.astype(BF16), (((2,), (2,)), ((0,), (0,))),
                           preferred_element_type=F32)


def _bmm_tn(a, b):
    return lax.dot_general(a.astype(BF16), b.astype(BF16), (((1,), (1,)), ((0,), (0,))),
                           preferred_element_type=F32)


def _rwkv_chunk_kernel(rt_ref, at_ref, bt_ref, kt_ref, v_ref, gt_ref, *rest, reverse):
    y_ref, s_ref = rest[-2:]
    other_ref, bonus_ref, gate_ref, lw_ref, lb_ref = rest[:5] if len(rest) > 2 else (None,) * 5
    n_rows, tb, width = v_ref.shape
    n = width // 2
    c = RWKV_CHUNK
    n_chunks = tb // c
    g = n_rows * n_chunks

    @pl.when(pl.program_id(2) == 0)
    def _():
        s_ref[...] = jnp.zeros_like(s_ref)

    load = lambda ref: ref[...].reshape(g, c, width)
    rt, at, bt, kt, v = (load(ref) for ref in (rt_ref, at_ref, bt_ref, kt_ref, v_ref))
    left = lax.broadcasted_iota(jnp.int32, (1, 1, width), 2) < n
    cat = lambda a, b: jnp.concatenate([a, b], axis=1)

    def bd(y):
        y = y.astype(BF16)
        zero = jnp.zeros_like(y)
        return cat(jnp.where(left, y, zero), jnp.where(left, zero, y))

    row = lax.broadcasted_iota(jnp.int32, (c, 2 * c), 0)
    col = lax.broadcasted_iota(jnp.int32, (c, 2 * c), 1) & (c - 1)
    if reverse:
        row, col = col, row
    strict = row > col
    incl = row >= col
    ar = cat(at, rt)
    ab_rb = _bmm_nt(ar, bd(bt))
    ak_rk = _bmm_nt(ar, bd(kt))
    a_ab = jnp.where(strict, ab_rb[:, :c], 0.0)
    a_rb = jnp.where(incl, ab_rb[:, c:], 0.0)
    a_ak = jnp.where(strict, ak_rk[:, :c], 0.0)
    a_rk = jnp.where(incl, ak_rk[:, c:], 0.0)
    v_bd = bd(v)
    av = _bmm(a_ak, v_bd)
    n1 = jnp.where((row >> 3) == (col >> 3), a_ab, 0.0)
    n2 = _bmm(n1, bd(n1))
    n2_bd = bd(n2)
    n4 = _bmm(n2, n2_bd)
    p = (row == col).astype(F32) + n1 + n2 + _bmm(n1, n2_bd)
    inv = p + _bmm(p, bd(n4))
    for level in range(3, c.bit_length() - 1):
        rb, cb = row >> level, col >> level
        off = jnp.where((rb == cb + 1) & ((rb & 1) == 1), a_ab, 0.0)
        inv = inv + _bmm(_bmm(inv, bd(off)), bd(inv))
    w = _bmm(inv, bd(at))
    u0 = _bmm(inv, bd(av))
    q_eff = rt.astype(F32) + _bmm(a_rb, bd(w))
    y0 = _bmm(jnp.concatenate([a_rb, a_rk], axis=2), cat(bd(u0), v_bd))

    per_chunk = lambda t: t.reshape((n_rows, n_chunks) + t.shape[1:])
    qw = per_chunk(cat(q_eff, w))
    u0, y0, v = per_chunk(u0), per_chunk(y0), per_chunk(v)
    g_tot = gt_ref[:, 0][:, :, None, :]
    bk_end = per_chunk(cat(bt, kt)) * g_tot
    state = s_ref[...]
    for i in (range(n_chunks - 1, -1, -1) if reverse else range(n_chunks)):
        tokens = slice(i * c, (i + 1) * c)
        yu = _bmm_nt(qw[:, i], bd(state))
        y = yu[:, :c] + y0[:, i]
        if other_ref is not None:
            y = _rwkv_output(y + other_ref[:, tokens, :], bonus_ref[:, tokens, :], gate_ref[:, tokens, :],
                             lw_ref[...], lb_ref[...])
        y_ref[:, tokens, :] = y.astype(y_ref.dtype)
        full = _bmm_tn(cat(yu[:, c:] + u0[:, i], v[:, i]), bk_end[:, i])
        state = g_tot[:, i] * state + jnp.where(left, full[:, :n], full[:, n:])
    s_ref[...] = state


def _rwkv_output(y, bonus, gate, lnx_w, lnx_b):
    shape = y.shape
    y = y.reshape(-1, shape[-1])
    row = lax.broadcasted_iota(jnp.int32, (shape[-1], shape[-1]), 0)
    col = lax.broadcasted_iota(jnp.int32, (shape[-1], shape[-1]), 1)
    shift = HEAD_DIM.bit_length() - 1
    head_sum = ((row >> shift) == (col >> shift)).astype(F32)
    mu = _sum_cols(y, head_sum) * (1.0 / HEAD_DIM)
    var = _sum_cols(jnp.square(y - mu), head_sum) * (1.0 / HEAD_DIM)
    yn = (y - mu) * lax.rsqrt(var + GN_EPS) * lnx_w + lnx_b
    return (yn.reshape(shape) + bonus) * gate


def rwkv_chunk_scan(streams, v, g_tot, ctx_blocks, reverse, finish=None):
    bsz, n_tok, width = v.shape
    tb = TOKEN_TILE
    n_blocks = n_tok // tb
    pair = 2 * HEAD_DIM

    def block(i):
        if not reverse:
            return i
        return jnp.where(i < ctx_blocks, ctx_blocks - 1 - i, n_blocks - 1 + ctx_blocks - i)

    rows = math.gcd(RWKV_ROWS_PER_STEP, bsz)
    spec = pl.BlockSpec((rows, tb, pair), lambda b, p, i: (b, block(i), p))
    in_specs = [spec] * 5 + [pl.BlockSpec((rows, 1, g_tot.shape[2], pair), lambda b, p, i: (b, block(i), 0, p))]
    args = (*streams, v, g_tot)
    if finish is not None:
        vec = pl.BlockSpec((1, pair), lambda b, p, i: (0, p))
        in_specs += [spec, spec, spec, vec, vec]
        args += tuple(finish)
    return pl.pallas_call(
        functools.partial(_rwkv_chunk_kernel, reverse=reverse),
        grid=(bsz // rows, width // pair, n_blocks),
        in_specs=in_specs,
        out_specs=spec,
        out_shape=jax.ShapeDtypeStruct((bsz, n_tok, width), F32 if finish is None else BF16),
        scratch_shapes=[pltpu.VMEM((rows, HEAD_DIM, pair), F32)],
        compiler_params=_params("parallel", "parallel", "arbitrary"),
        name="rwkv_chunk_scan_bwd" if reverse else "rwkv_chunk_scan_fwd",
    )(*args)


def _rope_tables(n_ctx, n_lat, rot_dim, n_rep):
    t = jnp.arange(n_lat, dtype=jnp.int32)
    rows = (t // GRID_W).astype(F32)
    cols = (t % GRID_W).astype(F32)
    per_axis = rot_dim // 2
    inv_freq = ROPE_THETA ** (-jnp.arange(0, per_axis, 2, dtype=F32) / per_axis)
    ang = jnp.concatenate([rows[:, None] * inv_freq, cols[:, None] * inv_freq], axis=-1)
    ang = jnp.concatenate([jnp.zeros((n_ctx, ang.shape[1]), F32), ang], axis=0)
    sign = jnp.tile(jnp.array([-1.0, 1.0], F32), rot_dim // 2)
    cos = jnp.repeat(jnp.cos(ang), 2, axis=-1)
    sin = jnp.repeat(jnp.sin(ang), 2, axis=-1) * sign
    return jnp.tile(cos, (1, n_rep)), jnp.tile(sin, (1, n_rep))


def _pairs(t):
    return t.reshape((t.shape[0], t.shape[1] // 2, 2) + t.shape[2:])


def _block_diag(m0, m1):
    z01 = jnp.zeros((m0.shape[0], m1.shape[1]), m0.dtype)
    z10 = jnp.zeros((m1.shape[0], m0.shape[1]), m0.dtype)
    return jnp.concatenate([jnp.concatenate([m0, z01], axis=1), jnp.concatenate([z10, m1], axis=1)], axis=0)


def _shift_halo(edge, ctx_tiles):
    zero = jnp.zeros_like(edge[:, :1, 0])
    prev = jnp.concatenate([zero, edge[:, :-1, 1]], axis=1)
    nxt = jnp.concatenate([edge[:, 1:, 0], zero], axis=1)
    tile = jnp.arange(prev.shape[1])[None, :, None]
    prev = jnp.where(tile == ctx_tiles, 0.0, prev)
    nxt = jnp.where(tile == ctx_tiles - 1, 0.0, nxt)
    return jnp.stack([prev, nxt], axis=2)


def _rwkv_mixer(rw, rw_edge, ctx_tiles, shift_taps, w0, w2, a0, a2, k_k, k_a, r_k, g2, lnx_w, lnx_b):
    both = lambda t: t.reshape(1, -1)
    outs = rwkv_prepare(rw, _shift_halo(rw_edge, ctx_tiles), shift_taps, both(w0),
                        _block_diag(w2[0], w2[1]).astype(BF16), both(a0),
                        _block_diag(a2[0], a2[1]).astype(BF16), both(k_k), both(k_a), both(r_k),
                        g2.astype(BF16))
    v, bonus, gate = outs[8], outs[11], outs[12]
    y_f = rwkv_chunk_scan(outs[0:4], v, outs[9], ctx_tiles, False)
    return rwkv_chunk_scan(outs[4:8], v, outs[10], ctx_tiles, True,
                           finish=(y_f, bonus, gate, lnx_w[None], lnx_b[None]))


def _dense_branch(q, k, v, n_ctx, want_ctx):
    n_tok = q.shape[3]
    o_lat = attention(q, k, v, n_ctx, n_tok - n_ctx, n_tok, HEAD_DIM, ATTN_LAT_Q_TILE)
    o_ctx = attention(q, k, v, 0, n_ctx, n_ctx, HEAD_DIM, ATTN_Q_TILE) if want_ctx else None
    return o_lat, o_ctx


def kernel(x, c, ctx, c_ctx, w_mod, b_mod, norm1_g, norm2_g, w_in, rwkv_shift, na_rpb, gqa_q_norm, gqa_k_norm, mla_q_norm, mla_kv_norm, mla_w_uq, mla_w_ukv, rwkv_w0, rwkv_w2, rwkv_a0, rwkv_a2, rwkv_k_k, rwkv_k_a, rwkv_r_k, rwkv_g2, rwkv_lnx_w, rwkv_lnx_b, w_out, w_fc1, w_fc2, final_norm_g):
    bsz, n_lat, d_model = x.shape
    n_ctx = ctx.shape[1]
    depth = w_mod.shape[0]
    rows = n_lat // GRID_W
    ctx_tiles = n_ctx // TOKEN_TILE
    assert n_ctx % TOKEN_TILE == 0 and n_lat % TOKEN_TILE == 0 and rows % NA_Q_ROWS == 0

    tables = (_rope_tables(n_ctx, n_lat, HEAD_DIM, N_HEADS) + _rope_tables(n_ctx, n_lat, MLA_ROPE_DIM, N_HEADS))

    cond = jnp.concatenate([jax.nn.silu(c), jax.nn.silu(c_ctx)[None]], axis=0)
    n_cond = -(-cond.shape[0] // 8) * 8
    cond = jnp.pad(cond, ((0, n_cond - cond.shape[0]), (0, 0)))
    mod_all = modulation(cond, w_mod, b_mod[:, None, :])

    x_all = jnp.concatenate([ctx, x], axis=1)
    bounds = np.cumsum([0, NA_COLS, GQA_COLS, MLA_COLS, RWKV_COLS])
    mla_pad = -MLA_COLS % 128

    for layer in range(depth):
        want_ctx = layer < depth - 1
        mod = mod_all[layer].reshape(n_cond, 6, d_model)
        mod = jnp.stack([jnp.broadcast_to(mod[bsz], (bsz, 6, d_model)), mod[:bsz]], axis=1)

        w_na, w_gqa, w_mla, w_rw = [w_in[layer][:, bounds[i]:bounds[i + 1]].astype(BF16) for i in range(4)]
        w_uq = mla_w_uq[layer].reshape(MLA_Q_RANK, N_HEADS, MLA_QK_DIM)
        w_uq = jnp.concatenate([w_uq[:, :, :HEAD_DIM].reshape(MLA_Q_RANK, -1),
                                w_uq[:, :, HEAD_DIM:].reshape(MLA_Q_RANK, -1)], axis=1)
        w_ukv = mla_w_ukv[layer].reshape(MLA_KV_RANK, N_HEADS, 2 * HEAD_DIM)
        w_ukv = jnp.concatenate([w_ukv[:, :, :HEAD_DIM].reshape(MLA_KV_RANK, -1),
                                 w_ukv[:, :, HEAD_DIM:].reshape(MLA_KV_RANK, -1)], axis=1)
        weights = (w_na, w_gqa, jnp.pad(w_mla, ((0, 0), (0, mla_pad))), w_rw,
                   jnp.tile(gqa_q_norm[layer], N_HEADS)[None], jnp.tile(gqa_k_norm[layer], GQA_KV_HEADS)[None],
                   mla_q_norm[layer][None], mla_kv_norm[layer][None], w_uq.astype(BF16), w_ukv.astype(BF16))
        (a_q, a_k, a_v, b_q, b_k, b_v, c_q, c_k, c_v, rw, rw_edge) = in_projection(
            x_all, mod, norm1_g[layer][None], weights, tables, ctx_tiles)

        a_q, a_k, a_v = _pairs(a_q), _pairs(a_k), _pairs(a_v)
        bias = na_bias_table(na_rpb[layer], rows, n_ctx)
        out_a = neighbourhood_attention(a_q, a_k, a_v, bias.reshape((N_HEADS // 2, 2) + bias.shape[1:]), n_ctx)
        out_ac = attention(a_q, a_k, a_v, 0, n_ctx, n_ctx, HEAD_DIM, ATTN_Q_TILE) if want_ctx else None
        mix_b = _dense_branch(_pairs(b_q), b_k[:, :, None], b_v[:, :, None], n_ctx, want_ctx)
        mix_c = _dense_branch(_pairs(c_q), _pairs(c_k), _pairs(c_v), n_ctx, want_ctx)
        mix_d = _rwkv_mixer(rw, rw_edge, ctx_tiles, rwkv_shift[layer], rwkv_w0[layer], rwkv_w2[layer], rwkv_a0[layer],
                            rwkv_a2[layer], rwkv_k_k[layer], rwkv_k_a[layer], rwkv_r_k[layer],
                            rwkv_g2[layer], rwkv_lnx_w[layer], rwkv_lnx_b[layer])

        x_all = out_projection_mlp(x_all, [(out_a, out_ac), mix_b, mix_c], mix_d, mod, norm2_g[layer][None],
                                   final_norm_g[None], w_out[layer].astype(BF16), w_fc1[layer].astype(BF16),
                                   w_fc2[layer].astype(BF16), ctx_tiles, not want_ctx)
    return x_all
```
